```python
import math
import jax
import jax.numpy as jnp
from jax import lax
import numpy as np

D_MODEL = 1024
BATCH = 16
SEQ = 2048
DEPTH = 1
DEC_BATCH = 2
DEC_SEQ = 16384
PAST_LEN = 128

MIX_WIDTH = D_MODEL
DN_HEADS = 4
DN_HEAD_DIM = 128
DN_WIDTH = DN_HEADS * DN_HEAD_DIM
ATT_HEADS = 8
ATT_HEAD_DIM = 64
ATT_WIDTH = ATT_HEADS * ATT_HEAD_DIM
CONV_K = 3
CHUNK = 64
DILATED_PATTERNS = ((128, 1), (512, 4), (2048, 16))
BAND = 64
NUM_BUCKETS = 32
MAX_DISTANCE = 1024
N_EXPERTS = 16
CAPACITY_FACTOR = 2
EXPERT_D_FF = 2816
EPS = 1e-6
PROJ_WIDTH = 4 * DN_WIDTH + 4 * DN_HEADS + 3 * ATT_WIDTH

kernel_name = 'hybrid_deltanet_dilated_attn_ec_moe_encoder'

F32 = jnp.float32


def rms_norm(x, g):
    xf = x.astype(F32)
    y = xf * lax.rsqrt(jnp.mean(xf * xf, axis=-1, keepdims=True) + EPS)
    return (y * g.astype(F32)).astype(x.dtype)


def l2norm(x):
    return x * lax.rsqrt(jnp.sum(x * x, axis=-1, keepdims=True) + EPS)


def centred_depthwise_conv(x, w):
    c = x.shape[-1]
    pad = (CONV_K - 1) // 2
    return lax.conv_general_dilated(
        x, w[:, None, :].astype(x.dtype), window_strides=(1,),
        padding=[(pad, CONV_K - 1 - pad)],
        dimension_numbers=('NWC', 'WIO', 'NWC'), feature_group_count=c)


def gated_delta_rule_chunked(q, k, v, g, beta):
    b, t, h, dk = q.shape
    dv = v.shape[-1]
    n = t // CHUNK

    def chunks(a):
        a = a.reshape(b, n, CHUNK, h, *a.shape[3:])
        return jnp.moveaxis(a, 3, 1)

    q = chunks(q * (dk ** -0.5))
    k = chunks(k)
    v = chunks(v)
    g = chunks(g)
    beta = chunks(beta)
    G = jnp.cumsum(g, axis=-1)
    tri = jnp.tril(jnp.ones((CHUNK, CHUNK), bool))
    strict = jnp.tril(jnp.ones((CHUNK, CHUNK), bool), -1)
    diff = G[..., :, None] - G[..., None, :]
    decay = jnp.where(tri, jnp.exp(jnp.where(tri, diff, 0.0)), 0.0)
    kbeta = k * beta[..., None]
    lower = jnp.where(strict, jnp.einsum('bhnid,bhnjd->bhnij', kbeta, k) * decay, 0.0)
    eye = jnp.eye(CHUNK, dtype=q.dtype)
    rhs = jnp.concatenate([v * beta[..., None], kbeta * jnp.exp(G)[..., None]], axis=-1)
    sol = lax.linalg.triangular_solve(lower + eye, rhs, left_side=True, lower=True,
                                      unit_diagonal=True)
    u, w = sol[..., :dv], sol[..., dv:]
    attn = jnp.where(tri, jnp.einsum('bhnid,bhnjd->bhnij', q, k) * decay, 0.0)
    q_dec = q * jnp.exp(G)[..., None]
    k_dec = k * jnp.exp(G[..., -1:] - G)[..., None]
    c_dec = jnp.exp(G[..., -1])
    xs = tuple(jnp.moveaxis(a, 2, 0) for a in (q_dec, k_dec, u, w, attn, c_dec))

    def step(state, inp):
        qd, kd, u_c, w_c, a_c, cd = inp
        v_new = u_c - jnp.einsum('bhcd,bhde->bhce', w_c, state)
        o = jnp.einsum('bhcd,bhde->bhce', qd, state) + jnp.einsum('bhij,bhje->bhie', a_c, v_new)
        state = state * cd[..., None, None] + jnp.einsum('bhcd,bhce->bhde', kd, v_new)
        return state, o

    state0 = jnp.zeros((b, h, dk, dv), q.dtype)
    _, o = lax.scan(step, state0, xs)
    return jnp.transpose(o, (1, 0, 3, 2, 4)).reshape(b, t, h, dv)


def gated_deltanet_mixer(qkv, z, b_raw, a_raw, conv_w, a_log_fwd, dt_bias_fwd,
                         a_log_bwd, dt_bias_bwd, dn_norm_g):
    bsz, s, _ = qkv.shape
    qkv = jax.nn.silu(centred_depthwise_conv(qkv, conv_w).astype(F32))
    q, k, v = jnp.split(qkv, 3, axis=-1)
    heads = lambda a: a.reshape(bsz, s, DN_HEADS, DN_HEAD_DIM)
    q = l2norm(heads(q))
    k = l2norm(heads(k))
    v = heads(v)
    b_raw = b_raw.astype(F32)
    a_raw = a_raw.astype(F32)
    beta_f = jax.nn.sigmoid(b_raw[..., :DN_HEADS])
    beta_b = jax.nn.sigmoid(b_raw[..., DN_HEADS:])
    g_f = -jnp.exp(a_log_fwd.astype(F32)) * jax.nn.softplus(a_raw[..., :DN_HEADS] + dt_bias_fwd.astype(F32))
    g_b = -jnp.exp(a_log_bwd.astype(F32)) * jax.nn.softplus(a_raw[..., DN_HEADS:] + dt_bias_bwd.astype(F32))
    flip = lambda a: jnp.flip(a, axis=1)
    o_f = gated_delta_rule_chunked(q, k, v, g_f, beta_f)
    o_b = flip(gated_delta_rule_chunked(flip(q), flip(k), flip(v), flip(g_b), flip(beta_b)))
    o = o_f + o_b
    o = o * lax.rsqrt(jnp.mean(o * o, axis=-1, keepdims=True) + EPS) * dn_norm_g.astype(F32)
    o = o * jax.nn.silu(heads(z.astype(F32)))
    return o.reshape(bsz, s, DN_WIDTH)


def t5_bucket(rel):
    nb = NUM_BUCKETS // 2
    ret = jnp.where(rel > 0, nb, 0)
    n = jnp.abs(rel)
    max_exact = nb // 2
    nf = jnp.maximum(n, 1).astype(F32)
    large = max_exact + (jnp.log(nf / max_exact) / math.log(MAX_DISTANCE / max_exact)
                         * (nb - max_exact)).astype(jnp.int32)
    large = jnp.minimum(large, nb - 1)
    return ret + jnp.where(n < max_exact, n, large)


def dilated_window_branch(q, k, v, rel_bias, window, dil):
    bsz, s, h, hd = q.shape
    L = s // dil
    half = window // (2 * dil)
    nb = -(-L // BAND)
    Lp = nb * BAND
    bp = bsz * dil

    def to_res(a):
        return a.reshape(bsz, L, dil, h, hd).transpose(0, 2, 1, 3, 4).reshape(bp, L, h, hd)

    qr, kr, vr = to_res(q), to_res(k), to_res(v)
    qb = jnp.pad(qr, ((0, 0), (0, Lp - L), (0, 0), (0, 0))).reshape(bp, nb, BAND, h, hd)
    kv_pad = ((0, 0), (BAND, Lp - L + BAND), (0, 0), (0, 0))

    def key_blocks(a):
        ab = jnp.pad(a, kv_pad).reshape(bp, nb + 2, BAND, h, hd)
        return jnp.concatenate([ab[:, :-2], ab[:, 1:-1], ab[:, 2:]], axis=2)

    kb, vb = key_blocks(kr), key_blocks(vr)
    logits = jnp.einsum('bnqhd,bnkhd->bnhqk', qb, kb) * (ATT_HEAD_DIM ** -0.5)
    q_off = jnp.arange(BAND)
    k_off = jnp.arange(3 * BAND) - BAND
    delta = k_off[None, :] - q_off[:, None]
    bias = rel_bias.astype(F32)[t5_bucket(delta * dil)].transpose(2, 0, 1)
    key_pos = jnp.arange(nb)[:, None] * BAND + k_off[None, :]
    mask = (jnp.abs(delta) <= half)[None] & ((key_pos >= 0) & (key_pos < L))[:, None, :]
    logits = jnp.where(mask[None, :, None], logits + bias[None, None], -jnp.inf)
    m = jnp.max(logits, axis=-1, keepdims=True)
    p = jnp.exp(logits - m)
    den = jnp.sum(p, axis=-1, keepdims=True)
    o = jnp.einsum('bnhqk,bnkhd->bnqhd', p / den, vb)
    lse = (m + jnp.log(den))[..., 0].transpose(0, 1, 3, 2)

    def from_res(a):
        rest = a.shape[3:]
        a = a.reshape(bp, Lp, *rest)[:, :L]
        return a.reshape(bsz, dil, L, *rest).swapaxes(1, 2).reshape(bsz, s, *rest)

    return from_res(o), from_res(lse)


def dilated_mixture_attention(q, k, v, rel_bias):
    outs, lses = [], []
    for window, dil in DILATED_PATTERNS:
        o, lse = dilated_window_branch(q, k, v, rel_bias, window, dil)
        outs.append(o)
        lses.append(lse)
    wts = jax.nn.softmax(jnp.stack(lses, axis=0), axis=0)
    return jnp.sum(wts[..., None] * jnp.stack(outs, axis=0), axis=0)


def expert_choice_ffn(h, w_router, w_gate, w_up, w_down):
    bsz, s, d = h.shape
    n_tok = bsz * s
    cap = CAPACITY_FACTOR * n_tok // N_EXPERTS
    hf = h.reshape(n_tok, d)
    aff = jax.nn.softmax(jnp.einsum('nd,de->ne', hf.astype(F32), w_router.astype(F32)), axis=-1)
    gate, idx = lax.top_k(aff.T, cap)
    xe = hf[idx]
    a = jnp.einsum('ecd,edf->ecf', xe, w_gate)
    b = jnp.einsum('ecd,edf->ecf', xe, w_up)
    ye = jnp.einsum('ecf,efd->ecd', jax.nn.silu(a) * b, w_down)
    contrib = (gate[..., None].astype(ye.dtype) * ye).reshape(-1, d)
    out = jnp.zeros((n_tok, d), ye.dtype).at[idx.reshape(-1)].add(contrib)
    return out.reshape(bsz, s, d)


def encoder_layer(x, rel_bias, norm1_g, w_in, conv_w, a_log_fwd, dt_bias_fwd, a_log_bwd,
                  dt_bias_bwd, dn_norm_g, w_out, norm2_g, w_router, w_gate, w_up, w_down):
    bsz, s, _ = x.shape
    h = rms_norm(x, norm1_g)
    proj = jnp.einsum('bsd,dp->bsp', h, w_in)
    o_qkv = 3 * DN_WIDTH
    o_z = o_qkv + DN_WIDTH
    o_beta = o_z + 2 * DN_HEADS
    o_a = o_beta + 2 * DN_HEADS
    dn_out = gated_deltanet_mixer(proj[..., :o_qkv], proj[..., o_qkv:o_z], proj[..., o_z:o_beta],
                                  proj[..., o_beta:o_a], conv_w, a_log_fwd, dt_bias_fwd,
                                  a_log_bwd, dt_bias_bwd, dn_norm_g)
    att_q, att_k, att_v = jnp.split(proj[..., o_a:].astype(F32), 3, axis=-1)
    heads = lambda a: a.reshape(bsz, s, ATT_HEADS, ATT_HEAD_DIM)
    att_out = dilated_mixture_attention(heads(att_q), heads(att_k), heads(att_v), rel_bias)
    mixed = jnp.concatenate([dn_out, att_out.reshape(bsz, s, ATT_WIDTH)], axis=-1).astype(x.dtype)
    x = x + jnp.einsum('bsm,md->bsd', mixed, w_out)
    x = x + expert_choice_ffn(rms_norm(x, norm2_g), w_router, w_gate, w_up, w_down).astype(x.dtype)
    return x


def setup_inputs(seed: int = 0) -> dict:
    key = jax.random.key(seed)
    ks = jax.random.split(key, 20)
    nrm = lambda k, shape, scale: jax.random.normal(k, shape, F32) * scale
    return {
        'x_prompt': nrm(ks[0], (BATCH, SEQ, D_MODEL), 1.0),
        'x_sample': nrm(ks[1], (DEC_BATCH, DEC_SEQ, D_MODEL), 1.0),
        'rel_bias': nrm(ks[2], (NUM_BUCKETS, ATT_HEADS), 0.5),
        'norm1_g': 1.0 + nrm(ks[3], (DEPTH, D_MODEL), 0.02),
        'w_in': nrm(ks[4], (DEPTH, D_MODEL, PROJ_WIDTH), D_MODEL ** -0.5),
        'conv_w': nrm(ks[5], (DEPTH, CONV_K, 3 * DN_WIDTH), CONV_K ** -0.5),
        'a_log_fwd': jnp.log(jax.random.uniform(ks[6], (DEPTH, DN_HEADS), F32, 1.0, 16.0)),
        'dt_bias_fwd': 1.0 + nrm(ks[7], (DEPTH, DN_HEADS), 0.1),
        'a_log_bwd': jnp.log(jax.random.uniform(ks[8], (DEPTH, DN_HEADS), F32, 1.0, 16.0)),
        'dt_bias_bwd': 1.0 + nrm(ks[9], (DEPTH, DN_HEADS), 0.1),
        'dn_norm_g': 1.0 + nrm(ks[10], (DEPTH, DN_HEAD_DIM), 0.02),
        'w_out': nrm(ks[11], (DEPTH, MIX_WIDTH, D_MODEL), MIX_WIDTH ** -0.5),
        'norm2_g': 1.0 + nrm(ks[12], (DEPTH, D_MODEL), 0.02),
        'w_router': nrm(ks[13], (DEPTH, D_MODEL, N_EXPERTS), D_MODEL ** -0.5),
        'w_gate': nrm(ks[14], (DEPTH, N_EXPERTS, D_MODEL, EXPERT_D_FF), D_MODEL ** -0.5),
        'w_up': nrm(ks[15], (DEPTH, N_EXPERTS, D_MODEL, EXPERT_D_FF), D_MODEL ** -0.5),
        'w_down': nrm(ks[16], (DEPTH, N_EXPERTS, EXPERT_D_FF, D_MODEL), EXPERT_D_FF ** -0.5),
        'final_norm_g': 1.0 + nrm(ks[17], (D_MODEL,), 0.02),
    }


def reference(x_prompt, x_sample, rel_bias, norm1_g, w_in, conv_w, a_log_fwd, dt_bias_fwd,
              a_log_bwd, dt_bias_bwd, dn_norm_g, w_out, norm2_g, w_router, w_gate, w_up,
              w_down, final_norm_g):
    def trunk(x):
        for l in range(DEPTH):
            x = encoder_layer(x, rel_bias, norm1_g[l], w_in[l], conv_w[l], a_log_fwd[l],
                              dt_bias_fwd[l], a_log_bwd[l], dt_bias_bwd[l], dn_norm_g[l],
                              w_out[l], norm2_g[l], w_router[l], w_gate[l], w_up[l], w_down[l])
        return rms_norm(x, final_norm_g)

    y_prompt = trunk(x_prompt)
    y_sample = trunk(x_sample)
    return (y_prompt, y_sample)
```

```python
import functools
import math

import jax
import jax.numpy as jnp
from jax import lax
from jax.experimental import pallas as pl
from jax.experimental.pallas import tpu as pltpu

D_MODEL = 1024
DN_HEADS = 4
DN_HEAD_DIM = 128
DN_WIDTH = DN_HEADS * DN_HEAD_DIM
ATT_HEADS = 8
ATT_HEAD_DIM = 64
ATT_WIDTH = ATT_HEADS * ATT_HEAD_DIM
CONV_K = 3
CHUNK = 64
DILATED_PATTERNS = ((128, 1), (512, 4), (2048, 16))
BAND = 64
NUM_BUCKETS = 32
MAX_DISTANCE = 1024
N_EXPERTS = 16
CAPACITY_FACTOR = 2
EXPERT_D_FF = 2816
EPS = 1e-6
GATE_COLS = 4 * DN_HEADS
LANE = 128

F32 = jnp.float32
BF16 = jnp.bfloat16

VMEM_LIMIT = 56 * 1024 * 1024


def _params(*sem):
    return pltpu.CompilerParams(dimension_semantics=sem, vmem_limit_bytes=VMEM_LIMIT)


def _in_proj_kernel(x_ref, g_ref, wqkv_ref, wz_ref, wba_ref, watt_ref,
                    qkv_ref, z_ref, ba_ref, att_ref):
    x = x_ref[...]
    h = x * lax.rsqrt(jnp.mean(x * x, axis=-1, keepdims=True) + EPS) * g_ref[...]
    hb = h.astype(BF16)
    qkv_ref[...] = jnp.dot(hb, wqkv_ref[...], preferred_element_type=F32)
    z_ref[...] = jnp.dot(hb, wz_ref[...], preferred_element_type=F32)
    ba_ref[...] = jnp.dot(hb, wba_ref[...], preferred_element_type=F32)
    att_ref[...] = jnp.dot(hb, watt_ref[...], preferred_element_type=F32)


def _in_proj(x, g, w_in, tm=512):
    n = x.shape[0]
    o_qkv = 3 * DN_WIDTH
    o_z = o_qkv + DN_WIDTH
    o_ba = o_z + GATE_COLS
    wb = w_in.astype(BF16)
    wqkv = wb[:, :o_qkv]
    wz = wb[:, o_qkv:o_z]
    wba = jnp.pad(wb[:, o_z:o_ba], ((0, 0), (0, LANE - GATE_COLS)))
    watt = wb[:, o_ba:]
    full = lambda a: pl.BlockSpec(a.shape, lambda i: (0, 0))
    row = lambda w: pl.BlockSpec((tm, w), lambda i: (i, 0))
    return pl.pallas_call(
        _in_proj_kernel,
        grid=(n // tm,),
        in_specs=[row(D_MODEL), full(g), full(wqkv), full(wz), full(wba), full(watt)],
        out_specs=[row(o_qkv), row(DN_WIDTH), row(LANE), row(3 * ATT_WIDTH)],
        out_shape=[jax.ShapeDtypeStruct((n, o_qkv), F32),
                   jax.ShapeDtypeStruct((n, DN_WIDTH), F32),
                   jax.ShapeDtypeStruct((n, LANE), F32),
                   jax.ShapeDtypeStruct((n, 3 * ATT_WIDTH), F32)],
        compiler_params=_params("parallel"),
        name="in_proj",
    )(x, g, wqkv, wz, wba, watt)


def _out_proj_kernel(m_ref, x_ref, w_ref, g_ref, x1_ref, h2_ref):
    x1 = x_ref[...] + jnp.dot(m_ref[...].astype(BF16), w_ref[...], preferred_element_type=F32)
    x1_ref[...] = x1
    h2_ref[...] = x1 * lax.rsqrt(jnp.mean(x1 * x1, axis=-1, keepdims=True) + EPS) * g_ref[...]


def _out_proj(mixed, x, w_out, g, tm=512):
    n = x.shape[0]
    wb = w_out.astype(BF16)
    full = lambda a: pl.BlockSpec(a.shape, lambda i: (0, 0))
    row = pl.BlockSpec((tm, D_MODEL), lambda i: (i, 0))
    return pl.pallas_call(
        _out_proj_kernel,
        grid=(n // tm,),
        in_specs=[row, row, full(wb), full(g)],
        out_specs=[row, row],
        out_shape=[jax.ShapeDtypeStruct((n, D_MODEL), F32)] * 2,
        compiler_params=_params("parallel"),
        name="out_proj",
    )(mixed, x, wb, g)


def _ffn_kernel(xe_ref, wg_ref, wu_ref, wd_ref, ye_ref, acc_ref):
    f = pl.program_id(2)

    @pl.when(f == 0)
    def _():
        acc_ref[...] = jnp.zeros_like(acc_ref)

    xe = xe_ref[0]
    a = jnp.dot(xe, wg_ref[0].astype(BF16), preferred_element_type=F32)
    b = jnp.dot(xe, wu_ref[0].astype(BF16), preferred_element_type=F32)
    h = (a * jax.nn.sigmoid(a) * b).astype(BF16)
    acc_ref[...] += jnp.dot(h, wd_ref[0].astype(BF16), preferred_element_type=F32)

    @pl.when(f == pl.num_programs(2) - 1)
    def _():
        ye_ref[0] = acc_ref[...]


def _expert_ffn(xe, w_gate, w_up, w_down, tm=1024, tf=256):
    e, rows, d = xe.shape
    dff = w_gate.shape[-1]
    return pl.pallas_call(
        _ffn_kernel,
        grid=(e, rows // tm, dff // tf),
        in_specs=[pl.BlockSpec((1, tm, d), lambda i, r, f: (i, r, 0)),
                  pl.BlockSpec((1, d, tf), lambda i, r, f: (i, 0, f)),
                  pl.BlockSpec((1, d, tf), lambda i, r, f: (i, 0, f)),
                  pl.BlockSpec((1, tf, d), lambda i, r, f: (i, f, 0))],
        out_specs=pl.BlockSpec((1, tm, d), lambda i, r, f: (i, r, 0)),
        out_shape=jax.ShapeDtypeStruct((e, rows, d), F32),
        scratch_shapes=[pltpu.VMEM((tm, d), F32)],
        compiler_params=_params("parallel", "parallel", "arbitrary"),
        name="expert_ffn",
    )(xe, w_gate, w_up, w_down)


def _final_kernel(x1_ref, m_ref, g_ref, y_ref):
    x = x1_ref[...] + m_ref[...]
    y_ref[...] = x * lax.rsqrt(jnp.mean(x * x, axis=-1, keepdims=True) + EPS) * g_ref[...]


def _final_norm(x1, moe, g, tm=1024):
    n = x1.shape[0]
    row = pl.BlockSpec((tm, D_MODEL), lambda i: (i, 0))
    return pl.pallas_call(
        _final_kernel,
        grid=(n // tm,),
        in_specs=[row, row, pl.BlockSpec(g.shape, lambda i: (0, 0))],
        out_specs=row,
        out_shape=jax.ShapeDtypeStruct((n, D_MODEL), F32),
        compiler_params=_params("parallel"),
        name="final_norm",
    )(x1, moe, g)


def _l2norm(x):
    return x * lax.rsqrt(jnp.sum(x * x, axis=-1, keepdims=True) + EPS)


def _centred_depthwise_conv(x, w):
    c = x.shape[-1]
    pad = (CONV_K - 1) // 2
    return lax.conv_general_dilated(
        x, w[:, None, :].astype(x.dtype), window_strides=(1,),
        padding=[(pad, CONV_K - 1 - pad)],
        dimension_numbers=('NWC', 'WIO', 'NWC'), feature_group_count=c)


def _gated_delta_rule_chunked(q, k, v, g, beta):
    b, t, h, dk = q.shape
    dv = v.shape[-1]
    n = t // CHUNK

    def chunks(a):
        a = a.reshape(b, n, CHUNK, h, *a.shape[3:])
        return jnp.moveaxis(a, 3, 1)

    q = chunks(q * (dk ** -0.5))
    k = chunks(k)
    v = chunks(v)
    g = chunks(g)
    beta = chunks(beta)
    G = jnp.cumsum(g, axis=-1)
    tri = jnp.tril(jnp.ones((CHUNK, CHUNK), bool))
    strict = jnp.tril(jnp.ones((CHUNK, CHUNK), bool), -1)
    diff = G[..., :, None] - G[..., None, :]
    decay = jnp.where(tri, jnp.exp(jnp.where(tri, diff, 0.0)), 0.0)
    kbeta = k * beta[..., None]
    lower = jnp.where(strict, jnp.einsum('bhnid,bhnjd->bhnij', kbeta, k) * decay, 0.0)
    eye = jnp.eye(CHUNK, dtype=q.dtype)
    rhs = jnp.concatenate([v * beta[..., None], kbeta * jnp.exp(G)[..., None]], axis=-1)
    sol = lax.linalg.triangular_solve(lower + eye, rhs, left_side=True, lower=True,
                                      unit_diagonal=True)
    u, w = sol[..., :dv], sol[..., dv:]
    attn = jnp.where(tri, jnp.einsum('bhnid,bhnjd->bhnij', q, k) * decay, 0.0)
    q_dec = q * jnp.exp(G)[..., None]
    k_dec = k * jnp.exp(G[..., -1:] - G)[..., None]
    c_dec = jnp.exp(G[..., -1])
    xs = tuple(jnp.moveaxis(a, 2, 0) for a in (q_dec, k_dec, u, w, attn, c_dec))

    def step(state, inp):
        qd, kd, u_c, w_c, a_c, cd = inp
        v_new = u_c - jnp.einsum('bhcd,bhde->bhce', w_c, state)
        o = jnp.einsum('bhcd,bhde->bhce', qd, state) + jnp.einsum('bhij,bhje->bhie', a_c, v_new)
        state = state * cd[..., None, None] + jnp.einsum('bhcd,bhce->bhde', kd, v_new)
        return state, o

    state0 = jnp.zeros((b, h, dk, dv), q.dtype)
    _, o = lax.scan(step, state0, xs)
    return jnp.transpose(o, (1, 0, 3, 2, 4)).reshape(b, t, h, dv)


def _gated_deltanet_mixer(qkv, z, b_raw, a_raw, conv_w, a_log_fwd, dt_bias_fwd,
                          a_log_bwd, dt_bias_bwd, dn_norm_g):
    bsz, s, _ = qkv.shape
    qkv = jax.nn.silu(_centred_depthwise_conv(qkv, conv_w).astype(F32))
    q, k, v = jnp.split(qkv, 3, axis=-1)
    heads = lambda a: a.reshape(bsz, s, DN_HEADS, DN_HEAD_DIM)
    q = _l2norm(heads(q))
    k = _l2norm(heads(k))
    v = heads(v)
    beta_f = jax.nn.sigmoid(b_raw[..., :DN_HEADS])
    beta_b = jax.nn.sigmoid(b_raw[..., DN_HEADS:])
    g_f = -jnp.exp(a_log_fwd) * jax.nn.softplus(a_raw[..., :DN_HEADS] + dt_bias_fwd)
    g_b = -jnp.exp(a_log_bwd) * jax.nn.softplus(a_raw[..., DN_HEADS:] + dt_bias_bwd)
    flip = lambda a: jnp.flip(a, axis=1)
    o_f = _gated_delta_rule_chunked(q, k, v, g_f, beta_f)
    o_b = flip(_gated_delta_rule_chunked(flip(q), flip(k), flip(v), flip(g_b), flip(beta_b)))
    o = o_f + o_b
    o = o * lax.rsqrt(jnp.mean(o * o, axis=-1, keepdims=True) + EPS) * dn_norm_g
    o = o * jax.nn.silu(heads(z))
    return o.reshape(bsz, s, DN_WIDTH)


def _t5_bucket(rel):
    nb = NUM_BUCKETS // 2
    ret = jnp.where(rel > 0, nb, 0)
    n = jnp.abs(rel)
    max_exact = nb // 2
    nf = jnp.maximum(n, 1).astype(F32)
    large = max_exact + (jnp.log(nf / max_exact) / math.log(MAX_DISTANCE / max_exact)
                         * (nb - max_exact)).astype(jnp.int32)
    large = jnp.minimum(large, nb - 1)
    return ret + jnp.where(n < max_exact, n, large)


def _dilated_window_branch(q, k, v, rel_bias, window, dil):
    bsz, s, h, hd = q.shape
    L = s // dil
    half = window // (2 * dil)
    nb = -(-L // BAND)
    Lp = nb * BAND
    bp = bsz * dil

    def to_res(a):
        return a.reshape(bsz, L, dil, h, hd).transpose(0, 2, 1, 3, 4).reshape(bp, L, h, hd)

    qr, kr, vr = to_res(q), to_res(k), to_res(v)
    qb = jnp.pad(qr, ((0, 0), (0, Lp - L), (0, 0), (0, 0))).reshape(bp, nb, BAND, h, hd)
    kv_pad = ((0, 0), (BAND, Lp - L + BAND), (0, 0), (0, 0))

    def key_blocks(a):
        ab = jnp.pad(a, kv_pad).reshape(bp, nb + 2, BAND, h, hd)
        return jnp.concatenate([ab[:, :-2], ab[:, 1:-1], ab[:, 2:]], axis=2)

    kb, vb = key_blocks(kr), key_blocks(vr)
    logits = jnp.einsum('bnqhd,bnkhd->bnhqk', qb, kb) * (ATT_HEAD_DIM ** -0.5)
    q_off = jnp.arange(BAND)
    k_off = jnp.arange(3 * BAND) - BAND
    delta = k_off[None, :] - q_off[:, None]
    bias = rel_bias.astype(F32)[_t5_bucket(delta * dil)].transpose(2, 0, 1)
    key_pos = jnp.arange(nb)[:, None] * BAND + k_off[None, :]
    mask = (jnp.abs(delta) <= half)[None] & ((key_pos >= 0) & (key_pos < L))[:, None, :]
    logits = jnp.where(mask[None, :, None], logits + bias[None, None], -jnp.inf)
    m = jnp.max(logits, axis=-1, keepdims=True)
    p = jnp.exp(logits - m)
    den = jnp.sum(p, axis=-1, keepdims=True)
    o = jnp.einsum('bnhqk,bnkhd->bnqhd', p / den, vb)
    lse = (m + jnp.log(den))[..., 0].transpose(0, 1, 3, 2)

    def from_res(a):
        rest = a.shape[3:]
        a = a.reshape(bp, Lp, *rest)[:, :L]
        return a.reshape(bsz, dil, L, *rest).swapaxes(1, 2).reshape(bsz, s, *rest)

    return from_res(o), from_res(lse)


def _dilated_mixture_attention(q, k, v, rel_bias):
    outs, lses = [], []
    for window, dil in DILATED_PATTERNS:
        o, lse = _dilated_window_branch(q, k, v, rel_bias, window, dil)
        outs.append(o)
        lses.append(lse)
    wts = jax.nn.softmax(jnp.stack(lses, axis=0), axis=0)
    return jnp.sum(wts[..., None] * jnp.stack(outs, axis=0), axis=0)


def _mix(x, rel_bias, norm1_g, w_in, conv_w, a_log_fwd, dt_bias_fwd, a_log_bwd, dt_bias_bwd,
         dn_norm_g, w_out, norm2_g):
    bsz, s, d = x.shape
    xf = x.reshape(bsz * s, d)
    qkv, z, ba, att = _in_proj(xf, norm1_g[None], w_in)
    shp = lambda a: a.reshape(bsz, s, a.shape[-1])
    ba = shp(ba)
    dn_out = _gated_deltanet_mixer(shp(qkv), shp(z), ba[..., :2 * DN_HEADS],
                                   ba[..., 2 * DN_HEADS:GATE_COLS], conv_w, a_log_fwd,
                                   dt_bias_fwd, a_log_bwd, dt_bias_bwd, dn_norm_g)
    att_q, att_k, att_v = jnp.split(shp(att), 3, axis=-1)
    heads = lambda a: a.reshape(bsz, s, ATT_HEADS, ATT_HEAD_DIM)
    att_out = _dilated_mixture_attention(heads(att_q), heads(att_k), heads(att_v), rel_bias)
    mixed = jnp.concatenate([dn_out, att_out.reshape(bsz, s, ATT_WIDTH)], axis=-1)
    x1, h2 = _out_proj(mixed.reshape(bsz * s, d), xf, w_out, norm2_g[None])
    return x1, h2


def _route(h2, w_router):
    n_tok = h2.shape[0]
    cap = CAPACITY_FACTOR * n_tok // N_EXPERTS
    aff = jax.nn.softmax(jnp.einsum('nd,de->ne', h2, w_router,
                                    precision=lax.Precision.HIGHEST), axis=-1)
    gate, idx = lax.top_k(aff.T, cap)
    return gate, idx


def kernel(x_prompt, x_sample, rel_bias, norm1_g, w_in, conv_w, a_log_fwd, dt_bias_fwd, a_log_bwd,
           dt_bias_bwd, dn_norm_g, w_out, norm2_g, w_router, w_gate, w_up, w_down, final_norm_g):
    groups = (x_prompt, x_sample)
    x1s, gates, idxs, xes = [], [], [], []
    for x in groups:
        x1, h2 = _mix(x, rel_bias, norm1_g[0], w_in[0], conv_w[0], a_log_fwd[0], dt_bias_fwd[0],
                      a_log_bwd[0], dt_bias_bwd[0], dn_norm_g[0], w_out[0], norm2_g[0])
        gate, idx = _route(h2, w_router[0])
        x1s.append(x1)
        gates.append(gate)
        idxs.append(idx)
        xes.append(h2.astype(BF16)[idx])
    xe = jnp.concatenate(xes, axis=1)
    ye = _expert_ffn(xe, w_gate[0], w_up[0], w_down[0])
    outs = []
    off = 0
    for x, x1, gate, idx in zip(groups, x1s, gates, idxs):
        cap = idx.shape[1]
        contrib = (gate[..., None] * ye[:, off:off + cap]).reshape(-1, D_MODEL)
        moe = jnp.zeros_like(x1).at[idx.reshape(-1)].add(contrib)
        outs.append(_final_norm(x1, moe, final_norm_g[None]).reshape(x.shape))
        off += cap
    return tuple(outs)
```

```python
import functools
import math

import jax
import jax.numpy as jnp
from jax import lax
from jax.experimental import pallas as pl
from jax.experimental.pallas import tpu as pltpu

D_MODEL = 1024
DN_HEADS = 4
DN_HEAD_DIM = 128
DN_WIDTH = DN_HEADS * DN_HEAD_DIM
ATT_HEADS = 8
ATT_HEAD_DIM = 64
ATT_WIDTH = ATT_HEADS * ATT_HEAD_DIM
CONV_K = 3
CHUNK = 64
DILATED_PATTERNS = ((128, 1), (512, 4), (2048, 16))
BAND = 64
NUM_BUCKETS = 32
MAX_DISTANCE = 1024
N_EXPERTS = 16
CAPACITY_FACTOR = 2
EXPERT_D_FF = 2816
EPS = 1e-6
GATE_COLS = 4 * DN_HEADS
LANE = 128

F32 = jnp.float32
BF16 = jnp.bfloat16

VMEM_LIMIT = 56 * 1024 * 1024


def _params(*sem):
    return pltpu.CompilerParams(dimension_semantics=sem, vmem_limit_bytes=VMEM_LIMIT)


def _in_proj_kernel(x_ref, g_ref, wqkv_ref, wz_ref, wba_ref, watt_ref,
                    qkv_ref, z_ref, ba_ref, att_ref):
    x = x_ref[...]
    h = x * lax.rsqrt(jnp.mean(x * x, axis=-1, keepdims=True) + EPS) * g_ref[...]
    hb = h.astype(BF16)
    qkv_ref[...] = jnp.dot(hb, wqkv_ref[...], preferred_element_type=F32)
    z_ref[...] = jnp.dot(hb, wz_ref[...], preferred_element_type=F32)
    ba_ref[...] = jnp.dot(hb, wba_ref[...], preferred_element_type=F32)
    att_ref[...] = jnp.dot(hb, watt_ref[...], preferred_element_type=F32)


def _in_proj(x, g, w_in, tm=512):
    n = x.shape[0]
    o_qkv = 3 * DN_WIDTH
    o_z = o_qkv + DN_WIDTH
    o_ba = o_z + GATE_COLS
    wb = w_in.astype(BF16)
    wqkv = wb[:, :o_qkv]
    wz = wb[:, o_qkv:o_z]
    wba = jnp.pad(wb[:, o_z:o_ba], ((0, 0), (0, LANE - GATE_COLS)))
    watt = wb[:, o_ba:]
    full = lambda a: pl.BlockSpec(a.shape, lambda i: (0, 0))
    row = lambda w: pl.BlockSpec((tm, w), lambda i: (i, 0))
    return pl.pallas_call(
        _in_proj_kernel,
        grid=(n // tm,),
        in_specs=[row(D_MODEL), full(g), full(wqkv), full(wz), full(wba), full(watt)],
        out_specs=[row(o_qkv), row(DN_WIDTH), row(LANE), row(3 * ATT_WIDTH)],
        out_shape=[jax.ShapeDtypeStruct((n, o_qkv), F32),
                   jax.ShapeDtypeStruct((n, DN_WIDTH), F32),
                   jax.ShapeDtypeStruct((n, LANE), F32),
                   jax.ShapeDtypeStruct((n, 3 * ATT_WIDTH), F32)],
        compiler_params=_params("parallel"),
        name="in_proj",
    )(x, g, wqkv, wz, wba, watt)


def _out_proj_kernel(dn_ref, att_ref, x_ref, wdn_ref, watt_ref, g_ref, x1_ref, h2_ref):
    x1 = (x_ref[...]
          + jnp.dot(dn_ref[...].astype(BF16), wdn_ref[...], preferred_element_type=F32)
          + jnp.dot(att_ref[...].astype(BF16), watt_ref[...], preferred_element_type=F32))
    x1_ref[...] = x1
    h2_ref[...] = x1 * lax.rsqrt(jnp.mean(x1 * x1, axis=-1, keepdims=True) + EPS) * g_ref[...]


def _out_proj(dn, att, x, w_out, g, tm=512):
    n = x.shape[0]
    wb = w_out.astype(BF16)
    wdn, watt = wb[:DN_WIDTH], wb[DN_WIDTH:]
    full = lambda a: pl.BlockSpec(a.shape, lambda i: (0, 0))
    row = lambda w: pl.BlockSpec((tm, w), lambda i: (i, 0))
    return pl.pallas_call(
        _out_proj_kernel,
        grid=(n // tm,),
        in_specs=[row(DN_WIDTH), row(ATT_WIDTH), row(D_MODEL), full(wdn), full(watt), full(g)],
        out_specs=[row(D_MODEL), row(D_MODEL)],
        out_shape=[jax.ShapeDtypeStruct((n, D_MODEL), F32)] * 2,
        compiler_params=_params("parallel"),
        name="out_proj",
    )(dn, att, x, wdn, watt, g)


def _ffn_kernel(xe_ref, wg_ref, wu_ref, wd_ref, ye_ref, acc_ref):
    f = pl.program_id(2)

    @pl.when(f == 0)
    def _():
        acc_ref[...] = jnp.zeros_like(acc_ref)

    xe = xe_ref[0]
    a = jnp.dot(xe, wg_ref[0].astype(BF16), preferred_element_type=F32)
    b = jnp.dot(xe, wu_ref[0].astype(BF16), preferred_element_type=F32)
    h = (a * jax.nn.sigmoid(a) * b).astype(BF16)
    acc_ref[...] += jnp.dot(h, wd_ref[0].astype(BF16), preferred_element_type=F32)

    @pl.when(f == pl.num_programs(2) - 1)
    def _():
        ye_ref[0] = acc_ref[...]


def _expert_ffn(xe, w_gate, w_up, w_down, tm=1024, tf=256):
    e, rows, d = xe.shape
    dff = w_gate.shape[-1]
    return pl.pallas_call(
        _ffn_kernel,
        grid=(e, rows // tm, dff // tf),
        in_specs=[pl.BlockSpec((1, tm, d), lambda i, r, f: (i, r, 0)),
                  pl.BlockSpec((1, d, tf), lambda i, r, f: (i, 0, f)),
                  pl.BlockSpec((1, d, tf), lambda i, r, f: (i, 0, f)),
                  pl.BlockSpec((1, tf, d), lambda i, r, f: (i, f, 0))],
        out_specs=pl.BlockSpec((1, tm, d), lambda i, r, f: (i, r, 0)),
        out_shape=jax.ShapeDtypeStruct((e, rows, d), F32),
        scratch_shapes=[pltpu.VMEM((tm, d), F32)],
        compiler_params=_params("parallel", "parallel", "arbitrary"),
        name="expert_ffn",
    )(xe, w_gate, w_up, w_down)


def _final_kernel(x1_ref, m_ref, g_ref, y_ref):
    x = x1_ref[...] + m_ref[...]
    y_ref[...] = x * lax.rsqrt(jnp.mean(x * x, axis=-1, keepdims=True) + EPS) * g_ref[...]


def _final_norm(x1, moe, g, tm=1024):
    n = x1.shape[0]
    row = pl.BlockSpec((tm, D_MODEL), lambda i: (i, 0))
    return pl.pallas_call(
        _final_kernel,
        grid=(n // tm,),
        in_specs=[row, row, pl.BlockSpec(g.shape, lambda i: (0, 0))],
        out_specs=row,
        out_shape=jax.ShapeDtypeStruct((n, D_MODEL), F32),
        compiler_params=_params("parallel"),
        name="final_norm",
    )(x1, moe, g)


def _l2norm(x):
    return x * lax.rsqrt(jnp.sum(x * x, axis=-1, keepdims=True) + EPS)


def _centred_depthwise_conv(x, w):
    c = x.shape[-1]
    pad = (CONV_K - 1) // 2
    return lax.conv_general_dilated(
        x, w[:, None, :].astype(x.dtype), window_strides=(1,),
        padding=[(pad, CONV_K - 1 - pad)],
        dimension_numbers=('NWC', 'WIO', 'NWC'), feature_group_count=c)


def _gated_delta_rule_chunked(q, k, v, g, beta):
    b, t, h, dk = q.shape
    dv = v.shape[-1]
    n = t // CHUNK

    def chunks(a):
        a = a.reshape(b, n, CHUNK, h, *a.shape[3:])
        return jnp.moveaxis(a, 3, 1)

    q = chunks(q * (dk ** -0.5))
    k = chunks(k)
    v = chunks(v)
    g = chunks(g)
    beta = chunks(beta)
    G = jnp.cumsum(g, axis=-1)
    tri = jnp.tril(jnp.ones((CHUNK, CHUNK), bool))
    strict = jnp.tril(jnp.ones((CHUNK, CHUNK), bool), -1)
    diff = G[..., :, None] - G[..., None, :]
    decay = jnp.where(tri, jnp.exp(jnp.where(tri, diff, 0.0)), 0.0)
    kbeta = k * beta[..., None]
    lower = jnp.where(strict, jnp.einsum('bhnid,bhnjd->bhnij', kbeta, k) * decay, 0.0)
    eye = jnp.eye(CHUNK, dtype=q.dtype)
    rhs = jnp.concatenate([v * beta[..., None], kbeta * jnp.exp(G)[..., None]], axis=-1)
    sol = lax.linalg.triangular_solve(lower + eye, rhs, left_side=True, lower=True,
                                      unit_diagonal=True)
    u, w = sol[..., :dv], sol[..., dv:]
    attn = jnp.where(tri, jnp.einsum('bhnid,bhnjd->bhnij', q, k) * decay, 0.0)
    q_dec = q * jnp.exp(G)[..., None]
    k_dec = k * jnp.exp(G[..., -1:] - G)[..., None]
    c_dec = jnp.exp(G[..., -1])
    xs = tuple(jnp.moveaxis(a, 2, 0) for a in (q_dec, k_dec, u, w, attn, c_dec))

    def step(state, inp):
        qd, kd, u_c, w_c, a_c, cd = inp
        v_new = u_c - jnp.einsum('bhcd,bhde->bhce', w_c, state)
        o = jnp.einsum('bhcd,bhde->bhce', qd, state) + jnp.einsum('bhij,bhje->bhie', a_c, v_new)
        state = state * cd[..., None, None] + jnp.einsum('bhcd,bhce->bhde', kd, v_new)
        return state, o

    state0 = jnp.zeros((b, h, dk, dv), q.dtype)
    _, o = lax.scan(step, state0, xs)
    return jnp.transpose(o, (1, 0, 3, 2, 4)).reshape(b, t, h, dv)


def _gated_deltanet_mixer(qkv, z, b_raw, a_raw, conv_w, a_log_fwd, dt_bias_fwd,
                          a_log_bwd, dt_bias_bwd, dn_norm_g):
    bsz, s, _ = qkv.shape
    qkv = jax.nn.silu(_centred_depthwise_conv(qkv, conv_w).astype(F32))
    q, k, v = jnp.split(qkv, 3, axis=-1)
    heads = lambda a: a.reshape(bsz, s, DN_HEADS, DN_HEAD_DIM)
    q = _l2norm(heads(q))
    k = _l2norm(heads(k))
    v = heads(v)
    beta_f = jax.nn.sigmoid(b_raw[..., :DN_HEADS])
    beta_b = jax.nn.sigmoid(b_raw[..., DN_HEADS:])
    g_f = -jnp.exp(a_log_fwd) * jax.nn.softplus(a_raw[..., :DN_HEADS] + dt_bias_fwd)
    g_b = -jnp.exp(a_log_bwd) * jax.nn.softplus(a_raw[..., DN_HEADS:] + dt_bias_bwd)
    flip = lambda a: jnp.flip(a, axis=1)
    o_f = _gated_delta_rule_chunked(q, k, v, g_f, beta_f)
    o_b = flip(_gated_delta_rule_chunked(flip(q), flip(k), flip(v), flip(g_b), flip(beta_b)))
    o = o_f + o_b
    o = o * lax.rsqrt(jnp.mean(o * o, axis=-1, keepdims=True) + EPS) * dn_norm_g
    o = o * jax.nn.silu(heads(z))
    return o.reshape(bsz, s, DN_WIDTH)


def _t5_bucket(rel):
    nb = NUM_BUCKETS // 2
    ret = jnp.where(rel > 0, nb, 0)
    n = jnp.abs(rel)
    max_exact = nb // 2
    nf = jnp.maximum(n, 1).astype(F32)
    large = max_exact + (jnp.log(nf / max_exact) / math.log(MAX_DISTANCE / max_exact)
                         * (nb - max_exact)).astype(jnp.int32)
    large = jnp.minimum(large, nb - 1)
    return ret + jnp.where(n < max_exact, n, large)


ATT_TILE = 2048
ATT_HALO = BAND * max(d for _, d in DILATED_PATTERNS)
QBLK = 2 * BAND
KBLK = QBLK + 2 * BAND
NEG = -1e30


def _attn_bias_table(rel_bias):
    delta = (jnp.arange(KBLK) - BAND)[None, :] - jnp.arange(QBLK)[:, None]
    tabs = []
    for window, dil in DILATED_PATTERNS:
        half = window // (2 * dil)
        bias = rel_bias.astype(F32)[_t5_bucket(delta * dil)].transpose(2, 0, 1)
        tabs.append(jnp.where((jnp.abs(delta) <= half)[None], bias, NEG))
    return jnp.stack(tabs)


def _attn_kernel(q_ref, kp_ref, kc_ref, kn_ref, vp_ref, vc_ref, vn_ref, bias_ref, o_ref,
                 qd, kd, vd, od, md, ld, o_sc, m_sc, l_sc, *, seq_len):
    t = pl.program_id(1)
    lane = lax.broadcasted_iota(jnp.int32, (QBLK, LANE), 1)
    low = lane < ATT_HEAD_DIM
    scale = ATT_HEAD_DIM ** -0.5
    kcol = lax.broadcasted_iota(jnp.int32, (1, KBLK), 1)
    ones = jnp.ones((KBLK, LANE), BF16)

    for p, (window, dil) in enumerate(DILATED_PATTERNS):
        lt = ATT_TILE // dil
        lh = ATT_HALO // dil
        span = lt + 2 * lh
        nblk = lt // QBLK
        n_pos = seq_len // dil
        for r in range(dil):
            sl = lambda n: pl.ds(r, n, stride=dil) if dil > 1 else pl.ds(0, n)
            qd[pl.ds(r * lt, lt), :] = q_ref[0, sl(lt), :] * scale
            off = r * span
            for (kr, vr, n) in ((kp_ref, vp_ref, lh), (kc_ref, vc_ref, lt), (kn_ref, vn_ref, lh)):
                kd[pl.ds(off, n), :] = kr[0, sl(n), :].astype(BF16)
                vd[pl.ds(off, n), :] = vr[0, sl(n), :].astype(BF16)
                off += n

        def body(i, carry):
            r = i // nblk
            j = i % nblk
            q0 = pl.multiple_of(r * lt + j * QBLK, QBLK)
            k0 = pl.multiple_of(r * span + lh + j * QBLK - BAND, BAND)
            q = qd[pl.ds(q0, QBLK), :]
            k = kd[pl.ds(k0, KBLK), :]
            v1 = jnp.concatenate([vd[pl.ds(k0, KBLK), :], ones], axis=1)
            pos = t * lt + j * QBLK - BAND + kcol
            valid = (pos >= 0) & (pos < n_pos)
            res = []
            for h in range(2):
                qh = jnp.where(low, q, 0.0) if h == 0 else jnp.where(low, 0.0, q)
                s = lax.dot_general(qh.astype(BF16), k, (((1,), (1,)), ((), ())),
                                    preferred_element_type=F32) + bias_ref[p, h]
                s = jnp.where(valid, s, NEG)
                m = jnp.max(s, axis=-1, keepdims=True)
                e = jnp.exp(s - m).astype(BF16)
                res.append((jnp.dot(e, v1, preferred_element_type=F32), m))
            (pv0, m0), (pv1, m1) = res
            od[pl.ds(q0, QBLK), :] = jnp.where(low, pv0[:, :LANE], pv1[:, :LANE])
            ld[pl.ds(q0, QBLK), :] = jnp.where(low, pv0[:, LANE:], pv1[:, LANE:])
            md[pl.ds(q0, QBLK), :] = jnp.where(low, m0, m1)
            return carry

        lax.fori_loop(0, dil * nblk, body, 0, unroll=2)
        for r in range(dil):
            sl = pl.ds(r, lt, stride=dil) if dil > 1 else pl.ds(0, lt)
            src = pl.ds(r * lt, lt)
            o_sc[p, sl, :] = od[src, :]
            m_sc[p, sl, :] = md[src, :]
            l_sc[p, sl, :] = ld[src, :]

    n_p = len(DILATED_PATTERNS)
    mx = m_sc[0]
    for p in range(1, n_p):
        mx = jnp.maximum(mx, m_sc[p])
    num = jnp.zeros((ATT_TILE, LANE), F32)
    den = jnp.zeros((ATT_TILE, LANE), F32)
    for p in range(n_p):
        w = jnp.exp(m_sc[p] - mx)
        num += o_sc[p] * w
        den += l_sc[p] * w
    o_ref[0] = (num / den).astype(o_ref.dtype)


def _dilated_attention(att, rel_bias):
    b, s, _ = att.shape
    assert s % ATT_TILE == 0
    nt = s // ATT_TILE
    nh = s // ATT_HALO
    per = ATT_TILE // ATT_HALO
    hp = ATT_WIDTH // LANE
    bias = _attn_bias_table(rel_bias)
    cur = lambda col0: pl.BlockSpec((1, ATT_TILE, LANE), lambda bi, ti, hi: (bi, ti, col0 + hi))
    prev = lambda col0: pl.BlockSpec(
        (1, ATT_HALO, LANE), lambda bi, ti, hi: (bi, jnp.maximum(ti * per - 1, 0), col0 + hi))
    nxt = lambda col0: pl.BlockSpec(
        (1, ATT_HALO, LANE), lambda bi, ti, hi: (bi, jnp.minimum((ti + 1) * per, nh - 1), col0 + hi))
    n_p = len(DILATED_PATTERNS)
    tile_f32 = pltpu.VMEM((ATT_TILE, LANE), F32)
    span_bf16 = pltpu.VMEM((ATT_TILE + 2 * ATT_HALO, LANE), BF16)
    per_pattern = pltpu.VMEM((n_p, ATT_TILE, LANE), F32)
    return pl.pallas_call(
        functools.partial(_attn_kernel, seq_len=s),
        grid=(b, nt, hp),
        in_specs=[cur(0), prev(hp), cur(hp), nxt(hp), prev(2 * hp), cur(2 * hp), nxt(2 * hp),
                  pl.BlockSpec((n_p, 2, QBLK, KBLK), lambda bi, ti, hi: (0, hi, 0, 0))],
        out_specs=pl.BlockSpec((1, ATT_TILE, LANE), lambda bi, ti, hi: (bi, ti, hi)),
        out_shape=jax.ShapeDtypeStruct((b, s, ATT_WIDTH), BF16),
        scratch_shapes=[tile_f32, span_bf16, span_bf16, tile_f32, tile_f32, tile_f32,
                        per_pattern, per_pattern, per_pattern],
        compiler_params=_params("parallel", "parallel", "parallel"),
        name="dilated_attention",
    )(att, att, att, att, att, att, att, bias)


def _mix(x, rel_bias, norm1_g, w_in, conv_w, a_log_fwd, dt_bias_fwd, a_log_bwd, dt_bias_bwd,
         dn_norm_g, w_out, norm2_g):
    bsz, s, d = x.shape
    xf = x.reshape(bsz * s, d)
    qkv, z, ba, att = _in_proj(xf, norm1_g[None], w_in)
    shp = lambda a: a.reshape(bsz, s, a.shape[-1])
    ba = shp(ba)
    dn_out = _gated_deltanet_mixer(shp(qkv), shp(z), ba[..., :2 * DN_HEADS],
                                   ba[..., 2 * DN_HEADS:GATE_COLS], conv_w, a_log_fwd,
                                   dt_bias_fwd, a_log_bwd, dt_bias_bwd, dn_norm_g)
    att_out = _dilated_attention(shp(att), rel_bias)
    x1, h2 = _out_proj(dn_out.reshape(bsz * s, DN_WIDTH), att_out.reshape(bsz * s, ATT_WIDTH), xf,
                       w_out, norm2_g[None])
    return x1, h2


def _route(h2, w_router):
    n_tok = h2.shape[0]
    cap = CAPACITY_FACTOR * n_tok // N_EXPERTS
    aff = jax.nn.softmax(jnp.einsum('nd,de->ne', h2, w_router,
                                    precision=lax.Precision.HIGHEST), axis=-1)
    gate, idx = lax.top_k(aff.T, cap)
    return gate, idx


def kernel(x_prompt, x_sample, rel_bias, norm1_g, w_in, conv_w, a_log_fwd, dt_bias_fwd, a_log_bwd,
           dt_bias_bwd, dn_norm_g, w_out, norm2_g, w_router, w_gate, w_up, w_down, final_norm_g):
    groups = (x_prompt, x_sample)
    x1s, gates, idxs, xes = [], [], [], []
    for x in groups:
        x1, h2 = _mix(x, rel_bias, norm1_g[0], w_in[0], conv_w[0], a_log_fwd[0], dt_bias_fwd[0],
                      a_log_bwd[0], dt_bias_bwd[0], dn_norm_g[0], w_out[0], norm2_g[0])
        gate, idx = _route(h2, w_router[0])
        x1s.append(x1)
        gates.append(gate)
        idxs.append(idx)
        xes.append(h2.astype(BF16)[idx])
    xe = jnp.concatenate(xes, axis=1)
    ye = _expert_ffn(xe, w_gate[0], w_up[0], w_down[0])
    outs = []
    off = 0
    for x, x1, gate, idx in zip(groups, x1s, gates, idxs):
        cap = idx.shape[1]
        contrib = (gate[..., None] * ye[:, off:off + cap]).reshape(-1, D_MODEL)
        moe = jnp.zeros_like(x1).at[idx.reshape(-1)].add(contrib)
        outs.append(_final_norm(x1, moe, final_norm_g[None]).reshape(x.shape))
        off += cap
    return tuple(outs)
```

```python
import functools
import math

import jax
import jax.numpy as jnp
from jax import lax
from jax.experimental import pallas as pl
from jax.experimental.pallas import tpu as pltpu

D_MODEL = 1024
DN_HEADS = 4
DN_HEAD_DIM = 128
DN_WIDTH = DN_HEADS * DN_HEAD_DIM
ATT_HEADS = 8
ATT_HEAD_DIM = 64
ATT_WIDTH = ATT_HEADS * ATT_HEAD_DIM
CONV_K = 3
CHUNK = 64
DILATED_PATTERNS = ((128, 1), (512, 4), (2048, 16))
BAND = 64
NUM_BUCKETS = 32
MAX_DISTANCE = 1024
N_EXPERTS = 16
CAPACITY_FACTOR = 2
EXPERT_D_FF = 2816
EPS = 1e-6
GATE_COLS = 4 * DN_HEADS
LANE = 128
SUBLANE = 8

F32 = jnp.float32
BF16 = jnp.bfloat16

VMEM_LIMIT = 56 * 1024 * 1024


def _params(*sem):
    return pltpu.CompilerParams(dimension_semantics=sem, vmem_limit_bytes=VMEM_LIMIT)


def _in_proj_kernel(x_ref, g_ref, wqkv_ref, wz_ref, wba_ref, watt_ref,
                    qkv_ref, z_ref, ba_ref, att_ref):
    x = x_ref[...]
    h = x * lax.rsqrt(jnp.mean(x * x, axis=-1, keepdims=True) + EPS) * g_ref[...]
    hb = h.astype(BF16)
    qkv_ref[...] = jnp.dot(hb, wqkv_ref[...], preferred_element_type=F32)
    z_ref[...] = jnp.dot(hb, wz_ref[...], preferred_element_type=F32)
    ba_ref[...] = jnp.dot(hb, wba_ref[...], preferred_element_type=F32)
    att_ref[...] = jnp.dot(hb, watt_ref[...], preferred_element_type=F32)


def _in_proj(x, g, w_in, tm=512):
    n = x.shape[0]
    o_qkv = 3 * DN_WIDTH
    o_z = o_qkv + DN_WIDTH
    o_ba = o_z + GATE_COLS
    wb = w_in.astype(BF16)
    wqkv = wb[:, :o_qkv]
    wz = wb[:, o_qkv:o_z]
    wba = jnp.pad(wb[:, o_z:o_ba], ((0, 0), (0, LANE - GATE_COLS)))
    watt = wb[:, o_ba:]
    full = lambda a: pl.BlockSpec(a.shape, lambda i: (0, 0))
    row = lambda w: pl.BlockSpec((tm, w), lambda i: (i, 0))
    return pl.pallas_call(
        _in_proj_kernel,
        grid=(n // tm,),
        in_specs=[row(D_MODEL), full(g), full(wqkv), full(wz), full(wba), full(watt)],
        out_specs=[row(o_qkv), row(DN_WIDTH), row(LANE), row(3 * ATT_WIDTH)],
        out_shape=[jax.ShapeDtypeStruct((n, o_qkv), F32),
                   jax.ShapeDtypeStruct((n, DN_WIDTH), F32),
                   jax.ShapeDtypeStruct((n, LANE), F32),
                   jax.ShapeDtypeStruct((n, 3 * ATT_WIDTH), F32)],
        compiler_params=_params("parallel"),
        name="in_proj",
    )(x, g, wqkv, wz, wba, watt)


def _out_proj_kernel(of_ref, ob_ref, z_ref, gdn_ref, att_ref, x_ref, wdn_ref, watt_ref, g_ref,
                     x1_ref, h2_ref):
    o = of_ref[...].astype(F32) + ob_ref[...].astype(F32)
    z = z_ref[...]
    gated = []
    for h in range(DN_HEADS):
        hs = slice(h * DN_HEAD_DIM, (h + 1) * DN_HEAD_DIM)
        oh = o[:, hs]
        oh = oh * lax.rsqrt(jnp.mean(oh * oh, axis=-1, keepdims=True) + EPS) * gdn_ref[...]
        zh = z[:, hs]
        gated.append((oh * (zh * jax.nn.sigmoid(zh))).astype(BF16))
    dn = jnp.concatenate(gated, axis=1)
    x1 = (x_ref[...]
          + jnp.dot(dn, wdn_ref[...], preferred_element_type=F32)
          + jnp.dot(att_ref[...], watt_ref[...], preferred_element_type=F32))
    x1_ref[...] = x1
    h2_ref[...] = x1 * lax.rsqrt(jnp.mean(x1 * x1, axis=-1, keepdims=True) + EPS) * g_ref[...]


def _out_proj(o_f, o_b, z, dn_norm_g, att, x, w_out, g, tm=512):
    n = x.shape[0]
    wb = w_out.astype(BF16)
    wdn, watt = wb[:DN_WIDTH], wb[DN_WIDTH:]
    full = lambda a: pl.BlockSpec(a.shape, lambda i: (0, 0))
    row = lambda w: pl.BlockSpec((tm, w), lambda i: (i, 0))
    return pl.pallas_call(
        _out_proj_kernel,
        grid=(n // tm,),
        in_specs=[row(DN_WIDTH), row(DN_WIDTH), row(DN_WIDTH), full(dn_norm_g), row(ATT_WIDTH),
                  row(D_MODEL), full(wdn), full(watt), full(g)],
        out_specs=[row(D_MODEL), row(D_MODEL)],
        out_shape=[jax.ShapeDtypeStruct((n, D_MODEL), F32)] * 2,
        compiler_params=_params("parallel"),
        name="out_proj",
    )(o_f, o_b, z, dn_norm_g, att, x, wdn, watt, g)


def _ffn_kernel(xe_ref, wg_ref, wu_ref, wd_ref, ye_ref, acc_ref):
    f = pl.program_id(2)

    @pl.when(f == 0)
    def _():
        acc_ref[...] = jnp.zeros_like(acc_ref)

    xe = xe_ref[0]
    a = jnp.dot(xe, wg_ref[0].astype(BF16), preferred_element_type=F32)
    b = jnp.dot(xe, wu_ref[0].astype(BF16), preferred_element_type=F32)
    h = (a * jax.nn.sigmoid(a) * b).astype(BF16)
    acc_ref[...] += jnp.dot(h, wd_ref[0].astype(BF16), preferred_element_type=F32)

    @pl.when(f == pl.num_programs(2) - 1)
    def _():
        ye_ref[0] = acc_ref[...]


def _expert_ffn(xe, w_gate, w_up, w_down, tm=1024, tf=256):
    e, rows, d = xe.shape
    dff = w_gate.shape[-1]
    return pl.pallas_call(
        _ffn_kernel,
        grid=(e, rows // tm, dff // tf),
        in_specs=[pl.BlockSpec((1, tm, d), lambda i, r, f: (i, r, 0)),
                  pl.BlockSpec((1, d, tf), lambda i, r, f: (i, 0, f)),
                  pl.BlockSpec((1, d, tf), lambda i, r, f: (i, 0, f)),
                  pl.BlockSpec((1, tf, d), lambda i, r, f: (i, f, 0))],
        out_specs=pl.BlockSpec((1, tm, d), lambda i, r, f: (i, r, 0)),
        out_shape=jax.ShapeDtypeStruct((e, rows, d), F32),
        scratch_shapes=[pltpu.VMEM((tm, d), F32)],
        compiler_params=_params("parallel", "parallel", "arbitrary"),
        name="expert_ffn",
    )(xe, w_gate, w_up, w_down)


def _final_kernel(x1_ref, m_ref, g_ref, y_ref):
    x = x1_ref[...] + m_ref[...]
    y_ref[...] = x * lax.rsqrt(jnp.mean(x * x, axis=-1, keepdims=True) + EPS) * g_ref[...]


def _final_norm(x1, moe, g, tm=1024):
    n = x1.shape[0]
    row = pl.BlockSpec((tm, D_MODEL), lambda i: (i, 0))
    return pl.pallas_call(
        _final_kernel,
        grid=(n // tm,),
        in_specs=[row, row, pl.BlockSpec(g.shape, lambda i: (0, 0))],
        out_specs=row,
        out_shape=jax.ShapeDtypeStruct((n, D_MODEL), F32),
        compiler_params=_params("parallel"),
        name="final_norm",
    )(x1, moe, g)


def _dn_prep_kernel(x_ref, xp_ref, xn_ref, w_ref, q_ref, k_ref, v_ref, kt_ref):
    t = pl.program_id(1)
    nt = pl.num_programs(1)
    x = x_ref[0]
    rows = x.shape[0]
    row = lax.broadcasted_iota(jnp.int32, (rows, 1), 0)
    before = jnp.where(t > 0, xp_ref[0, SUBLANE - 1:SUBLANE, :], 0.0)
    after = jnp.where(t < nt - 1, xn_ref[0, 0:1, :], 0.0)
    x_prev = jnp.where(row == 0, before, pltpu.roll(x, 1, axis=0))
    x_next = jnp.where(row == rows - 1, after, pltpu.roll(x, rows - 1, axis=0))
    y = w_ref[0:1, :] * x_prev + w_ref[1:2, :] * x + w_ref[2:3, :] * x_next
    y = y * jax.nn.sigmoid(y)
    for h in range(DN_HEADS):
        for part, ref, scale in ((0, q_ref, DN_HEAD_DIM ** -0.5), (1, k_ref, 1.0)):
            c0 = part * DN_WIDTH + h * DN_HEAD_DIM
            a = y[:, c0:c0 + DN_HEAD_DIM]
            inv = lax.rsqrt(jnp.sum(a * a, axis=-1, keepdims=True) + EPS) * scale
            a = a * inv
            ref[0, :, h * DN_HEAD_DIM:(h + 1) * DN_HEAD_DIM] = a.astype(ref.dtype)
            if part == 1:
                for j in range(rows // CHUNK):
                    kt_ref[0, j, h * DN_HEAD_DIM:(h + 1) * DN_HEAD_DIM, :] = (
                        a[j * CHUNK:(j + 1) * CHUNK, :].T.astype(kt_ref.dtype))
    v_ref[0] = y[:, 2 * DN_WIDTH:].astype(v_ref.dtype)


def _dn_prep(qkv, conv_w, tile=512):
    b, s, c = qkv.shape
    tile = min(tile, s)
    nt = s // tile
    per = tile // SUBLANE
    nsub = s // SUBLANE
    out = jax.ShapeDtypeStruct((b, s, DN_WIDTH), BF16)
    ospec = pl.BlockSpec((1, tile, DN_WIDTH), lambda bi, ti: (bi, ti, 0))
    return pl.pallas_call(
        _dn_prep_kernel,
        grid=(b, nt),
        in_specs=[pl.BlockSpec((1, tile, c), lambda bi, ti: (bi, ti, 0)),
                  pl.BlockSpec((1, SUBLANE, c), lambda bi, ti: (bi, jnp.maximum(ti * per - 1, 0), 0)),
                  pl.BlockSpec((1, SUBLANE, c), lambda bi, ti: (bi, jnp.minimum((ti + 1) * per, nsub - 1), 0)),
                  pl.BlockSpec((CONV_K, c), lambda bi, ti: (0, 0))],
        out_specs=[ospec, ospec, ospec,
                   pl.BlockSpec((1, tile // CHUNK, DN_WIDTH, CHUNK), lambda bi, ti: (bi, ti, 0, 0))],
        out_shape=[out, out, out, jax.ShapeDtypeStruct((b, s // CHUNK, DN_WIDTH, CHUNK), BF16)],
        compiler_params=_params("parallel", "parallel"),
        name="dn_prep",
    )(qkv, qkv, qkv, conv_w)


def _lane_bcast(a, col, width=LANE):
    return jnp.broadcast_to(a[:, col:col + 1], (a.shape[0], width))


DN_TILE = 1024
INV_STEPS = 5
assert 2 ** (INV_STEPS + 1) == CHUNK
DN_UNROLL = 16


def _dn_scan_kernel(q_ref, k_ref, v_ref, kt_ref, ba_ref, alog_ref, dtb_ref, o_ref,
                    s_ref, xbuf, tbuf, abuf, rbuf, uwbuf, qgbuf, kdbuf, lbuf, nbuf, obuf, ecbuf, *, reverse):
    @pl.when(pl.program_id(1) == 0)
    def _():
        s_ref[...] = jnp.zeros_like(s_ref)

    tile = q_ref.shape[1]
    nc = tile // CHUNK
    n_prob = nc * DN_HEADS
    unroll = min(DN_UNROLL, n_prob)
    beta_col = DN_HEADS if reverse else 0
    a_col = 2 * DN_HEADS + (DN_HEADS if reverse else 0)
    ri = lax.broadcasted_iota(jnp.int32, (CHUNK, CHUNK), 0)
    ci = lax.broadcasted_iota(jnp.int32, (CHUNK, CHUNK), 1)
    tri = (ri <= ci) if reverse else (ri >= ci)
    strict = (ri < ci) if reverse else (ri > ci)
    eye = jnp.where(ri == ci, 1.0, 0.0)
    row = lax.broadcasted_iota(jnp.int32, (CHUNK, LANE), 0)
    a_scale = jnp.exp(alog_ref[...])
    dtb = dtb_ref[...]
    dot = lambda a, b: jnp.dot(a, b, preferred_element_type=F32)

    def build(c, carry):
        rows = pl.ds(pl.multiple_of(c * CHUNK, CHUNK), CHUNK)
        raw = ba_ref[0, rows, :]
        beta_all = jax.nn.sigmoid(raw)
        xs = raw + dtb
        g = -a_scale * (jnp.maximum(xs, 0.0) + jnp.log(1.0 + jnp.exp(-jnp.abs(xs))))
        gs = g
        sh = 1
        while sh < CHUNK:
            if reverse:
                gs = gs + jnp.where(row < CHUNK - sh, pltpu.roll(gs, CHUNK - sh, axis=0), 0.0)
            else:
                gs = gs + jnp.where(row >= sh, pltpu.roll(gs, sh, axis=0), 0.0)
            sh *= 2
        gs_t = gs.T
        g_last = gs[0:1, :] if reverse else gs[CHUNK - 1:CHUNK, :]
        eg_all = jnp.exp(gs)
        ek_t = jnp.exp(g_last.T - gs_t)
        ecbuf[c] = jnp.broadcast_to(jnp.exp(g_last), (SUBLANE, LANE))
        for h in range(DN_HEADS):
            p = c * DN_HEADS + h
            hs = slice(h * DN_HEAD_DIM, (h + 1) * DN_HEAD_DIM)
            q = q_ref[0, rows, hs].astype(F32)
            k = k_ref[0, rows, hs]
            v = v_ref[0, rows, hs].astype(F32)
            beta = _lane_bcast(beta_all, beta_col + h)
            eg = _lane_bcast(eg_all, a_col + h)
            diff = _lane_bcast(gs, a_col + h, CHUNK) - gs_t[a_col + h:a_col + h + 1, :]
            decay = jnp.where(tri, jnp.exp(jnp.where(tri, diff, 0.0)), 0.0)
            kb = k.astype(F32) * beta
            kq = lax.dot_general(jnp.concatenate([kb, q], axis=0).astype(BF16), k,
                                 (((1,), (1,)), ((), ())), preferred_element_type=F32)
            lower = jnp.where(strict, kq[:CHUNK] * decay, 0.0)
            xbuf[p] = (-lower).astype(BF16)
            tbuf[p] = eye - lower
            abuf[p] = (kq[CHUNK:] * decay).astype(BF16)
            rbuf[p] = jnp.concatenate([v * beta, kb * eg], axis=1).astype(BF16)
            qgbuf[p] = q * eg
            kdbuf[p] = (kt_ref[0, c, hs, :].astype(F32)
                        * ek_t[a_col + h:a_col + h + 1, :]).astype(BF16)
        return carry

    lax.fori_loop(0, nc, build, 0)

    for step in range(INV_STEPS + 1):
        def double(p, carry, first=(step == 0), last=(step == INV_STEPS)):
            x = xbuf[p]
            if not first:
                t = tbuf[p]
                tbuf[p] = t + dot(t.astype(BF16), x)
            if not last:
                xbuf[p] = dot(x, x).astype(BF16)
            return carry
        lax.fori_loop(0, n_prob, double, 0, unroll=unroll)

    def solve(p, carry):
        uwbuf[p] = dot(tbuf[p].astype(BF16), rbuf[p]).astype(BF16)
        return carry

    lax.fori_loop(0, n_prob, solve, 0, unroll=unroll)

    def fold(p, carry):
        uw = uwbuf[p]
        a_uw = dot(abuf[p], uw)
        k_uw = dot(kdbuf[p], uw)
        obuf[p] = a_uw[:, :DN_HEAD_DIM]
        nbuf[p] = k_uw[:, :DN_HEAD_DIM]
        lbuf[p, :DN_HEAD_DIM, :] = k_uw[:, DN_HEAD_DIM:].astype(BF16)
        lbuf[p, DN_HEAD_DIM:, :] = (qgbuf[p] - a_uw[:, DN_HEAD_DIM:]).astype(BF16)
        return carry

    lax.fori_loop(0, n_prob, fold, 0, unroll=unroll)

    def scan(jc, carry):
        c = (nc - 1 - jc) if reverse else jc
        rows = pl.ds(pl.multiple_of(c * CHUNK, CHUNK), CHUNK)
        ec_all = ecbuf[c]
        for h in range(DN_HEADS):
            p = c * DN_HEADS + h
            state = s_ref[h]
            r = dot(lbuf[p], state.astype(BF16))
            s_ref[h] = state * _lane_bcast(ec_all[0:1], a_col + h) - r[:DN_HEAD_DIM] + nbuf[p]
            o = r[DN_HEAD_DIM:] + obuf[p]
            o_ref[0, rows, h * DN_HEAD_DIM:(h + 1) * DN_HEAD_DIM] = o.astype(o_ref.dtype)
        return carry

    lax.fori_loop(0, nc, scan, 0)


def _dn_scan(q, k, v, kt, ba, a_log_row, dt_bias_row, reverse, tile=DN_TILE):
    b, s, _ = q.shape
    tile = min(tile, s)
    nt = s // tile
    nc = tile // CHUNK
    n_prob = nc * DN_HEADS
    tmap = (lambda bi, ti: (bi, nt - 1 - ti, 0)) if reverse else (lambda bi, ti: (bi, ti, 0))
    wide = pl.BlockSpec((1, tile, DN_WIDTH), tmap)
    par = pl.BlockSpec((1, LANE), lambda bi, ti: (0, 0))
    return pl.pallas_call(
        functools.partial(_dn_scan_kernel, reverse=reverse),
        grid=(b, nt),
        in_specs=[wide, wide, wide,
                  pl.BlockSpec((1, nc, DN_WIDTH, CHUNK), lambda bi, ti: tmap(bi, ti) + (0,)),
                  pl.BlockSpec((1, tile, LANE), tmap), par, par],
        out_specs=wide,
        out_shape=jax.ShapeDtypeStruct((b, s, DN_WIDTH), BF16),
        scratch_shapes=[pltpu.VMEM((DN_HEADS, DN_HEAD_DIM, DN_HEAD_DIM), F32),
                        pltpu.VMEM((n_prob, CHUNK, CHUNK), BF16),
                        pltpu.VMEM((n_prob, CHUNK, CHUNK), F32),
                        pltpu.VMEM((n_prob, CHUNK, CHUNK), BF16),
                        pltpu.VMEM((n_prob, CHUNK, 2 * DN_HEAD_DIM), BF16),
                        pltpu.VMEM((n_prob, CHUNK, 2 * DN_HEAD_DIM), BF16),
                        pltpu.VMEM((n_prob, CHUNK, DN_HEAD_DIM), F32),
                        pltpu.VMEM((n_prob, DN_HEAD_DIM, CHUNK), BF16),
                        pltpu.VMEM((n_prob, DN_HEAD_DIM + CHUNK, DN_HEAD_DIM), BF16),
                        pltpu.VMEM((n_prob, DN_HEAD_DIM, DN_HEAD_DIM), F32),
                        pltpu.VMEM((n_prob, CHUNK, DN_HEAD_DIM), F32),
                        pltpu.VMEM((nc, SUBLANE, LANE), F32)],
        compiler_params=_params("parallel", "arbitrary"),
        name="dn_scan_bwd" if reverse else "dn_scan_fwd",
    )(q, k, v, kt, ba, a_log_row, dt_bias_row)


def _gate_rows(a_log_fwd, a_log_bwd, dt_bias_fwd, dt_bias_bwd):
    pad = lambda f, b: jnp.pad(jnp.concatenate([f, b]).astype(F32), (2 * DN_HEADS, LANE - 4 * DN_HEADS))[None]
    return pad(a_log_fwd, a_log_bwd), pad(dt_bias_fwd, dt_bias_bwd)


def _deltanet(qkv, ba, conv_w, a_log_fwd, dt_bias_fwd, a_log_bwd, dt_bias_bwd):
    q, k, v, kt = _dn_prep(qkv, conv_w)
    alog, dtb = _gate_rows(a_log_fwd, a_log_bwd, dt_bias_fwd, dt_bias_bwd)
    o_f = _dn_scan(q, k, v, kt, ba, alog, dtb, reverse=False)
    o_b = _dn_scan(q, k, v, kt, ba, alog, dtb, reverse=True)
    return o_f, o_b


def _t5_bucket(rel):
    nb = NUM_BUCKETS // 2
    ret = jnp.where(rel > 0, nb, 0)
    n = jnp.abs(rel)
    max_exact = nb // 2
    nf = jnp.maximum(n, 1).astype(F32)
    large = max_exact + (jnp.log(nf / max_exact) / math.log(MAX_DISTANCE / max_exact)
                         * (nb - max_exact)).astype(jnp.int32)
    large = jnp.minimum(large, nb - 1)
    return ret + jnp.where(n < max_exact, n, large)


ATT_TILE = 2048
ATT_HALO = BAND * max(d for _, d in DILATED_PATTERNS)
QBLK = 2 * BAND
KBLK = QBLK + 2 * BAND
NEG = -1e30


def _attn_bias_table(rel_bias):
    delta = (jnp.arange(KBLK) - BAND)[None, :] - jnp.arange(QBLK)[:, None]
    tabs = []
    for window, dil in DILATED_PATTERNS:
        half = window // (2 * dil)
        bias = rel_bias.astype(F32)[_t5_bucket(delta * dil)].transpose(2, 0, 1)
        tabs.append(jnp.where((jnp.abs(delta) <= half)[None], bias, NEG))
    return jnp.stack(tabs)


def _attn_kernel(q_ref, kp_ref, kc_ref, kn_ref, vp_ref, vc_ref, vn_ref, bias_ref, o_ref,
                 qd, kd, vd, od, md, ld, o_sc, m_sc, l_sc, *, seq_len):
    t = pl.program_id(1)
    lane = lax.broadcasted_iota(jnp.int32, (QBLK, LANE), 1)
    low = lane < ATT_HEAD_DIM
    scale = ATT_HEAD_DIM ** -0.5
    kcol = lax.broadcasted_iota(jnp.int32, (1, KBLK), 1)
    ones = jnp.ones((KBLK, LANE), BF16)

    for p, (window, dil) in enumerate(DILATED_PATTERNS):
        lt = ATT_TILE // dil
        lh = ATT_HALO // dil
        span = lt + 2 * lh
        nblk = lt // QBLK
        n_pos = seq_len // dil
        for r in range(dil):
            sl = lambda n: pl.ds(r, n, stride=dil) if dil > 1 else pl.ds(0, n)
            qd[pl.ds(r * lt, lt), :] = q_ref[0, sl(lt), :] * scale
            off = r * span
            for (kr, vr, n) in ((kp_ref, vp_ref, lh), (kc_ref, vc_ref, lt), (kn_ref, vn_ref, lh)):
                kd[pl.ds(off, n), :] = kr[0, sl(n), :].astype(BF16)
                vd[pl.ds(off, n), :] = vr[0, sl(n), :].astype(BF16)
                off += n

        def body(i, carry):
            r = i // nblk
            j = i % nblk
            q0 = pl.multiple_of(r * lt + j * QBLK, QBLK)
            k0 = pl.multiple_of(r * span + lh + j * QBLK - BAND, BAND)
            q = qd[pl.ds(q0, QBLK), :]
            k = kd[pl.ds(k0, KBLK), :]
            v1 = jnp.concatenate([vd[pl.ds(k0, KBLK), :], ones], axis=1)
            pos = t * lt + j * QBLK - BAND + kcol
            valid = (pos >= 0) & (pos < n_pos)
            res = []
            for h in range(2):
                qh = jnp.where(low, q, 0.0) if h == 0 else jnp.where(low, 0.0, q)
                s = lax.dot_general(qh.astype(BF16), k, (((1,), (1,)), ((), ())),
                                    preferred_element_type=F32) + bias_ref[p, h]
                s = jnp.where(valid, s, NEG)
                m = jnp.max(s, axis=-1, keepdims=True)
                e = jnp.exp(s - m).astype(BF16)
                res.append((jnp.dot(e, v1, preferred_element_type=F32), m))
            (pv0, m0), (pv1, m1) = res
            od[pl.ds(q0, QBLK), :] = jnp.where(low, pv0[:, :LANE], pv1[:, :LANE])
            ld[pl.ds(q0, QBLK), :] = jnp.where(low, pv0[:, LANE:], pv1[:, LANE:])
            md[pl.ds(q0, QBLK), :] = jnp.where(low, m0, m1)
            return carry

        lax.fori_loop(0, dil * nblk, body, 0, unroll=2)
        for r in range(dil):
            sl = pl.ds(r, lt, stride=dil) if dil > 1 else pl.ds(0, lt)
            src = pl.ds(r * lt, lt)
            o_sc[p, sl, :] = od[src, :]
            m_sc[p, sl, :] = md[src, :]
            l_sc[p, sl, :] = ld[src, :]

    n_p = len(DILATED_PATTERNS)
    mx = m_sc[0]
    for p in range(1, n_p):
        mx = jnp.maximum(mx, m_sc[p])
    num = jnp.zeros((ATT_TILE, LANE), F32)
    den = jnp.zeros((ATT_TILE, LANE), F32)
    for p in range(n_p):
        w = jnp.exp(m_sc[p] - mx)
        num += o_sc[p] * w
        den += l_sc[p] * w
    o_ref[0] = (num / den).astype(o_ref.dtype)


def _dilated_attention(att, rel_bias):
    b, s, _ = att.shape
    assert s % ATT_TILE == 0
    nt = s // ATT_TILE
    nh = s // ATT_HALO
    per = ATT_TILE // ATT_HALO
    hp = ATT_WIDTH // LANE
    bias = _attn_bias_table(rel_bias)
    cur = lambda col0: pl.BlockSpec((1, ATT_TILE, LANE), lambda bi, ti, hi: (bi, ti, col0 + hi))
    prev = lambda col0: pl.BlockSpec(
        (1, ATT_HALO, LANE), lambda bi, ti, hi: (bi, jnp.maximum(ti * per - 1, 0), col0 + hi))
    nxt = lambda col0: pl.BlockSpec(
        (1, ATT_HALO, LANE), lambda bi, ti, hi: (bi, jnp.minimum((ti + 1) * per, nh - 1), col0 + hi))
    n_p = len(DILATED_PATTERNS)
    tile_f32 = pltpu.VMEM((ATT_TILE, LANE), F32)
    span_bf16 = pltpu.VMEM((ATT_TILE + 2 * ATT_HALO, LANE), BF16)
    per_pattern = pltpu.VMEM((n_p, ATT_TILE, LANE), F32)
    return pl.pallas_call(
        functools.partial(_attn_kernel, seq_len=s),
        grid=(b, nt, hp),
        in_specs=[cur(0), prev(hp), cur(hp), nxt(hp), prev(2 * hp), cur(2 * hp), nxt(2 * hp),
                  pl.BlockSpec((n_p, 2, QBLK, KBLK), lambda bi, ti, hi: (0, hi, 0, 0))],
        out_specs=pl.BlockSpec((1, ATT_TILE, LANE), lambda bi, ti, hi: (bi, ti, hi)),
        out_shape=jax.ShapeDtypeStruct((b, s, ATT_WIDTH), BF16),
        scratch_shapes=[tile_f32, span_bf16, span_bf16, tile_f32, tile_f32, tile_f32,
                        per_pattern, per_pattern, per_pattern],
        compiler_params=_params("parallel", "parallel", "parallel"),
        name="dilated_attention",
    )(att, att, att, att, att, att, att, bias)


def _mix(x, rel_bias, norm1_g, w_in, conv_w, a_log_fwd, dt_bias_fwd, a_log_bwd, dt_bias_bwd,
         dn_norm_g, w_out, norm2_g):
    bsz, s, d = x.shape
    xf = x.reshape(bsz * s, d)
    qkv, z, ba, att = _in_proj(xf, norm1_g[None], w_in)
    shp = lambda a: a.reshape(bsz, s, a.shape[-1])
    o_f, o_b = _deltanet(shp(qkv), shp(ba), conv_w, a_log_fwd, dt_bias_fwd, a_log_bwd, dt_bias_bwd)
    att_out = _dilated_attention(shp(att), rel_bias)
    flat = lambda a: a.reshape(bsz * s, a.shape[-1])
    x1, h2 = _out_proj(flat(o_f), flat(o_b), z, dn_norm_g[None], flat(att_out), xf, w_out, norm2_g[None])
    return x1, h2


def _route(h2, w_router):
    n_tok = h2.shape[0]
    cap = CAPACITY_FACTOR * n_tok // N_EXPERTS
    aff = jax.nn.softmax(jnp.einsum('nd,de->ne', h2, w_router,
                                    precision=lax.Precision.HIGHEST), axis=-1)
    gate, idx = lax.top_k(aff.T, cap)
    return gate, idx


def kernel(x_prompt, x_sample, rel_bias, norm1_g, w_in, conv_w, a_log_fwd, dt_bias_fwd, a_log_bwd,
           dt_bias_bwd, dn_norm_g, w_out, norm2_g, w_router, w_gate, w_up, w_down, final_norm_g):
    groups = (x_prompt, x_sample)
    x1s, gates, idxs, xes = [], [], [], []
    for x in groups:
        x1, h2 = _mix(x, rel_bias, norm1_g[0], w_in[0], conv_w[0], a_log_fwd[0], dt_bias_fwd[0],
                      a_log_bwd[0], dt_bias_bwd[0], dn_norm_g[0], w_out[0], norm2_g[0])
        gate, idx = _route(h2, w_router[0])
        x1s.append(x1)
        gates.append(gate)
        idxs.append(idx)
        xes.append(h2.astype(BF16)[idx])
    xe = jnp.concatenate(xes, axis=1)
    ye = _expert_ffn(xe, w_gate[0], w_up[0], w_down[0])
    outs = []
    off = 0
    for x, x1, gate, idx in zip(groups, x1s, gates, idxs):
        cap = idx.shape[1]
        contrib = (gate[..., None] * ye[:, off:off + cap]).reshape(-1, D_MODEL)
        moe = jnp.zeros_like(x1).at[idx.reshape(-1)].add(contrib)
        outs.append(_final_norm(x1, moe, final_norm_g[None]).reshape(x.shape))
        off += cap
    return tuple(outs)
```

```python
import functools
import math

import jax
import jax.numpy as jnp
from jax import lax
from jax.experimental import pallas as pl
from jax.experimental.pallas import tpu as pltpu
from jax.experimental.pallas import tpu_sc as plsc

D_MODEL = 1024
DN_HEADS = 4
DN_HEAD_DIM = 128
DN_WIDTH = DN_HEADS * DN_HEAD_DIM
ATT_HEADS = 8
ATT_HEAD_DIM = 64
ATT_WIDTH = ATT_HEADS * ATT_HEAD_DIM
CONV_K = 3
CHUNK = 64
DILATED_PATTERNS = ((128, 1), (512, 4), (2048, 16))
BAND = 64
NUM_BUCKETS = 32
MAX_DISTANCE = 1024
N_EXPERTS = 16
CAPACITY_FACTOR = 2
EXPERT_D_FF = 2816
EPS = 1e-6
NEG = -1e30
GATE_COLS = 4 * DN_HEADS
LANE = 128
SUBLANE = 8

F32 = jnp.float32
BF16 = jnp.bfloat16

VMEM_LIMIT = 56 * 1024 * 1024


def _params(*sem):
    return pltpu.CompilerParams(dimension_semantics=sem, vmem_limit_bytes=VMEM_LIMIT)


def _in_proj_kernel(x_ref, g_ref, wqkv_ref, wz_ref, wba_ref, watt_ref,
                    qkv_ref, z_ref, ba_ref, att_ref):
    x = x_ref[...]
    h = x * lax.rsqrt(jnp.mean(x * x, axis=-1, keepdims=True) + EPS) * g_ref[...]
    hb = h.astype(BF16)
    qkv_ref[...] = jnp.dot(hb, wqkv_ref[...], preferred_element_type=F32)
    z_ref[...] = jnp.dot(hb, wz_ref[...], preferred_element_type=F32)
    ba_ref[...] = jnp.dot(hb, wba_ref[...], preferred_element_type=F32)
    att_ref[...] = jnp.dot(hb, watt_ref[...], preferred_element_type=F32)


def _in_proj(x, g, w_in, tm=512):
    n = x.shape[0]
    o_qkv = 3 * DN_WIDTH
    o_z = o_qkv + DN_WIDTH
    o_ba = o_z + GATE_COLS
    wb = w_in.astype(BF16)
    wqkv = wb[:, :o_qkv]
    wz = wb[:, o_qkv:o_z]
    wba = jnp.pad(wb[:, o_z:o_ba], ((0, 0), (0, LANE - GATE_COLS)))
    watt = wb[:, o_ba:]
    full = lambda a: pl.BlockSpec(a.shape, lambda i: (0, 0))
    row = lambda w: pl.BlockSpec((tm, w), lambda i: (i, 0))
    return pl.pallas_call(
        _in_proj_kernel,
        grid=(n // tm,),
        in_specs=[row(D_MODEL), full(g), full(wqkv), full(wz), full(wba), full(watt)],
        out_specs=[row(o_qkv), row(DN_WIDTH), row(LANE), row(3 * ATT_WIDTH)],
        out_shape=[jax.ShapeDtypeStruct((n, o_qkv), F32),
                   jax.ShapeDtypeStruct((n, DN_WIDTH), F32),
                   jax.ShapeDtypeStruct((n, LANE), F32),
                   jax.ShapeDtypeStruct((n, 3 * ATT_WIDTH), F32)],
        compiler_params=_params("parallel"),
        name="in_proj",
    )(x, g, wqkv, wz, wba, watt)


def _split(a):
    hi = a.astype(BF16)
    return hi, (a - hi.astype(F32)).astype(BF16)


def _out_proj_kernel(of_ref, ob_ref, z_ref, gdn_ref, att_ref, x_ref, wdn_ref, watt_ref, g_ref, wr_ref,
                     x1_ref, h2_ref, aff_ref, afft_ref):
    dot = lambda a, b: jnp.dot(a, b, preferred_element_type=F32)
    o = of_ref[...].astype(F32) + ob_ref[...].astype(F32)
    z = z_ref[...]
    gated = []
    for h in range(DN_HEADS):
        hs = slice(h * DN_HEAD_DIM, (h + 1) * DN_HEAD_DIM)
        oh = o[:, hs]
        oh = oh * lax.rsqrt(jnp.mean(oh * oh, axis=-1, keepdims=True) + EPS) * gdn_ref[...]
        zh = z[:, hs]
        gated.append((oh * (zh * jax.nn.sigmoid(zh))).astype(BF16))
    dn = jnp.concatenate(gated, axis=1)
    x1 = x_ref[...] + dot(dn, wdn_ref[...]) + dot(att_ref[...], watt_ref[...])
    x1_ref[...] = x1
    h2 = x1 * lax.rsqrt(jnp.mean(x1 * x1, axis=-1, keepdims=True) + EPS) * g_ref[...]
    h2_ref[...] = h2
    hh, hl = _split(h2)
    wh, wl = _split(wr_ref[...])
    logits = dot(hh, wh) + (dot(hh, wl) + dot(hl, wh))
    lane = lax.broadcasted_iota(jnp.int32, logits.shape, 1)
    logits = jnp.where(lane < N_EXPERTS, logits, NEG)
    p = jnp.exp(logits - jnp.max(logits, axis=-1, keepdims=True))
    aff = p / jnp.sum(p, axis=-1, keepdims=True)
    aff_ref[...] = aff
    afft_ref[...] = aff.T[:N_EXPERTS, :]


def _out_proj(o_f, o_b, z, dn_norm_g, att, x, w_out, g, w_router, tm=512):
    n = x.shape[0]
    wb = w_out.astype(BF16)
    wdn, watt = wb[:DN_WIDTH], wb[DN_WIDTH:]
    wr = jnp.pad(w_router.astype(F32), ((0, 0), (0, LANE - N_EXPERTS)))
    full = lambda a: pl.BlockSpec(a.shape, lambda i: (0, 0))
    row = lambda w: pl.BlockSpec((tm, w), lambda i: (i, 0))
    return pl.pallas_call(
        _out_proj_kernel,
        grid=(n // tm,),
        in_specs=[row(DN_WIDTH), row(DN_WIDTH), row(DN_WIDTH), full(dn_norm_g), row(ATT_WIDTH),
                  row(D_MODEL), full(wdn), full(watt), full(g), full(wr)],
        out_specs=[row(D_MODEL), row(D_MODEL), row(LANE), pl.BlockSpec((N_EXPERTS, tm), lambda i: (0, i))],
        out_shape=[jax.ShapeDtypeStruct((n, D_MODEL), F32), jax.ShapeDtypeStruct((n, D_MODEL), F32),
                   jax.ShapeDtypeStruct((n, LANE), F32), jax.ShapeDtypeStruct((N_EXPERTS, n), F32)],
        compiler_params=_params("parallel"),
        name="out_proj",
    )(o_f, o_b, z, dn_norm_g, att, x, wdn, watt, g, wr)


def _route_kernel(aff_ref, pos_ref, idx_ref, off_ref, *, cap):
    x = aff_ref[0]
    nb = x.shape[0]
    bits = pltpu.bitcast(x, jnp.int32)
    total = lambda m: jnp.sum(jnp.sum(m, axis=0, keepdims=True), axis=1, keepdims=True)
    dot = lambda a, b: jnp.dot(a, b, preferred_element_type=F32)
    one_hot = lambda m: jnp.where(m, 1.0, 0.0)

    def bit_step(i, prefix):
        cand = prefix | lax.shift_left(jnp.int32(1), 30 - i)
        cnt = total(one_hot(bits >= cand))
        return jnp.where(cnt >= cap, cand, prefix)

    thr = lax.fori_loop(0, 31, bit_step, jnp.zeros((1, 1), jnp.int32))
    gt = bits > thr
    eq = bits == thr
    need = cap - total(one_hot(gt))

    li = lax.broadcasted_iota(jnp.int32, (LANE, LANE), 0)
    lj = lax.broadcasted_iota(jnp.int32, (LANE, LANE), 1)
    lane_incl = one_hot(li <= lj).astype(BF16)
    ones = jnp.ones((LANE, LANE), BF16)
    bi = lax.broadcasted_iota(jnp.int32, (nb, nb), 0)
    bj = lax.broadcasted_iota(jnp.int32, (nb, nb), 1)
    blk_before = one_hot(bj < bi).astype(BF16)

    def ranks(mask):
        m = one_hot(mask).astype(BF16)
        within = dot(m, lane_incl)
        tot = dot(m, ones)
        before = dot(blk_before, tot.astype(BF16))
        return within, tot, before

    w_eq, _, b_eq = ranks(eq)
    sel = gt | (eq & (w_eq + b_eq <= need))
    within, tot, before = ranks(sel)
    pos_ref[0] = jnp.where(sel, within + before - 1.0, -1.0)
    off_ref[0] = before.T[0:SUBLANE, :].astype(jnp.int32)

    within_t = within.T.astype(BF16)
    sel_t = one_hot(sel).T.astype(BF16)
    after = before + tot
    blk_id = lax.broadcasted_iota(jnp.int32, (nb, LANE), 0).astype(F32)
    lane_id = lax.broadcasted_iota(jnp.int32, (LANE, LANE), 0).astype(F32)
    for c in range(cap // LANE):
        j = (lax.broadcasted_iota(jnp.int32, (1, LANE), 1) + c * LANE).astype(F32)
        holds = one_hot((before <= j) & (j < after))
        local = j - jnp.sum(holds * before, axis=0, keepdims=True) + 1.0
        blk = jnp.sum(holds * blk_id, axis=0, keepdims=True)
        hb = holds.astype(BF16)
        cnt_in_blk = dot(within_t, hb)
        sel_in_blk = dot(sel_t, hb)
        hit = one_hot((sel_in_blk > 0.5) & (cnt_in_blk == local))
        lane_of = jnp.sum(hit * lane_id, axis=0, keepdims=True)
        idx_ref[0, c:c + 1, :] = (blk * LANE + lane_of).astype(jnp.int32)


def _route(aff3, cap):
    e, nb, _ = aff3.shape
    blk = lambda rows: pl.BlockSpec((1, rows, LANE), lambda i: (i, 0, 0))
    return pl.pallas_call(
        functools.partial(_route_kernel, cap=cap),
        grid=(e,),
        in_specs=[blk(nb)],
        out_specs=[blk(nb), blk(cap // LANE), pl.BlockSpec((1, SUBLANE, nb), lambda i: (i, 0, 0))],
        out_shape=[jax.ShapeDtypeStruct((e, nb, LANE), F32),
                   jax.ShapeDtypeStruct((e, cap // LANE, LANE), jnp.int32),
                   jax.ShapeDtypeStruct((e, SUBLANE, nb), jnp.int32)],
        compiler_params=_params("parallel"),
        name="route",
    )(aff3)


GATHER_WINDOW = 128
GATHER_SPLIT = 4


def _sc_gather(x, idx):
    n, d0 = x.shape
    x = x.reshape(n * GATHER_SPLIT, d0 // GATHER_SPLIT)
    idx = (idx[:, None] * GATHER_SPLIT + jnp.arange(GATHER_SPLIT, dtype=idx.dtype)[None, :]).reshape(-1)
    m = idx.shape[0]
    d = x.shape[1]
    mesh = plsc.VectorSubcoreMesh(core_axis_name="c", subcore_axis_name="s")

    @pl.kernel(out_type=jax.ShapeDtypeStruct((m, d), x.dtype), mesh=mesh, scratch_types=[])
    def gather_kernel(x_hbm, i_hbm, o_hbm):
        def body(i_vmem, o_vmem):
            pltpu.sync_copy(x_hbm.at[i_vmem.at[0]], o_vmem)

        pltpu.emit_pipeline(
            body,
            grid=(m // GATHER_WINDOW,),
            in_specs=[pl.BlockSpec((1, GATHER_WINDOW), lambda i: (0, i))],
            out_specs=[pl.BlockSpec((GATHER_WINDOW, d), lambda i: (i, 0))],
            core_axis_name=("c", "s"),
            dimension_semantics=(pltpu.PARALLEL,),
        )(i_hbm, o_hbm)

    return gather_kernel(x, idx.reshape(1, m)).reshape(m // GATHER_SPLIT, d0)


def _ffn_kernel(xa_ref, xb_ref, wg_ref, wu_ref, wd_ref, ye_ref, xe_ref, acc_ref, *, tiles_a):
    r = pl.program_id(1)
    f = pl.program_id(2)

    @pl.when(f == 0)
    def _():
        acc_ref[...] = jnp.zeros_like(acc_ref)
        xe_ref[...] = jnp.where(r < tiles_a, xa_ref[0], xb_ref[0]).astype(BF16)

    xe = xe_ref[...]
    a = jnp.dot(xe, wg_ref[0].astype(BF16), preferred_element_type=F32)
    b = jnp.dot(xe, wu_ref[0].astype(BF16), preferred_element_type=F32)
    h = (a * jax.nn.sigmoid(a) * b).astype(BF16)
    acc_ref[...] += jnp.dot(h, wd_ref[0].astype(BF16), preferred_element_type=F32)

    @pl.when(f == pl.num_programs(2) - 1)
    def _():
        ye_ref[0] = acc_ref[...].astype(ye_ref.dtype)


def _expert_ffn(xe_a, xe_b, w_gate, w_up, w_down, tm=1024, tf=256):
    e, cap_a, d = xe_a.shape
    cap_b = xe_b.shape[1]
    dff = w_gate.shape[-1]
    tiles_a, tiles_b = cap_a // tm, cap_b // tm
    return pl.pallas_call(
        functools.partial(_ffn_kernel, tiles_a=tiles_a),
        grid=(e, tiles_a + tiles_b, dff // tf),
        in_specs=[pl.BlockSpec((1, tm, d), lambda i, r, f: (i, jnp.minimum(r, tiles_a - 1), 0)),
                  pl.BlockSpec((1, tm, d), lambda i, r, f: (i, jnp.maximum(r - tiles_a, 0), 0)),
                  pl.BlockSpec((1, d, tf), lambda i, r, f: (i, 0, f)),
                  pl.BlockSpec((1, d, tf), lambda i, r, f: (i, 0, f)),
                  pl.BlockSpec((1, tf, d), lambda i, r, f: (i, f, 0))],
        out_specs=pl.BlockSpec((1, tm, d), lambda i, r, f: (i, r, 0)),
        out_shape=jax.ShapeDtypeStruct((e, cap_a + cap_b, d), BF16),
        scratch_shapes=[pltpu.VMEM((tm, d), BF16), pltpu.VMEM((tm, d), F32)],
        compiler_params=_params("parallel", "parallel", "arbitrary"),
        name="expert_ffn",
    )(xe_a, xe_b, w_gate, w_up, w_down)


COMBINE_ROWS = LANE + 16


def _combine_kernel(r0_ref, x1_ref, aff_ref, pos_ref, g_ref, *rest):
    ye_refs, y_ref = rest[:N_EXPERTS], rest[N_EXPERTS]
    i = pl.program_id(0)
    nb = pl.num_programs(0)
    t = x1_ref.shape[0]
    pos_t = jnp.concatenate([pos_ref[:, 0, :], jnp.zeros((LANE - N_EXPERTS, LANE), F32)], axis=0).T
    aff = aff_ref[...]
    col = lax.broadcasted_iota(jnp.int32, (t, COMBINE_ROWS), 1)
    acc = x1_ref[...]
    for e in range(N_EXPERTS):
        r0 = r0_ref[e * nb + i]
        slot = jnp.broadcast_to(pos_t[:, e:e + 1], (t, COMBINE_ROWS))
        pick = jnp.where(slot == (col + r0).astype(F32), 1.0, 0.0).astype(BF16)
        contrib = jnp.dot(pick, ye_refs[e][...], preferred_element_type=F32)
        acc = acc + jnp.broadcast_to(aff[:, e:e + 1], contrib.shape) * contrib
    y_ref[...] = acc * lax.rsqrt(jnp.mean(acc * acc, axis=-1, keepdims=True) + EPS) * g_ref[...]


def _combine(x1, aff, pos3, off, ye, row0, cap, final_g):
    n, d = x1.shape
    nb = n // LANE
    assert row0 % 16 == 0
    r0 = jnp.minimum(off[:, 0, :] // 16 * 16, cap - COMBINE_ROWS)
    pos4 = pos3.reshape(N_EXPERTS, nb, 1, LANE)
    ye_spec = lambda e: pl.BlockSpec((None, pl.Element(COMBINE_ROWS), pl.Element(d)),
                                     lambda i, r0_ref: (e, pl.multiple_of(r0_ref[e * nb + i] + row0, 16), 0))
    grid_spec = pltpu.PrefetchScalarGridSpec(
        num_scalar_prefetch=1,
        grid=(nb,),
        in_specs=[pl.BlockSpec((LANE, d), lambda i, r: (i, 0)),
                  pl.BlockSpec((LANE, LANE), lambda i, r: (i, 0)),
                  pl.BlockSpec((N_EXPERTS, None, 1, LANE), lambda i, r: (0, i, 0, 0)),
                  pl.BlockSpec((1, d), lambda i, r: (0, 0))] + [ye_spec(e) for e in range(N_EXPERTS)],
        out_specs=pl.BlockSpec((LANE, d), lambda i, r: (i, 0)),
    )
    return pl.pallas_call(
        _combine_kernel,
        grid_spec=grid_spec,
        out_shape=jax.ShapeDtypeStruct((n, d), F32),
        compiler_params=_params("parallel"),
        name="moe_combine",
    )(r0.reshape(-1).astype(jnp.int32), x1, aff, pos4, final_g, *([ye] * N_EXPERTS))


def _dn_prep_kernel(x_ref, xp_ref, xn_ref, w_ref, q_ref, k_ref, v_ref, kt_ref):
    t = pl.program_id(1)
    nt = pl.num_programs(1)
    x = x_ref[0]
    rows = x.shape[0]
    row = lax.broadcasted_iota(jnp.int32, (rows, 1), 0)
    before = jnp.where(t > 0, xp_ref[0, SUBLANE - 1:SUBLANE, :], 0.0)
    after = jnp.where(t < nt - 1, xn_ref[0, 0:1, :], 0.0)
    x_prev = jnp.where(row == 0, before, pltpu.roll(x, 1, axis=0))
    x_next = jnp.where(row == rows - 1, after, pltpu.roll(x, rows - 1, axis=0))
    y = w_ref[0:1, :] * x_prev + w_ref[1:2, :] * x + w_ref[2:3, :] * x_next
    y = y * jax.nn.sigmoid(y)
    for h in range(DN_HEADS):
        for part, ref, scale in ((0, q_ref, DN_HEAD_DIM ** -0.5), (1, k_ref, 1.0)):
            c0 = part * DN_WIDTH + h * DN_HEAD_DIM
            a = y[:, c0:c0 + DN_HEAD_DIM]
            inv = lax.rsqrt(jnp.sum(a * a, axis=-1, keepdims=True) + EPS) * scale
            a = a * inv
            ref[0, :, h * DN_HEAD_DIM:(h + 1) * DN_HEAD_DIM] = a.astype(ref.dtype)
            if part == 1:
                for j in range(rows // CHUNK):
                    kt_ref[0, j, h * DN_HEAD_DIM:(h + 1) * DN_HEAD_DIM, :] = (
                        a[j * CHUNK:(j + 1) * CHUNK, :].T.astype(kt_ref.dtype))
    v_ref[0] = y[:, 2 * DN_WIDTH:].astype(v_ref.dtype)


def _dn_prep(qkv, conv_w, tile=512):
    b, s, c = qkv.shape
    tile = min(tile, s)
    nt = s // tile
    per = tile // SUBLANE
    nsub = s // SUBLANE
    out = jax.ShapeDtypeStruct((b, s, DN_WIDTH), BF16)
    ospec = pl.BlockSpec((1, tile, DN_WIDTH), lambda bi, ti: (bi, ti, 0))
    return pl.pallas_call(
        _dn_prep_kernel,
        grid=(b, nt),
        in_specs=[pl.BlockSpec((1, tile, c), lambda bi, ti: (bi, ti, 0)),
                  pl.BlockSpec((1, SUBLANE, c), lambda bi, ti: (bi, jnp.maximum(ti * per - 1, 0), 0)),
                  pl.BlockSpec((1, SUBLANE, c), lambda bi, ti: (bi, jnp.minimum((ti + 1) * per, nsub - 1), 0)),
                  pl.BlockSpec((CONV_K, c), lambda bi, ti: (0, 0))],
        out_specs=[ospec, ospec, ospec,
                   pl.BlockSpec((1, tile // CHUNK, DN_WIDTH, CHUNK), lambda bi, ti: (bi, ti, 0, 0))],
        out_shape=[out, out, out, jax.ShapeDtypeStruct((b, s // CHUNK, DN_WIDTH, CHUNK), BF16)],
        compiler_params=_params("parallel", "parallel"),
        name="dn_prep",
    )(qkv, qkv, qkv, conv_w)


def _lane_bcast(a, col, width=LANE):
    return jnp.broadcast_to(a[:, col:col + 1], (a.shape[0], width))


DN_TILE = 1024
INV_STEPS = 5
assert 2 ** (INV_STEPS + 1) == CHUNK
DN_UNROLL = 16


def _dn_scan_kernel(q_ref, k_ref, v_ref, kt_ref, ba_ref, alog_ref, dtb_ref, o_ref,
                    s_ref, xbuf, tbuf, abuf, rbuf, uwbuf, qgbuf, kdbuf, lbuf, nbuf, obuf, ecbuf, *, reverse):
    @pl.when(pl.program_id(1) == 0)
    def _():
        s_ref[...] = jnp.zeros_like(s_ref)

    tile = q_ref.shape[1]
    nc = tile // CHUNK
    n_prob = nc * DN_HEADS
    unroll = min(DN_UNROLL, n_prob)
    beta_col = DN_HEADS if reverse else 0
    a_col = 2 * DN_HEADS + (DN_HEADS if reverse else 0)
    ri = lax.broadcasted_iota(jnp.int32, (CHUNK, CHUNK), 0)
    ci = lax.broadcasted_iota(jnp.int32, (CHUNK, CHUNK), 1)
    tri = (ri <= ci) if reverse else (ri >= ci)
    strict = (ri < ci) if reverse else (ri > ci)
    eye = jnp.where(ri == ci, 1.0, 0.0)
    row = lax.broadcasted_iota(jnp.int32, (CHUNK, LANE), 0)
    a_scale = jnp.exp(alog_ref[...])
    dtb = dtb_ref[...]
    dot = lambda a, b: jnp.dot(a, b, preferred_element_type=F32)

    def build(c, carry):
        rows = pl.ds(pl.multiple_of(c * CHUNK, CHUNK), CHUNK)
        raw = ba_ref[0, rows, :]
        beta_all = jax.nn.sigmoid(raw)
        xs = raw + dtb
        g = -a_scale * (jnp.maximum(xs, 0.0) + jnp.log(1.0 + jnp.exp(-jnp.abs(xs))))
        gs = g
        sh = 1
        while sh < CHUNK:
            if reverse:
                gs = gs + jnp.where(row < CHUNK - sh, pltpu.roll(gs, CHUNK - sh, axis=0), 0.0)
            else:
                gs = gs + jnp.where(row >= sh, pltpu.roll(gs, sh, axis=0), 0.0)
            sh *= 2
        gs_t = gs.T
        g_last = gs[0:1, :] if reverse else gs[CHUNK - 1:CHUNK, :]
        eg_all = jnp.exp(gs)
        ek_t = jnp.exp(g_last.T - gs_t)
        ecbuf[c] = jnp.broadcast_to(jnp.exp(g_last), (SUBLANE, LANE))
        for h in range(DN_HEADS):
            p = c * DN_HEADS + h
            hs = slice(h * DN_HEAD_DIM, (h + 1) * DN_HEAD_DIM)
            q = q_ref[0, rows, hs].astype(F32)
            k = k_ref[0, rows, hs]
            v = v_ref[0, rows, hs].astype(F32)
            beta = _lane_bcast(beta_all, beta_col + h)
            eg = _lane_bcast(eg_all, a_col + h)
            diff = _lane_bcast(gs, a_col + h, CHUNK) - gs_t[a_col + h:a_col + h + 1, :]
            decay = jnp.where(tri, jnp.exp(jnp.where(tri, diff, 0.0)), 0.0)
            kb = k.astype(F32) * beta
            kq = lax.dot_general(jnp.concatenate([kb, q], axis=0).astype(BF16), k,
                                 (((1,), (1,)), ((), ())), preferred_element_type=F32)
            lower = jnp.where(strict, kq[:CHUNK] * decay, 0.0)
            xbuf[p] = (-lower).astype(BF16)
            tbuf[p] = eye - lower
            abuf[p] = (kq[CHUNK:] * decay).astype(BF16)
            rbuf[p] = jnp.concatenate([v * beta, kb * eg], axis=1).astype(BF16)
            qgbuf[p] = q * eg
            kdbuf[p] = (kt_ref[0, c, hs, :].astype(F32)
                        * ek_t[a_col + h:a_col + h + 1, :]).astype(BF16)
        return carry

    lax.fori_loop(0, nc, build, 0)

    for step in range(INV_STEPS + 1):
        def double(p, carry, first=(step == 0), last=(step == INV_STEPS)):
            x = xbuf[p]
            if not first:
                t = tbuf[p]
                tbuf[p] = t + dot(t.astype(BF16), x)
            if not last:
                xbuf[p] = dot(x, x).astype(BF16)
            return carry
        lax.fori_loop(0, n_prob, double, 0, unroll=unroll)

    def solve(p, carry):
        uwbuf[p] = dot(tbuf[p].astype(BF16), rbuf[p]).astype(BF16)
        return carry

    lax.fori_loop(0, n_prob, solve, 0, unroll=unroll)

    def fold(p, carry):
        uw = uwbuf[p]
        a_uw = dot(abuf[p], uw)
        k_uw = dot(kdbuf[p], uw)
        obuf[p] = a_uw[:, :DN_HEAD_DIM]
        nbuf[p] = k_uw[:, :DN_HEAD_DIM]
        lbuf[p, :DN_HEAD_DIM, :] = k_uw[:, DN_HEAD_DIM:].astype(BF16)
        lbuf[p, DN_HEAD_DIM:, :] = (qgbuf[p] - a_uw[:, DN_HEAD_DIM:]).astype(BF16)
        return carry

    lax.fori_loop(0, n_prob, fold, 0, unroll=unroll)

    def scan(jc, carry):
        c = (nc - 1 - jc) if reverse else jc
        rows = pl.ds(pl.multiple_of(c * CHUNK, CHUNK), CHUNK)
        ec_all = ecbuf[c]
        for h in range(DN_HEADS):
            p = c * DN_HEADS + h
            state = s_ref[h]
            r = dot(lbuf[p], state.astype(BF16))
            s_ref[h] = state * _lane_bcast(ec_all[0:1], a_col + h) - r[:DN_HEAD_DIM] + nbuf[p]
            o = r[DN_HEAD_DIM:] + obuf[p]
            o_ref[0, rows, h * DN_HEAD_DIM:(h + 1) * DN_HEAD_DIM] = o.astype(o_ref.dtype)
        return carry

    lax.fori_loop(0, nc, scan, 0)


def _dn_scan(q, k, v, kt, ba, a_log_row, dt_bias_row, reverse, tile=DN_TILE):
    b, s, _ = q.shape
    tile = min(tile, s)
    nt = s // tile
    nc = tile // CHUNK
    n_prob = nc * DN_HEADS
    tmap = (lambda bi, ti: (bi, nt - 1 - ti, 0)) if reverse else (lambda bi, ti: (bi, ti, 0))
    wide = pl.BlockSpec((1, tile, DN_WIDTH), tmap)
    par = pl.BlockSpec((1, LANE), lambda bi, ti: (0, 0))
    return pl.pallas_call(
        functools.partial(_dn_scan_kernel, reverse=reverse),
        grid=(b, nt),
        in_specs=[wide, wide, wide,
                  pl.BlockSpec((1, nc, DN_WIDTH, CHUNK), lambda bi, ti: tmap(bi, ti) + (0,)),
                  pl.BlockSpec((1, tile, LANE), tmap), par, par],
        out_specs=wide,
        out_shape=jax.ShapeDtypeStruct((b, s, DN_WIDTH), BF16),
        scratch_shapes=[pltpu.VMEM((DN_HEADS, DN_HEAD_DIM, DN_HEAD_DIM), F32),
                        pltpu.VMEM((n_prob, CHUNK, CHUNK), BF16),
                        pltpu.VMEM((n_prob, CHUNK, CHUNK), F32),
                        pltpu.VMEM((n_prob, CHUNK, CHUNK), BF16),
                        pltpu.VMEM((n_prob, CHUNK, 2 * DN_HEAD_DIM), BF16),
                        pltpu.VMEM((n_prob, CHUNK, 2 * DN_HEAD_DIM), BF16),
                        pltpu.VMEM((n_prob, CHUNK, DN_HEAD_DIM), F32),
                        pltpu.VMEM((n_prob, DN_HEAD_DIM, CHUNK), BF16),
                        pltpu.VMEM((n_prob, DN_HEAD_DIM + CHUNK, DN_HEAD_DIM), BF16),
                        pltpu.VMEM((n_prob, DN_HEAD_DIM, DN_HEAD_DIM), F32),
                        pltpu.VMEM((n_prob, CHUNK, DN_HEAD_DIM), F32),
                        pltpu.VMEM((nc, SUBLANE, LANE), F32)],
        compiler_params=_params("parallel", "arbitrary"),
        name="dn_scan_bwd" if reverse else "dn_scan_fwd",
    )(q, k, v, kt, ba, a_log_row, dt_bias_row)


def _gate_rows(a_log_fwd, a_log_bwd, dt_bias_fwd, dt_bias_bwd):
    pad = lambda f, b: jnp.pad(jnp.concatenate([f, b]).astype(F32), (2 * DN_HEADS, LANE - 4 * DN_HEADS))[None]
    return pad(a_log_fwd, a_log_bwd), pad(dt_bias_fwd, dt_bias_bwd)


def _deltanet(qkv, ba, conv_w, a_log_fwd, dt_bias_fwd, a_log_bwd, dt_bias_bwd):
    q, k, v, kt = _dn_prep(qkv, conv_w)
    alog, dtb = _gate_rows(a_log_fwd, a_log_bwd, dt_bias_fwd, dt_bias_bwd)
    o_f = _dn_scan(q, k, v, kt, ba, alog, dtb, reverse=False)
    o_b = _dn_scan(q, k, v, kt, ba, alog, dtb, reverse=True)
    return o_f, o_b


def _t5_bucket(rel):
    nb = NUM_BUCKETS // 2
    ret = jnp.where(rel > 0, nb, 0)
    n = jnp.abs(rel)
    max_exact = nb // 2
    nf = jnp.maximum(n, 1).astype(F32)
    large = max_exact + (jnp.log(nf / max_exact) / math.log(MAX_DISTANCE / max_exact)
                         * (nb - max_exact)).astype(jnp.int32)
    large = jnp.minimum(large, nb - 1)
    return ret + jnp.where(n < max_exact, n, large)


ATT_TILE = 2048
ATT_HALO = BAND * max(d for _, d in DILATED_PATTERNS)
QBLK = 2 * BAND
KBLK = QBLK + 2 * BAND


def _attn_bias_table(rel_bias):
    delta = (jnp.arange(KBLK) - BAND)[None, :] - jnp.arange(QBLK)[:, None]
    tabs = []
    for window, dil in DILATED_PATTERNS:
        half = window // (2 * dil)
        bias = rel_bias.astype(F32)[_t5_bucket(delta * dil)].transpose(2, 0, 1)
        tabs.append(jnp.where((jnp.abs(delta) <= half)[None], bias, NEG))
    return jnp.stack(tabs)


def _attn_kernel(q_ref, kp_ref, kc_ref, kn_ref, vp_ref, vc_ref, vn_ref, bias_ref, o_ref,
                 qd, kd, vd, od, md, ld, o_sc, m_sc, l_sc, *, seq_len):
    t = pl.program_id(1)
    lane = lax.broadcasted_iota(jnp.int32, (QBLK, LANE), 1)
    low = lane < ATT_HEAD_DIM
    scale = ATT_HEAD_DIM ** -0.5
    kcol = lax.broadcasted_iota(jnp.int32, (1, KBLK), 1)
    ones = jnp.ones((KBLK, LANE), BF16)

    for p, (window, dil) in enumerate(DILATED_PATTERNS):
        lt = ATT_TILE // dil
        lh = ATT_HALO // dil
        span = lt + 2 * lh
        nblk = lt // QBLK
        n_pos = seq_len // dil
        for r in range(dil):
            sl = lambda n: pl.ds(r, n, stride=dil) if dil > 1 else pl.ds(0, n)
            qd[pl.ds(r * lt, lt), :] = q_ref[0, sl(lt), :] * scale
            off = r * span
            for (kr, vr, n) in ((kp_ref, vp_ref, lh), (kc_ref, vc_ref, lt), (kn_ref, vn_ref, lh)):
                kd[pl.ds(off, n), :] = kr[0, sl(n), :].astype(BF16)
                vd[pl.ds(off, n), :] = vr[0, sl(n), :].astype(BF16)
                off += n

        def body(i, carry):
            r = i // nblk
            j = i % nblk
            q0 = pl.multiple_of(r * lt + j * QBLK, QBLK)
            k0 = pl.multiple_of(r * span + lh + j * QBLK - BAND, BAND)
            q = qd[pl.ds(q0, QBLK), :]
            k = kd[pl.ds(k0, KBLK), :]
            v1 = jnp.concatenate([vd[pl.ds(k0, KBLK), :], ones], axis=1)
            pos = t * lt + j * QBLK - BAND + kcol
            valid = (pos >= 0) & (pos < n_pos)
            res = []
            for h in range(2):
                qh = jnp.where(low, q, 0.0) if h == 0 else jnp.where(low, 0.0, q)
                s = lax.dot_general(qh.astype(BF16), k, (((1,), (1,)), ((), ())),
                                    preferred_element_type=F32) + bias_ref[p, h]
                s = jnp.where(valid, s, NEG)
                m = jnp.max(s, axis=-1, keepdims=True)
                e = jnp.exp(s - m).astype(BF16)
                res.append((jnp.dot(e, v1, preferred_element_type=F32), m))
            (pv0, m0), (pv1, m1) = res
            od[pl.ds(q0, QBLK), :] = jnp.where(low, pv0[:, :LANE], pv1[:, :LANE])
            ld[pl.ds(q0, QBLK), :] = jnp.where(low, pv0[:, LANE:], pv1[:, LANE:])
            md[pl.ds(q0, QBLK), :] = jnp.where(low, m0, m1)
            return carry

        lax.fori_loop(0, dil * nblk, body, 0, unroll=2)
        for r in range(dil):
            sl = pl.ds(r, lt, stride=dil) if dil > 1 else pl.ds(0, lt)
            src = pl.ds(r * lt, lt)
            o_sc[p, sl, :] = od[src, :]
            m_sc[p, sl, :] = md[src, :]
            l_sc[p, sl, :] = ld[src, :]

    n_p = len(DILATED_PATTERNS)
    mx = m_sc[0]
    for p in range(1, n_p):
        mx = jnp.maximum(mx, m_sc[p])
    num = jnp.zeros((ATT_TILE, LANE), F32)
    den = jnp.zeros((ATT_TILE, LANE), F32)
    for p in range(n_p):
        w = jnp.exp(m_sc[p] - mx)
        num += o_sc[p] * w
        den += l_sc[p] * w
    o_ref[0] = (num / den).astype(o_ref.dtype)


def _dilated_attention(att, rel_bias):
    b, s, _ = att.shape
    assert s % ATT_TILE == 0
    nt = s // ATT_TILE
    nh = s // ATT_HALO
    per = ATT_TILE // ATT_HALO
    hp = ATT_WIDTH // LANE
    bias = _attn_bias_table(rel_bias)
    cur = lambda col0: pl.BlockSpec((1, ATT_TILE, LANE), lambda bi, ti, hi: (bi, ti, col0 + hi))
    prev = lambda col0: pl.BlockSpec(
        (1, ATT_HALO, LANE), lambda bi, ti, hi: (bi, jnp.maximum(ti * per - 1, 0), col0 + hi))
    nxt = lambda col0: pl.BlockSpec(
        (1, ATT_HALO, LANE), lambda bi, ti, hi: (bi, jnp.minimum((ti + 1) * per, nh - 1), col0 + hi))
    n_p = len(DILATED_PATTERNS)
    tile_f32 = pltpu.VMEM((ATT_TILE, LANE), F32)
    span_bf16 = pltpu.VMEM((ATT_TILE + 2 * ATT_HALO, LANE), BF16)
    per_pattern = pltpu.VMEM((n_p, ATT_TILE, LANE), F32)
    return pl.pallas_call(
        functools.partial(_attn_kernel, seq_len=s),
        grid=(b, nt, hp),
        in_specs=[cur(0), prev(hp), cur(hp), nxt(hp), prev(2 * hp), cur(2 * hp), nxt(2 * hp),
                  pl.BlockSpec((n_p, 2, QBLK, KBLK), lambda bi, ti, hi: (0, hi, 0, 0))],
        out_specs=pl.BlockSpec((1, ATT_TILE, LANE), lambda bi, ti, hi: (bi, ti, hi)),
        out_shape=jax.ShapeDtypeStruct((b, s, ATT_WIDTH), BF16),
        scratch_shapes=[tile_f32, span_bf16, span_bf16, tile_f32, tile_f32, tile_f32,
                        per_pattern, per_pattern, per_pattern],
        compiler_params=_params("parallel", "parallel", "parallel"),
        name="dilated_attention",
    )(att, att, att, att, att, att, att, bias)


def _mix(x, rel_bias, norm1_g, w_in, conv_w, a_log_fwd, dt_bias_fwd, a_log_bwd, dt_bias_bwd,
         dn_norm_g, w_out, norm2_g, w_router):
    bsz, s, d = x.shape
    xf = x.reshape(bsz * s, d)
    qkv, z, ba, att = _in_proj(xf, norm1_g[None], w_in)
    shp = lambda a: a.reshape(bsz, s, a.shape[-1])
    o_f, o_b = _deltanet(shp(qkv), shp(ba), conv_w, a_log_fwd, dt_bias_fwd, a_log_bwd, dt_bias_bwd)
    att_out = _dilated_attention(shp(att), rel_bias)
    flat = lambda a: a.reshape(bsz * s, a.shape[-1])
    return _out_proj(flat(o_f), flat(o_b), z, dn_norm_g[None], flat(att_out), xf, w_out, norm2_g[None],
                     w_router)


def kernel(x_prompt, x_sample, rel_bias, norm1_g, w_in, conv_w, a_log_fwd, dt_bias_fwd, a_log_bwd,
           dt_bias_bwd, dn_norm_g, w_out, norm2_g, w_router, w_gate, w_up, w_down, final_norm_g):
    groups = (x_prompt, x_sample)
    parts = []
    for x in groups:
        x1, h2, aff, afft = _mix(x, rel_bias, norm1_g[0], w_in[0], conv_w[0], a_log_fwd[0], dt_bias_fwd[0],
                                 a_log_bwd[0], dt_bias_bwd[0], dn_norm_g[0], w_out[0], norm2_g[0],
                                 w_router[0])
        n_tok = x1.shape[0]
        cap = CAPACITY_FACTOR * n_tok // N_EXPERTS
        pos, idx, off = _route(afft.reshape(N_EXPERTS, n_tok // LANE, LANE), cap)
        xe = _sc_gather(h2, idx.reshape(-1)).reshape(N_EXPERTS, cap, D_MODEL)
        parts.append((x1, aff, pos, off, xe, cap))
    ye = _expert_ffn(parts[0][4], parts[1][4], w_gate[0], w_up[0], w_down[0])
    outs = []
    row0 = 0
    for x, (x1, aff, pos, off, _, cap) in zip(groups, parts):
        outs.append(_combine(x1, aff, pos, off, ye, row0, cap, final_norm_g[None]).reshape(x.shape))
        row0 += cap
    return tuple(outs)
```

```python
import functools
import math

import jax
import jax.numpy as jnp
from jax import lax
from jax.experimental import pallas as pl
from jax.experimental.pallas import tpu as pltpu
from jax.experimental.pallas import tpu_sc as plsc

D_MODEL = 1024
DN_HEADS = 4
DN_HEAD_DIM = 128
DN_WIDTH = DN_HEADS * DN_HEAD_DIM
ATT_HEADS = 8
ATT_HEAD_DIM = 64
ATT_WIDTH = ATT_HEADS * ATT_HEAD_DIM
CONV_K = 3
CHUNK = 64
DILATED_PATTERNS = ((128, 1), (512, 4), (2048, 16))
BAND = 64
NUM_BUCKETS = 32
MAX_DISTANCE = 1024
N_EXPERTS = 16
CAPACITY_FACTOR = 2
EXPERT_D_FF = 2816
EPS = 1e-6
NEG = -1e30
GATE_COLS = 4 * DN_HEADS
LANE = 128
SUBLANE = 8
GATHER_WINDOW = 128
GATHER_SPLIT = 8

F32 = jnp.float32
BF16 = jnp.bfloat16

VMEM_LIMIT = 56 * 1024 * 1024


def _params(*sem):
    return pltpu.CompilerParams(dimension_semantics=sem, vmem_limit_bytes=VMEM_LIMIT)


def _in_proj_kernel(x_ref, g_ref, wqkv_ref, wz_ref, wba_ref, watt_ref,
                    qkv_ref, z_ref, ba_ref, att_ref):
    x = x_ref[...]
    h = x * lax.rsqrt(jnp.mean(x * x, axis=-1, keepdims=True) + EPS) * g_ref[...]
    hb = h.astype(BF16)
    qkv_ref[...] = jnp.dot(hb, wqkv_ref[...], preferred_element_type=F32)
    z_ref[...] = jnp.dot(hb, wz_ref[...], preferred_element_type=F32)
    ba_ref[...] = jnp.dot(hb, wba_ref[...], preferred_element_type=F32)
    att_ref[...] = jnp.dot(hb, watt_ref[...], preferred_element_type=F32)


def _in_proj(x, g, w_in, tm=512):
    n = x.shape[0]
    o_qkv = 3 * DN_WIDTH
    o_z = o_qkv + DN_WIDTH
    o_ba = o_z + GATE_COLS
    wb = w_in.astype(BF16)
    wqkv = wb[:, :o_qkv]
    wz = wb[:, o_qkv:o_z]
    wba = jnp.pad(wb[:, o_z:o_ba], ((0, 0), (0, LANE - GATE_COLS)))
    watt = wb[:, o_ba:]
    full = lambda a: pl.BlockSpec(a.shape, lambda i: (0, 0))
    row = lambda w: pl.BlockSpec((tm, w), lambda i: (i, 0))
    return pl.pallas_call(
        _in_proj_kernel,
        grid=(n // tm,),
        in_specs=[row(D_MODEL), full(g), full(wqkv), full(wz), full(wba), full(watt)],
        out_specs=[row(o_qkv), row(DN_WIDTH), row(LANE), row(3 * ATT_WIDTH)],
        out_shape=[jax.ShapeDtypeStruct((n, o_qkv), F32),
                   jax.ShapeDtypeStruct((n, DN_WIDTH), F32),
                   jax.ShapeDtypeStruct((n, LANE), F32),
                   jax.ShapeDtypeStruct((n, 3 * ATT_WIDTH), F32)],
        compiler_params=_params("parallel"),
        name="in_proj",
    )(x, g, wqkv, wz, wba, watt)


def _split(a):
    hi = a.astype(BF16)
    return hi, (a - hi.astype(F32)).astype(BF16)


def _out_proj_kernel(of_ref, ob_ref, z_ref, gdn_ref, att_ref, x_ref, wdn_ref, watt_ref, g_ref, wr_ref,
                     x1_ref, h2_ref, aff_ref, afft_ref):
    dot = lambda a, b: jnp.dot(a, b, preferred_element_type=F32)
    o = of_ref[...].astype(F32) + ob_ref[...].astype(F32)
    z = z_ref[...]
    gated = []
    for h in range(DN_HEADS):
        hs = slice(h * DN_HEAD_DIM, (h + 1) * DN_HEAD_DIM)
        oh = o[:, hs]
        oh = oh * lax.rsqrt(jnp.mean(oh * oh, axis=-1, keepdims=True) + EPS) * gdn_ref[...]
        zh = z[:, hs]
        gated.append((oh * (zh * jax.nn.sigmoid(zh))).astype(BF16))
    dn = jnp.concatenate(gated, axis=1)
    x1 = x_ref[...] + dot(dn, wdn_ref[...]) + dot(att_ref[...], watt_ref[...])
    x1_ref[...] = x1
    h2 = x1 * lax.rsqrt(jnp.mean(x1 * x1, axis=-1, keepdims=True) + EPS) * g_ref[...]
    rows = h2.shape[0]
    piece = D_MODEL // GATHER_SPLIT
    for q in range(GATHER_SPLIT):
        h2_ref[pl.ds(q, rows, stride=GATHER_SPLIT), :] = h2[:, q * piece:(q + 1) * piece]
    hh, hl = _split(h2)
    wh, wl = _split(wr_ref[...])
    logits = dot(hh, wh) + (dot(hh, wl) + dot(hl, wh))
    lane = lax.broadcasted_iota(jnp.int32, logits.shape, 1)
    logits = jnp.where(lane < N_EXPERTS, logits, NEG)
    p = jnp.exp(logits - jnp.max(logits, axis=-1, keepdims=True))
    aff = p / jnp.sum(p, axis=-1, keepdims=True)
    aff_ref[...] = aff
    for j in range(rows // LANE):
        afft_ref[:, j, :] = aff[j * LANE:(j + 1) * LANE, :].T[:N_EXPERTS, :]


def _out_proj(o_f, o_b, z, dn_norm_g, att, x, w_out, g, w_router, tm=1024):
    n = x.shape[0]
    wb = w_out.astype(BF16)
    wdn, watt = wb[:DN_WIDTH], wb[DN_WIDTH:]
    wr = jnp.pad(w_router.astype(F32), ((0, 0), (0, LANE - N_EXPERTS)))
    full = lambda a: pl.BlockSpec(a.shape, lambda i: (0, 0))
    row = lambda w: pl.BlockSpec((tm, w), lambda i: (i, 0))
    return pl.pallas_call(
        _out_proj_kernel,
        grid=(n // tm,),
        in_specs=[row(DN_WIDTH), row(DN_WIDTH), row(DN_WIDTH), full(dn_norm_g), row(ATT_WIDTH),
                  row(D_MODEL), full(wdn), full(watt), full(g), full(wr)],
        out_specs=[row(D_MODEL),
                   pl.BlockSpec((tm * GATHER_SPLIT, D_MODEL // GATHER_SPLIT), lambda i: (i, 0)),
                   row(LANE),
                   pl.BlockSpec((N_EXPERTS, tm // LANE, LANE), lambda i: (0, i, 0))],
        out_shape=[jax.ShapeDtypeStruct((n, D_MODEL), F32),
                   jax.ShapeDtypeStruct((n * GATHER_SPLIT, D_MODEL // GATHER_SPLIT), F32),
                   jax.ShapeDtypeStruct((n, LANE), F32),
                   jax.ShapeDtypeStruct((N_EXPERTS, n // LANE, LANE), F32)],
        compiler_params=_params("parallel"),
        name="out_proj",
    )(o_f, o_b, z, dn_norm_g, att, x, wdn, watt, g, wr)


def _route_kernel(aff_ref, pos_ref, idx_ref, off_ref, *, cap):
    x = aff_ref[0]
    nb = x.shape[0]
    bits = pltpu.bitcast(x, jnp.int32)
    total = lambda m: jnp.sum(jnp.sum(m, axis=0, keepdims=True), axis=1, keepdims=True)
    dot = lambda a, b: jnp.dot(a, b, preferred_element_type=F32)
    one_hot = lambda m: jnp.where(m, 1.0, 0.0)

    def bit_step(i, prefix):
        cand = prefix | lax.shift_left(jnp.int32(1), 30 - i)
        cnt = total(one_hot(bits >= cand))
        return jnp.where(cnt >= cap, cand, prefix)

    thr = lax.fori_loop(0, 31, bit_step, jnp.zeros((1, 1), jnp.int32))
    gt = bits > thr
    eq = bits == thr
    need = cap - total(one_hot(gt))

    li = lax.broadcasted_iota(jnp.int32, (LANE, LANE), 0)
    lj = lax.broadcasted_iota(jnp.int32, (LANE, LANE), 1)
    lane_incl = one_hot(li <= lj).astype(BF16)
    ones = jnp.ones((LANE, LANE), BF16)
    bi = lax.broadcasted_iota(jnp.int32, (nb, nb), 0)
    bj = lax.broadcasted_iota(jnp.int32, (nb, nb), 1)
    blk_before = one_hot(bj < bi).astype(BF16)

    def ranks(mask):
        m = one_hot(mask).astype(BF16)
        within = dot(m, lane_incl)
        tot = dot(m, ones)
        before = dot(blk_before, tot.astype(BF16))
        return within, tot, before

    w_eq, _, b_eq = ranks(eq)
    sel = gt | (eq & (w_eq + b_eq <= need))
    within, tot, before = ranks(sel)
    pos_ref[0] = jnp.where(sel, within + before - 1.0, -1.0)
    off_ref[0] = before.T[0:SUBLANE, :].astype(jnp.int32)

    within_t = within.T.astype(BF16)
    sel_t = one_hot(sel).T.astype(BF16)
    after = before + tot
    blk_id = lax.broadcasted_iota(jnp.int32, (nb, LANE), 0).astype(F32)
    lane_id = lax.broadcasted_iota(jnp.int32, (LANE, LANE), 0).astype(F32)
    for c in range(cap // LANE):
        j = (lax.broadcasted_iota(jnp.int32, (1, LANE), 1) + c * LANE).astype(F32)
        holds = one_hot((before <= j) & (j < after))
        local = j - jnp.sum(holds * before, axis=0, keepdims=True) + 1.0
        blk = jnp.sum(holds * blk_id, axis=0, keepdims=True)
        hb = holds.astype(BF16)
        cnt_in_blk = dot(within_t, hb)
        sel_in_blk = dot(sel_t, hb)
        hit = one_hot((sel_in_blk > 0.5) & (cnt_in_blk == local))
        lane_of = jnp.sum(hit * lane_id, axis=0, keepdims=True)
        idx_ref[0, c:c + 1, :] = (blk * LANE + lane_of).astype(jnp.int32)


def _route(aff3, cap):
    e, nb, _ = aff3.shape
    blk = lambda rows: pl.BlockSpec((1, rows, LANE), lambda i: (i, 0, 0))
    return pl.pallas_call(
        functools.partial(_route_kernel, cap=cap),
        grid=(e,),
        in_specs=[blk(nb)],
        out_specs=[blk(nb), blk(cap // LANE), pl.BlockSpec((1, SUBLANE, nb), lambda i: (i, 0, 0))],
        out_shape=[jax.ShapeDtypeStruct((e, nb, LANE), F32),
                   jax.ShapeDtypeStruct((e, cap // LANE, LANE), jnp.int32),
                   jax.ShapeDtypeStruct((e, SUBLANE, nb), jnp.int32)],
        compiler_params=_params("parallel"),
        name="route",
    )(aff3)


def _sc_gather(x, idx):
    idx =(idx[:, None] * GATHER_SPLIT + jnp.arange(GATHER_SPLIT, dtype=idx.dtype)[None, :]).reshape(-1)
    m = idx.shape[0]
    d = x.shape[1]
    mesh = plsc.VectorSubcoreMesh(core_axis_name="c", subcore_axis_name="s")

    @pl.kernel(out_type=jax.ShapeDtypeStruct((m, d), x.dtype), mesh=mesh, scratch_types=[])
    def gather_kernel(x_hbm, i_hbm, o_hbm):
        def body(i_vmem, o_vmem):
            pltpu.sync_copy(x_hbm.at[i_vmem.at[0]], o_vmem)

        pltpu.emit_pipeline(
            body,
            grid=(m // GATHER_WINDOW,),
            in_specs=[pl.BlockSpec((1, GATHER_WINDOW), lambda i: (0, i))],
            out_specs=[pl.BlockSpec((GATHER_WINDOW, d), lambda i: (i, 0))],
            core_axis_name=("c", "s"),
            dimension_semantics=(pltpu.PARALLEL,),
        )(i_hbm, o_hbm)

    return gather_kernel(x, idx.reshape(1, m))


def _ffn_kernel(xa_ref, xb_ref, wg_ref, wu_ref, wd_ref, ye_ref, xe_ref, acc_ref, *, tiles_a):
    r = pl.program_id(1)
    f = pl.program_id(2)

    @pl.when(f == 0)
    def _():
        acc_ref[...] = jnp.zeros_like(acc_ref)
        piece = xa_ref.shape[2]
        for q in range(GATHER_SPLIT):
            rows = pl.ds(q, xe_ref.shape[0], stride=GATHER_SPLIT)
            xe_ref[:, q * piece:(q + 1) * piece] = jnp.where(
                r < tiles_a, xa_ref[0, rows, :], xb_ref[0, rows, :]).astype(BF16)

    xe = xe_ref[...]
    a = jnp.dot(xe, wg_ref[0].astype(BF16), preferred_element_type=F32)
    b = jnp.dot(xe, wu_ref[0].astype(BF16), preferred_element_type=F32)
    h = (a * jax.nn.sigmoid(a) * b).astype(BF16)
    acc_ref[...] += jnp.dot(h, wd_ref[0].astype(BF16), preferred_element_type=F32)

    @pl.when(f == pl.num_programs(2) - 1)
    def _():
        ye_ref[0] = acc_ref[...].astype(ye_ref.dtype)


def _expert_ffn(xe_a, xe_b, w_gate, w_up, w_down, tm=1024, tf=256):
    e, rows_a, piece = xe_a.shape
    cap_a, cap_b, d = rows_a // GATHER_SPLIT, xe_b.shape[1] // GATHER_SPLIT, piece * GATHER_SPLIT
    dff = w_gate.shape[-1]
    tiles_a, tiles_b = cap_a // tm, cap_b // tm
    return pl.pallas_call(
        functools.partial(_ffn_kernel, tiles_a=tiles_a),
        grid=(e, tiles_a + tiles_b, dff // tf),
        in_specs=[pl.BlockSpec((1, tm * GATHER_SPLIT, piece), lambda i, r, f: (i, jnp.minimum(r, tiles_a - 1), 0)),
                  pl.BlockSpec((1, tm * GATHER_SPLIT, piece), lambda i, r, f: (i, jnp.maximum(r - tiles_a, 0), 0)),
                  pl.BlockSpec((1, d, tf), lambda i, r, f: (i, 0, f)),
                  pl.BlockSpec((1, d, tf), lambda i, r, f: (i, 0, f)),
                  pl.BlockSpec((1, tf, d), lambda i, r, f: (i, f, 0))],
        out_specs=pl.BlockSpec((1, tm, d), lambda i, r, f: (i, r, 0)),
        out_shape=jax.ShapeDtypeStruct((e, cap_a + cap_b, d), BF16),
        scratch_shapes=[pltpu.VMEM((tm, d), BF16), pltpu.VMEM((tm, d), F32)],
        compiler_params=_params("parallel", "parallel", "arbitrary"),
        name="expert_ffn",
    )(xe_a, xe_b, w_gate, w_up, w_down)


COMBINE_ROWS = LANE + 16


def _combine_kernel(r0_ref, x1_ref, aff_ref, pos_ref, g_ref, *rest):
    ye_refs, y_ref = rest[:N_EXPERTS], rest[N_EXPERTS]
    i = pl.program_id(0)
    nb = pl.num_programs(0)
    t = x1_ref.shape[0]
    pos_t = jnp.concatenate([pos_ref[:, 0, :], jnp.zeros((LANE - N_EXPERTS, LANE), F32)], axis=0).T
    aff = aff_ref[...]
    col = lax.broadcasted_iota(jnp.int32, (t, COMBINE_ROWS), 1)
    acc = x1_ref[...]
    for e in range(N_EXPERTS):
        r0 = r0_ref[e * nb + i]
        slot = jnp.broadcast_to(pos_t[:, e:e + 1], (t, COMBINE_ROWS))
        pick = jnp.where(slot == (col + r0).astype(F32), 1.0, 0.0).astype(BF16)
        contrib = jnp.dot(pick, ye_refs[e][...], preferred_element_type=F32)
        acc = acc + jnp.broadcast_to(aff[:, e:e + 1], contrib.shape) * contrib
    y_ref[...] = acc * lax.rsqrt(jnp.mean(acc * acc, axis=-1, keepdims=True) + EPS) * g_ref[...]


def _combine(x1, aff, pos3, off, ye, row0, cap, final_g):
    n, d = x1.shape
    nb = n // LANE
    assert row0 % 16 == 0
    r0 = jnp.minimum(off[:, 0, :] // 16 * 16, cap - COMBINE_ROWS)
    pos4 = pos3.reshape(N_EXPERTS, nb, 1, LANE)
    ye_spec = lambda e: pl.BlockSpec((None, pl.Element(COMBINE_ROWS), pl.Element(d)),
                                     lambda i, r0_ref: (e, pl.multiple_of(r0_ref[e * nb + i] + row0, 16), 0))
    grid_spec = pltpu.PrefetchScalarGridSpec(
        num_scalar_prefetch=1,
        grid=(nb,),
        in_specs=[pl.BlockSpec((LANE, d), lambda i, r: (i, 0)),
                  pl.BlockSpec((LANE, LANE), lambda i, r: (i, 0)),
                  pl.BlockSpec((N_EXPERTS, None, 1, LANE), lambda i, r: (0, i, 0, 0)),
                  pl.BlockSpec((1, d), lambda i, r: (0, 0))] + [ye_spec(e) for e in range(N_EXPERTS)],
        out_specs=pl.BlockSpec((LANE, d), lambda i, r: (i, 0)),
    )
    return pl.pallas_call(
        _combine_kernel,
        grid_spec=grid_spec,
        out_shape=jax.ShapeDtypeStruct((n, d), F32),
        compiler_params=_params("parallel"),
        name="moe_combine",
    )(r0.reshape(-1).astype(jnp.int32), x1, aff, pos4, final_g, *([ye] * N_EXPERTS))


def _dn_prep_kernel(x_ref, xp_ref, xn_ref, w_ref, q_ref, k_ref, v_ref, kt_ref):
    t = pl.program_id(1)
    nt = pl.num_programs(1)
    x = x_ref[0]
    rows = x.shape[0]
    row = lax.broadcasted_iota(jnp.int32, (rows, 1), 0)
    before = jnp.where(t > 0, xp_ref[0, SUBLANE - 1:SUBLANE, :], 0.0)
    after = jnp.where(t < nt - 1, xn_ref[0, 0:1, :], 0.0)
    x_prev = jnp.where(row == 0, before, pltpu.roll(x, 1, axis=0))
    x_next = jnp.where(row == rows - 1, after, pltpu.roll(x, rows - 1, axis=0))
    y = w_ref[0:1, :] * x_prev + w_ref[1:2, :] * x + w_ref[2:3, :] * x_next
    y = y * jax.nn.sigmoid(y)
    for h in range(DN_HEADS):
        for part, ref, scale in ((0, q_ref, DN_HEAD_DIM ** -0.5), (1, k_ref, 1.0)):
            c0 = part * DN_WIDTH + h * DN_HEAD_DIM
            a = y[:, c0:c0 + DN_HEAD_DIM]
            inv = lax.rsqrt(jnp.sum(a * a, axis=-1, keepdims=True) + EPS) * scale
            a = a * inv
            ref[0, :, h * DN_HEAD_DIM:(h + 1) * DN_HEAD_DIM] = a.astype(ref.dtype)
            if part == 1:
                for j in range(rows // CHUNK):
                    kt_ref[0, j, h * DN_HEAD_DIM:(h + 1) * DN_HEAD_DIM, :] = (
                        a[j * CHUNK:(j + 1) * CHUNK, :].T.astype(kt_ref.dtype))
    v_ref[0] = y[:, 2 * DN_WIDTH:].astype(v_ref.dtype)


def _dn_prep(qkv, conv_w, tile=512):
    b, s, c = qkv.shape
    tile = min(tile, s)
    nt = s // tile
    per = tile // SUBLANE
    nsub = s // SUBLANE
    out = jax.ShapeDtypeStruct((b, s, DN_WIDTH), BF16)
    ospec = pl.BlockSpec((1, tile, DN_WIDTH), lambda bi, ti: (bi, ti, 0))
    return pl.pallas_call(
        _dn_prep_kernel,
        grid=(b, nt),
        in_specs=[pl.BlockSpec((1, tile, c), lambda bi, ti: (bi, ti, 0)),
                  pl.BlockSpec((1, SUBLANE, c), lambda bi, ti: (bi, jnp.maximum(ti * per - 1, 0), 0)),
                  pl.BlockSpec((1, SUBLANE, c), lambda bi, ti: (bi, jnp.minimum((ti + 1) * per, nsub - 1), 0)),
                  pl.BlockSpec((CONV_K, c), lambda bi, ti: (0, 0))],
        out_specs=[ospec, ospec, ospec,
                   pl.BlockSpec((1, tile // CHUNK, DN_WIDTH, CHUNK), lambda bi, ti: (bi, ti, 0, 0))],
        out_shape=[out, out, out, jax.ShapeDtypeStruct((b, s // CHUNK, DN_WIDTH, CHUNK), BF16)],
        compiler_params=_params("parallel", "parallel"),
        name="dn_prep",
    )(qkv, qkv, qkv, conv_w)


def _lane_bcast(a, col, width=LANE):
    return jnp.broadcast_to(a[:, col:col + 1], (a.shape[0], width))


DN_TILE = 1024
INV_STEPS = 5
assert 2 ** (INV_STEPS + 1) == CHUNK
DN_UNROLL = 16


def _dn_scan_kernel(q_ref, k_ref, v_ref, kt_ref, ba_ref, alog_ref, dtb_ref, o_ref,
                    s_ref, xbuf, tbuf, abuf, rbuf, uwbuf, qgbuf, kdbuf, lbuf, nbuf, obuf, ecbuf, *, reverse):
    @pl.when(pl.program_id(1) == 0)
    def _():
        s_ref[...] = jnp.zeros_like(s_ref)

    tile = q_ref.shape[1]
    nc = tile // CHUNK
    n_prob = nc * DN_HEADS
    unroll = min(DN_UNROLL, n_prob)
    beta_col = DN_HEADS if reverse else 0
    a_col = 2 * DN_HEADS + (DN_HEADS if reverse else 0)
    ri = lax.broadcasted_iota(jnp.int32, (CHUNK, CHUNK), 0)
    ci = lax.broadcasted_iota(jnp.int32, (CHUNK, CHUNK), 1)
    tri = (ri <= ci) if reverse else (ri >= ci)
    strict = (ri < ci) if reverse else (ri > ci)
    eye = jnp.where(ri == ci, 1.0, 0.0)
    row = lax.broadcasted_iota(jnp.int32, (CHUNK, LANE), 0)
    a_scale = jnp.exp(alog_ref[...])
    dtb = dtb_ref[...]
    dot = lambda a, b: jnp.dot(a, b, preferred_element_type=F32)

    def build(c, carry):
        rows = pl.ds(pl.multiple_of(c * CHUNK, CHUNK), CHUNK)
        raw = ba_ref[0, rows, :]
        beta_all = jax.nn.sigmoid(raw)
        xs = raw + dtb
        g = -a_scale * (jnp.maximum(xs, 0.0) + jnp.log(1.0 + jnp.exp(-jnp.abs(xs))))
        gs = g
        sh = 1
        while sh < CHUNK:
            if reverse:
                gs = gs + jnp.where(row < CHUNK - sh, pltpu.roll(gs, CHUNK - sh, axis=0), 0.0)
            else:
                gs = gs + jnp.where(row >= sh, pltpu.roll(gs, sh, axis=0), 0.0)
            sh *= 2
        gs_t = gs.T
        g_last = gs[0:1, :] if reverse else gs[CHUNK - 1:CHUNK, :]
        eg_all = jnp.exp(gs)
        ek_t = jnp.exp(g_last.T - gs_t)
        ecbuf[c] = jnp.broadcast_to(jnp.exp(g_last), (SUBLANE, LANE))
        for h in range(DN_HEADS):
            p = c * DN_HEADS + h
            hs = slice(h * DN_HEAD_DIM, (h + 1) * DN_HEAD_DIM)
            q = q_ref[0, rows, hs].astype(F32)
            k = k_ref[0, rows, hs]
            v = v_ref[0, rows, hs].astype(F32)
            beta = _lane_bcast(beta_all, beta_col + h)
            eg = _lane_bcast(eg_all, a_col + h)
            diff = _lane_bcast(gs, a_col + h, CHUNK) - gs_t[a_col + h:a_col + h + 1, :]
            decay = jnp.where(tri, jnp.exp(jnp.where(tri, diff, 0.0)), 0.0)
            kb = k.astype(F32) * beta
            kq = lax.dot_general(jnp.concatenate([kb, q], axis=0).astype(BF16), k,
                                 (((1,), (1,)), ((), ())), preferred_element_type=F32)
            lower = jnp.where(strict, kq[:CHUNK] * decay, 0.0)
            xbuf[p] = (-lower).astype(BF16)
            tbuf[p] = eye - lower
            abuf[p] = (kq[CHUNK:] * decay).astype(BF16)
            rbuf[p] = jnp.concatenate([v * beta, kb * eg], axis=1).astype(BF16)
            qgbuf[p] = q * eg
            kdbuf[p] = (kt_ref[0, c, hs, :].astype(F32)
                        * ek_t[a_col + h:a_col + h + 1, :]).astype(BF16)
        return carry

    lax.fori_loop(0, nc, build, 0)

    for step in range(INV_STEPS + 1):
        def double(p, carry, first=(step == 0), last=(step == INV_STEPS)):
            x = xbuf[p]
            if not first:
                t = tbuf[p]
                tbuf[p] = t + dot(t.astype(BF16), x)
            if not last:
                xbuf[p] = dot(x, x).astype(BF16)
            return carry
        lax.fori_loop(0, n_prob, double, 0, unroll=unroll)

    def solve(p, carry):
        uwbuf[p] = dot(tbuf[p].astype(BF16), rbuf[p]).astype(BF16)
        return carry

    lax.fori_loop(0, n_prob, solve, 0, unroll=unroll)

    def fold(p, carry):
        uw = uwbuf[p]
        a_uw = dot(abuf[p], uw)
        k_uw = dot(kdbuf[p], uw)
        obuf[p] = a_uw[:, :DN_HEAD_DIM]
        nbuf[p] = k_uw[:, :DN_HEAD_DIM]
        lbuf[p, :DN_HEAD_DIM, :] = k_uw[:, DN_HEAD_DIM:].astype(BF16)
        lbuf[p, DN_HEAD_DIM:, :] = (qgbuf[p] - a_uw[:, DN_HEAD_DIM:]).astype(BF16)
        return carry

    lax.fori_loop(0, n_prob, fold, 0, unroll=unroll)

    def scan(jc, carry):
        c = (nc - 1 - jc) if reverse else jc
        rows = pl.ds(pl.multiple_of(c * CHUNK, CHUNK), CHUNK)
        ec_all = ecbuf[c]
        for h in range(DN_HEADS):
            p = c * DN_HEADS + h
            state = s_ref[h]
            r = dot(lbuf[p], state.astype(BF16))
            s_ref[h] = state * _lane_bcast(ec_all[0:1], a_col + h) - r[:DN_HEAD_DIM] + nbuf[p]
            o = r[DN_HEAD_DIM:] + obuf[p]
            o_ref[0, rows, h * DN_HEAD_DIM:(h + 1) * DN_HEAD_DIM] = o.astype(o_ref.dtype)
        return carry

    lax.fori_loop(0, nc, scan, 0)


def _dn_scan(q, k, v, kt, ba, a_log_row, dt_bias_row, reverse, tile=DN_TILE):
    b, s, _ = q.shape
    tile = min(tile, s)
    nt = s // tile
    nc = tile // CHUNK
    n_prob = nc * DN_HEADS
    tmap = (lambda bi, ti: (bi, nt - 1 - ti, 0)) if reverse else (lambda bi, ti: (bi, ti, 0))
    wide = pl.BlockSpec((1, tile, DN_WIDTH), tmap)
    par = pl.BlockSpec((1, LANE), lambda bi, ti: (0, 0))
    return pl.pallas_call(
        functools.partial(_dn_scan_kernel, reverse=reverse),
        grid=(b, nt),
        in_specs=[wide, wide, wide,
                  pl.BlockSpec((1, nc, DN_WIDTH, CHUNK), lambda bi, ti: tmap(bi, ti) + (0,)),
                  pl.BlockSpec((1, tile, LANE), tmap), par, par],
        out_specs=wide,
        out_shape=jax.ShapeDtypeStruct((b, s, DN_WIDTH), BF16),
        scratch_shapes=[pltpu.VMEM((DN_HEADS, DN_HEAD_DIM, DN_HEAD_DIM), F32),
                        pltpu.VMEM((n_prob, CHUNK, CHUNK), BF16),
                        pltpu.VMEM((n_prob, CHUNK, CHUNK), F32),
                        pltpu.VMEM((n_prob, CHUNK, CHUNK), BF16),
                        pltpu.VMEM((n_prob, CHUNK, 2 * DN_HEAD_DIM), BF16),
                        pltpu.VMEM((n_prob, CHUNK, 2 * DN_HEAD_DIM), BF16),
                        pltpu.VMEM((n_prob, CHUNK, DN_HEAD_DIM), F32),
                        pltpu.VMEM((n_prob, DN_HEAD_DIM, CHUNK), BF16),
                        pltpu.VMEM((n_prob, DN_HEAD_DIM + CHUNK, DN_HEAD_DIM), BF16),
                        pltpu.VMEM((n_prob, DN_HEAD_DIM, DN_HEAD_DIM), F32),
                        pltpu.VMEM((n_prob, CHUNK, DN_HEAD_DIM), F32),
                        pltpu.VMEM((nc, SUBLANE, LANE), F32)],
        compiler_params=_params("parallel", "arbitrary"),
        name="dn_scan_bwd" if reverse else "dn_scan_fwd",
    )(q, k, v, kt, ba, a_log_row, dt_bias_row)


def _gate_rows(a_log_fwd, a_log_bwd, dt_bias_fwd, dt_bias_bwd):
    pad = lambda f, b: jnp.pad(jnp.concatenate([f, b]).astype(F32), (2 * DN_HEADS, LANE - 4 * DN_HEADS))[None]
    return pad(a_log_fwd, a_log_bwd), pad(dt_bias_fwd, dt_bias_bwd)


def _deltanet(qkv, ba, conv_w, a_log_fwd, dt_bias_fwd, a_log_bwd, dt_bias_bwd):
    q, k, v, kt = _dn_prep(qkv, conv_w)
    alog, dtb = _gate_rows(a_log_fwd, a_log_bwd, dt_bias_fwd, dt_bias_bwd)
    o_f = _dn_scan(q, k, v, kt, ba, alog, dtb, reverse=False)
    o_b = _dn_scan(q, k, v, kt, ba, alog, dtb, reverse=True)
    return o_f, o_b


def _t5_bucket(rel):
    nb = NUM_BUCKETS // 2
    ret = jnp.where(rel > 0, nb, 0)
    n = jnp.abs(rel)
    max_exact = nb // 2
    nf = jnp.maximum(n, 1).astype(F32)
    large = max_exact + (jnp.log(nf / max_exact) / math.log(MAX_DISTANCE / max_exact)
                         * (nb - max_exact)).astype(jnp.int32)
    large = jnp.minimum(large, nb - 1)
    return ret + jnp.where(n < max_exact, n, large)


ATT_TILE = 2048
ATT_HALO = BAND * max(d for _, d in DILATED_PATTERNS)
QBLK = 2 * BAND
KBLK = QBLK + 2 * BAND


def _attn_bias_table(rel_bias):
    delta = (jnp.arange(KBLK) - BAND)[None, :] - jnp.arange(QBLK)[:, None]
    tabs = []
    for window, dil in DILATED_PATTERNS:
        half = window // (2 * dil)
        bias = rel_bias.astype(F32)[_t5_bucket(delta * dil)].transpose(2, 0, 1)
        tabs.append(jnp.where((jnp.abs(delta) <= half)[None], bias, NEG))
    return jnp.stack(tabs)


def _attn_kernel(q_ref, kp_ref, kc_ref, kn_ref, vp_ref, vc_ref, vn_ref, bias_ref, o_ref,
                 qd, kd, vd, od, md, ld, o_sc, m_sc, l_sc, *, seq_len):
    t = pl.program_id(1)
    lane = lax.broadcasted_iota(jnp.int32, (QBLK, LANE), 1)
    low = lane < ATT_HEAD_DIM
    scale = ATT_HEAD_DIM ** -0.5
    kcol = lax.broadcasted_iota(jnp.int32, (1, KBLK), 1)
    ones = jnp.ones((KBLK, LANE), BF16)

    for p, (window, dil) in enumerate(DILATED_PATTERNS):
        lt = ATT_TILE // dil
        lh = ATT_HALO // dil
        span = lt + 2 * lh
        nblk = lt // QBLK
        n_pos = seq_len // dil
        for r in range(dil):
            sl = lambda n: pl.ds(r, n, stride=dil) if dil > 1 else pl.ds(0, n)
            qd[pl.ds(r * lt, lt), :] = q_ref[0, sl(lt), :] * scale
            off = r * span
            for (kr, vr, n) in ((kp_ref, vp_ref, lh), (kc_ref, vc_ref, lt), (kn_ref, vn_ref, lh)):
                kd[pl.ds(off, n), :] = kr[0, sl(n), :].astype(BF16)
                vd[pl.ds(off, n), :] = vr[0, sl(n), :].astype(BF16)
                off += n

        def body(i, carry):
            r = i // nblk
            j = i % nblk
            q0 = pl.multiple_of(r * lt + j * QBLK, QBLK)
            k0 = pl.multiple_of(r * span + lh + j * QBLK - BAND, BAND)
            q = qd[pl.ds(q0, QBLK), :]
            k = kd[pl.ds(k0, KBLK), :]
            v1 = jnp.concatenate([vd[pl.ds(k0, KBLK), :], ones], axis=1)
            pos = t * lt + j * QBLK - BAND + kcol
            valid = (pos >= 0) & (pos < n_pos)
            res = []
            for h in range(2):
                qh = jnp.where(low, q, 0.0) if h == 0 else jnp.where(low, 0.0, q)
                s = lax.dot_general(qh.astype(BF16), k, (((1,), (1,)), ((), ())),
                                    preferred_element_type=F32) + bias_ref[p, h]
                s = jnp.where(valid, s, NEG)
                m = jnp.max(s, axis=-1, keepdims=True)
                e = jnp.exp(s - m).astype(BF16)
                res.append((jnp.dot(e, v1, preferred_element_type=F32), m))
            (pv0, m0), (pv1, m1) = res
            od[pl.ds(q0, QBLK), :] = jnp.where(low, pv0[:, :LANE], pv1[:, :LANE])
            ld[pl.ds(q0, QBLK), :] = jnp.where(low, pv0[:, LANE:], pv1[:, LANE:])
            md[pl.ds(q0, QBLK), :] = jnp.where(low, m0, m1)
            return carry

        lax.fori_loop(0, dil * nblk, body, 0, unroll=2)
        for r in range(dil):
            sl = pl.ds(r, lt, stride=dil) if dil > 1 else pl.ds(0, lt)
            src = pl.ds(r * lt, lt)
            o_sc[p, sl, :] = od[src, :]
            m_sc[p, sl, :] = md[src, :]
            l_sc[p, sl, :] = ld[src, :]

    n_p = len(DILATED_PATTERNS)
    mx = m_sc[0]
    for p in range(1, n_p):
        mx = jnp.maximum(mx, m_sc[p])
    num = jnp.zeros((ATT_TILE, LANE), F32)
    den = jnp.zeros((ATT_TILE, LANE), F32)
    for p in range(n_p):
        w = jnp.exp(m_sc[p] - mx)
        num += o_sc[p] * w
        den += l_sc[p] * w
    o_ref[0] = (num / den).astype(o_ref.dtype)


def _dilated_attention(att, rel_bias):
    b, s, _ = att.shape
    assert s % ATT_TILE == 0
    nt = s // ATT_TILE
    nh = s // ATT_HALO
    per = ATT_TILE // ATT_HALO
    hp = ATT_WIDTH // LANE
    bias = _attn_bias_table(rel_bias)
    cur = lambda col0: pl.BlockSpec((1, ATT_TILE, LANE), lambda bi, ti, hi: (bi, ti, col0 + hi))
    prev = lambda col0: pl.BlockSpec(
        (1, ATT_HALO, LANE), lambda bi, ti, hi: (bi, jnp.maximum(ti * per - 1, 0), col0 + hi))
    nxt = lambda col0: pl.BlockSpec(
        (1, ATT_HALO, LANE), lambda bi, ti, hi: (bi, jnp.minimum((ti + 1) * per, nh - 1), col0 + hi))
    n_p = len(DILATED_PATTERNS)
    tile_f32 = pltpu.VMEM((ATT_TILE, LANE), F32)
    span_bf16 = pltpu.VMEM((ATT_TILE + 2 * ATT_HALO, LANE), BF16)
    per_pattern = pltpu.VMEM((n_p, ATT_TILE, LANE), F32)
    return pl.pallas_call(
        functools.partial(_attn_kernel, seq_len=s),
        grid=(b, nt, hp),
        in_specs=[cur(0), prev(hp), cur(hp), nxt(hp), prev(2 * hp), cur(2 * hp), nxt(2 * hp),
                  pl.BlockSpec((n_p, 2, QBLK, KBLK), lambda bi, ti, hi: (0, hi, 0, 0))],
        out_specs=pl.BlockSpec((1, ATT_TILE, LANE), lambda bi, ti, hi: (bi, ti, hi)),
        out_shape=jax.ShapeDtypeStruct((b, s, ATT_WIDTH), BF16),
        scratch_shapes=[tile_f32, span_bf16, span_bf16, tile_f32, tile_f32, tile_f32,
                        per_pattern, per_pattern, per_pattern],
        compiler_params=_params("parallel", "parallel", "parallel"),
        name="dilated_attention",
    )(att, att, att, att, att, att, att, bias)


def _mix(x, rel_bias, norm1_g, w_in, conv_w, a_log_fwd, dt_bias_fwd, a_log_bwd, dt_bias_bwd,
         dn_norm_g, w_out, norm2_g, w_router):
    bsz, s, d = x.shape
    xf = x.reshape(bsz * s, d)
    qkv, z, ba, att = _in_proj(xf, norm1_g[None], w_in)
    shp = lambda a: a.reshape(bsz, s, a.shape[-1])
    o_f, o_b = _deltanet(shp(qkv), shp(ba), conv_w, a_log_fwd, dt_bias_fwd, a_log_bwd, dt_bias_bwd)
    att_out = _dilated_attention(shp(att), rel_bias)
    flat = lambda a: a.reshape(bsz * s, a.shape[-1])
    return _out_proj(flat(o_f), flat(o_b), z, dn_norm_g[None], flat(att_out), xf, w_out, norm2_g[None],
                     w_router)


def kernel(x_prompt, x_sample, rel_bias, norm1_g, w_in, conv_w, a_log_fwd, dt_bias_fwd, a_log_bwd,
           dt_bias_bwd, dn_norm_g, w_out, norm2_g, w_router, w_gate, w_up, w_down, final_norm_g):
    groups = (x_prompt, x_sample)
    parts = []
    for x in groups:
        x1, h2, aff, afft = _mix(x, rel_bias, norm1_g[0], w_in[0], conv_w[0], a_log_fwd[0], dt_bias_fwd[0],
                                 a_log_bwd[0], dt_bias_bwd[0], dn_norm_g[0], w_out[0], norm2_g[0],
                                 w_router[0])
        n_tok = x1.shape[0]
        cap = CAPACITY_FACTOR * n_tok // N_EXPERTS
        pos, idx, off = _route(afft, cap)
        xe = _sc_gather(h2, idx.reshape(-1)).reshape(N_EXPERTS, cap * GATHER_SPLIT, D_MODEL // GATHER_SPLIT)
        parts.append((x1, aff, pos, off, xe, cap))
    ye = _expert_ffn(parts[0][4], parts[1][4], w_gate[0], w_up[0], w_down[0])
    outs = []
    row0 = 0
    for x, (x1, aff, pos, off, _, cap) in zip(groups, parts):
        outs.append(_combine(x1, aff, pos, off, ye, row0, cap, final_norm_g[None]).reshape(x.shape))
        row0 += cap
    return tuple(outs)
```

```python
import functools
import math

import jax
import jax.numpy as jnp
from jax import lax
from jax.experimental import pallas as pl
from jax.experimental.pallas import tpu as pltpu
from jax.experimental.pallas import tpu_sc as plsc

D_MODEL = 1024
DN_HEADS = 4
DN_HEAD_DIM = 128
DN_WIDTH = DN_HEADS * DN_HEAD_DIM
ATT_HEADS = 8
ATT_HEAD_DIM = 64
ATT_WIDTH = ATT_HEADS * ATT_HEAD_DIM
CONV_K = 3
CHUNK = 64
DILATED_PATTERNS = ((128, 1), (512, 4), (2048, 16))
BAND = 64
NUM_BUCKETS = 32
MAX_DISTANCE = 1024
N_EXPERTS = 16
CAPACITY_FACTOR = 2
EXPERT_D_FF = 2816
EPS = 1e-6
NEG = -1e30
GATE_COLS = 4 * DN_HEADS
LANE = 128
SUBLANE = 8
GATHER_WINDOW = 128
GATHER_SPLIT = 8

F32 = jnp.float32
BF16 = jnp.bfloat16

VMEM_LIMIT = 56 * 1024 * 1024


def _params(*sem):
    return pltpu.CompilerParams(dimension_semantics=sem, vmem_limit_bytes=VMEM_LIMIT)


def _in_proj_kernel(x_ref, g_ref, wqkv_ref, wz_ref, wba_ref, watt_ref,
                    qkv_ref, z_ref, ba_ref, att_ref):
    x = x_ref[...]
    h = x * lax.rsqrt(jnp.mean(x * x, axis=-1, keepdims=True) + EPS) * g_ref[...]
    hb = h.astype(BF16)
    qkv_ref[...] = jnp.dot(hb, wqkv_ref[...], preferred_element_type=F32)
    z_ref[...] = jnp.dot(hb, wz_ref[...], preferred_element_type=F32)
    ba_ref[...] = jnp.dot(hb, wba_ref[...], preferred_element_type=F32)
    att_ref[...] = jnp.dot(hb, watt_ref[...], preferred_element_type=F32)


def _in_proj(x, g, w_in, tm=512):
    n = x.shape[0]
    o_qkv = 3 * DN_WIDTH
    o_z = o_qkv + DN_WIDTH
    o_ba = o_z + GATE_COLS
    wb = w_in.astype(BF16)
    wqkv = wb[:, :o_qkv]
    wz = wb[:, o_qkv:o_z]
    wba = jnp.pad(wb[:, o_z:o_ba], ((0, 0), (0, LANE - GATE_COLS)))
    watt = wb[:, o_ba:]
    full = lambda a: pl.BlockSpec(a.shape, lambda i: (0, 0))
    row = lambda w: pl.BlockSpec((tm, w), lambda i: (i, 0))
    return pl.pallas_call(
        _in_proj_kernel,
        grid=(n // tm,),
        in_specs=[row(D_MODEL), full(g), full(wqkv), full(wz), full(wba), full(watt)],
        out_specs=[row(o_qkv), row(DN_WIDTH), row(LANE), row(3 * ATT_WIDTH)],
        out_shape=[jax.ShapeDtypeStruct((n, o_qkv), F32),
                   jax.ShapeDtypeStruct((n, DN_WIDTH), F32),
                   jax.ShapeDtypeStruct((n, LANE), F32),
                   jax.ShapeDtypeStruct((n, 3 * ATT_WIDTH), F32)],
        compiler_params=_params("parallel"),
        name="in_proj",
    )(x, g, wqkv, wz, wba, watt)


def _split(a):
    hi = a.astype(BF16)
    return hi, (a - hi.astype(F32)).astype(BF16)


def _out_proj_kernel(of_ref, ob_ref, z_ref, gdn_ref, att_ref, x_ref, wdn_ref, watt_ref, g_ref, wr_ref,
                     x1_ref, h2_ref, aff_ref, afft_ref):
    dot = lambda a, b: jnp.dot(a, b, preferred_element_type=F32)
    o = of_ref[...].astype(F32) + ob_ref[...].astype(F32)
    z = z_ref[...]
    gated = []
    for h in range(DN_HEADS):
        hs = slice(h * DN_HEAD_DIM, (h + 1) * DN_HEAD_DIM)
        oh = o[:, hs]
        oh = oh * lax.rsqrt(jnp.mean(oh * oh, axis=-1, keepdims=True) + EPS) * gdn_ref[...]
        zh = z[:, hs]
        gated.append((oh * (zh * jax.nn.sigmoid(zh))).astype(BF16))
    dn = jnp.concatenate(gated, axis=1)
    x1 = x_ref[...] + dot(dn, wdn_ref[...]) + dot(att_ref[...], watt_ref[...])
    x1_ref[...] = x1
    h2 = x1 * lax.rsqrt(jnp.mean(x1 * x1, axis=-1, keepdims=True) + EPS) * g_ref[...]
    rows = h2.shape[0]
    piece = D_MODEL // GATHER_SPLIT
    for q in range(GATHER_SPLIT):
        h2_ref[q] = h2[:, q * piece:(q + 1) * piece]
    hh, hl = _split(h2)
    wh, wl = _split(wr_ref[...])
    logits = dot(hh, wh) + (dot(hh, wl) + dot(hl, wh))
    lane = lax.broadcasted_iota(jnp.int32, logits.shape, 1)
    logits = jnp.where(lane < N_EXPERTS, logits, NEG)
    p = jnp.exp(logits - jnp.max(logits, axis=-1, keepdims=True))
    aff = p / jnp.sum(p, axis=-1, keepdims=True)
    aff_ref[...] = aff
    for j in range(rows // LANE):
        afft_ref[:, j, :] = aff[j * LANE:(j + 1) * LANE, :].T[:N_EXPERTS, :]


def _out_proj(o_f, o_b, z, dn_norm_g, att, x, w_out, g, w_router, tm=1024):
    n = x.shape[0]
    wb = w_out.astype(BF16)
    wdn, watt = wb[:DN_WIDTH], wb[DN_WIDTH:]
    wr = jnp.pad(w_router.astype(F32), ((0, 0), (0, LANE - N_EXPERTS)))
    full = lambda a: pl.BlockSpec(a.shape, lambda i: (0, 0))
    row = lambda w: pl.BlockSpec((tm, w), lambda i: (i, 0))
    return pl.pallas_call(
        _out_proj_kernel,
        grid=(n // tm,),
        in_specs=[row(DN_WIDTH), row(DN_WIDTH), row(DN_WIDTH), full(dn_norm_g), row(ATT_WIDTH),
                  row(D_MODEL), full(wdn), full(watt), full(g), full(wr)],
        out_specs=[row(D_MODEL),
                   pl.BlockSpec((GATHER_SPLIT, tm, D_MODEL // GATHER_SPLIT), lambda i: (0, i, 0)),
                   row(LANE),
                   pl.BlockSpec((N_EXPERTS, tm // LANE, LANE), lambda i: (0, i, 0))],
        out_shape=[jax.ShapeDtypeStruct((n, D_MODEL), F32),
                   jax.ShapeDtypeStruct((GATHER_SPLIT, n, D_MODEL // GATHER_SPLIT), F32),
                   jax.ShapeDtypeStruct((n, LANE), F32),
                   jax.ShapeDtypeStruct((N_EXPERTS, n // LANE, LANE), F32)],
        compiler_params=_params("parallel"),
        name="out_proj",
    )(o_f, o_b, z, dn_norm_g, att, x, wdn, watt, g, wr)


def _route_kernel(aff_ref, pos_ref, idx_ref, off_ref, *, cap):
    x = aff_ref[0]
    nb = x.shape[0]
    bits = pltpu.bitcast(x, jnp.int32)
    total = lambda m: jnp.sum(jnp.sum(m, axis=0, keepdims=True), axis=1, keepdims=True)
    dot = lambda a, b: jnp.dot(a, b, preferred_element_type=F32)
    one_hot = lambda m: jnp.where(m, 1.0, 0.0)

    def bit_step(i, prefix):
        cand = prefix | lax.shift_left(jnp.int32(1), 30 - i)
        cnt = total(one_hot(bits >= cand))
        return jnp.where(cnt >= cap, cand, prefix)

    thr = lax.fori_loop(0, 31, bit_step, jnp.zeros((1, 1), jnp.int32))
    gt = bits > thr
    eq = bits == thr
    need = cap - total(one_hot(gt))

    li = lax.broadcasted_iota(jnp.int32, (LANE, LANE), 0)
    lj = lax.broadcasted_iota(jnp.int32, (LANE, LANE), 1)
    lane_incl = one_hot(li <= lj).astype(BF16)
    ones = jnp.ones((LANE, LANE), BF16)
    bi = lax.broadcasted_iota(jnp.int32, (nb, nb), 0)
    bj = lax.broadcasted_iota(jnp.int32, (nb, nb), 1)
    blk_before = one_hot(bj < bi).astype(BF16)

    def ranks(mask):
        m = one_hot(mask).astype(BF16)
        within = dot(m, lane_incl)
        tot = dot(m, ones)
        before = dot(blk_before, tot.astype(BF16))
        return within, tot, before

    w_eq, _, b_eq = ranks(eq)
    sel = gt | (eq & (w_eq + b_eq <= need))
    within, tot, before = ranks(sel)
    pos_ref[0] = jnp.where(sel, within + before - 1.0, -1.0)
    off_ref[0] = before.T[0:SUBLANE, :].astype(jnp.int32)

    within_t = within.T.astype(BF16)
    sel_t = one_hot(sel).T.astype(BF16)
    after = before + tot
    blk_id = lax.broadcasted_iota(jnp.int32, (nb, LANE), 0).astype(F32)
    lane_id = lax.broadcasted_iota(jnp.int32, (LANE, LANE), 0).astype(F32)
    for c in range(cap // LANE):
        j = (lax.broadcasted_iota(jnp.int32, (1, LANE), 1) + c * LANE).astype(F32)
        holds = one_hot((before <= j) & (j < after))
        local = j - jnp.sum(holds * before, axis=0, keepdims=True) + 1.0
        blk = jnp.sum(holds * blk_id, axis=0, keepdims=True)
        hb = holds.astype(BF16)
        cnt_in_blk = dot(within_t, hb)
        sel_in_blk = dot(sel_t, hb)
        hit = one_hot((sel_in_blk > 0.5) & (cnt_in_blk == local))
        lane_of = jnp.sum(hit * lane_id, axis=0, keepdims=True)
        idx_ref[0, c:c + 1, :] = (blk * LANE + lane_of).astype(jnp.int32)


def _route(aff3, cap):
    e, nb, _ = aff3.shape
    blk = lambda rows: pl.BlockSpec((1, rows, LANE), lambda i: (i, 0, 0))
    return pl.pallas_call(
        functools.partial(_route_kernel, cap=cap),
        grid=(e,),
        in_specs=[blk(nb)],
        out_specs=[blk(nb), blk(cap // LANE), pl.BlockSpec((1, SUBLANE, nb), lambda i: (i, 0, 0))],
        out_shape=[jax.ShapeDtypeStruct((e, nb, LANE), F32),
                   jax.ShapeDtypeStruct((e, cap // LANE, LANE), jnp.int32),
                   jax.ShapeDtypeStruct((e, SUBLANE, nb), jnp.int32)],
        compiler_params=_params("parallel"),
        name="route",
    )(aff3)


def _sc_gather(x, idx):
    pieces, n, d = x.shape
    x = x.reshape(pieces * n, d)
    idx = (jnp.arange(pieces, dtype=idx.dtype)[:, None] * n + idx[None, :]).reshape(-1)
    m = idx.shape[0]
    mesh = plsc.VectorSubcoreMesh(core_axis_name="c", subcore_axis_name="s")

    @pl.kernel(out_type=jax.ShapeDtypeStruct((m, d), x.dtype), mesh=mesh, scratch_types=[])
    def gather_kernel(x_hbm, i_hbm, o_hbm):
        def body(i_vmem, o_vmem):
            pltpu.sync_copy(x_hbm.at[i_vmem.at[0]], o_vmem)

        pltpu.emit_pipeline(
            body,
            grid=(m // GATHER_WINDOW,),
            in_specs=[pl.BlockSpec((1, GATHER_WINDOW), lambda i: (0, i))],
            out_specs=[pl.BlockSpec((GATHER_WINDOW, d), lambda i: (i, 0))],
            core_axis_name=("c", "s"),
            dimension_semantics=(pltpu.PARALLEL,),
        )(i_hbm, o_hbm)

    return gather_kernel(x, idx.reshape(1, m)).reshape(pieces, m // pieces, d)


def _ffn_kernel(xa_ref, xb_ref, wg_ref, wu_ref, wd_ref, ye_ref, xe_ref, acc_ref, *, tiles_a):
    r = pl.program_id(1)
    f = pl.program_id(2)

    @pl.when(f == 0)
    def _():
        acc_ref[...] = jnp.zeros_like(acc_ref)
        piece = xa_ref.shape[2]
        for q in range(GATHER_SPLIT):
            xe_ref[:, q * piece:(q + 1) * piece] = jnp.where(r < tiles_a, xa_ref[q], xb_ref[q]).astype(BF16)

    xe = xe_ref[...]
    a = jnp.dot(xe, wg_ref[0].astype(BF16), preferred_element_type=F32)
    b = jnp.dot(xe, wu_ref[0].astype(BF16), preferred_element_type=F32)
    h = (a * jax.nn.sigmoid(a) * b).astype(BF16)
    acc_ref[...] += jnp.dot(h, wd_ref[0].astype(BF16), preferred_element_type=F32)

    @pl.when(f == pl.num_programs(2) - 1)
    def _():
        ye_ref[0] = acc_ref[...].astype(ye_ref.dtype)


def _expert_ffn(xe_a, xe_b, w_gate, w_up, w_down, tm=1024, tf=256):
    e = w_gate.shape[0]
    pieces, rows_a, piece = xe_a.shape
    cap_a, cap_b, d = rows_a // e, xe_b.shape[1] // e, piece * pieces
    dff = w_gate.shape[-1]
    tiles_a, tiles_b = cap_a // tm, cap_b // tm
    return pl.pallas_call(
        functools.partial(_ffn_kernel, tiles_a=tiles_a),
        grid=(e, tiles_a + tiles_b, dff // tf),
        in_specs=[pl.BlockSpec((pieces, tm, piece),
                               lambda i, r, f: (0, i * tiles_a + jnp.minimum(r, tiles_a - 1), 0)),
                  pl.BlockSpec((pieces, tm, piece),
                               lambda i, r, f: (0, i * tiles_b + jnp.maximum(r - tiles_a, 0), 0)),
                  pl.BlockSpec((1, d, tf), lambda i, r, f: (i, 0, f)),
                  pl.BlockSpec((1, d, tf), lambda i, r, f: (i, 0, f)),
                  pl.BlockSpec((1, tf, d), lambda i, r, f: (i, f, 0))],
        out_specs=pl.BlockSpec((1, tm, d), lambda i, r, f: (i, r, 0)),
        out_shape=jax.ShapeDtypeStruct((e, cap_a + cap_b, d), BF16),
        scratch_shapes=[pltpu.VMEM((tm, d), BF16), pltpu.VMEM((tm, d), F32)],
        compiler_params=_params("parallel", "parallel", "arbitrary"),
        name="expert_ffn",
    )(xe_a, xe_b, w_gate, w_up, w_down)


COMBINE_ROWS = LANE + 16


def _combine_kernel(r0_ref, x1_ref, aff_ref, pos_ref, g_ref, *rest):
    ye_refs, y_ref = rest[:N_EXPERTS], rest[N_EXPERTS]
    i = pl.program_id(0)
    nb = pl.num_programs(0)
    t = x1_ref.shape[0]
    pos_t = jnp.concatenate([pos_ref[:, 0, :], jnp.zeros((LANE - N_EXPERTS, LANE), F32)], axis=0).T
    aff = aff_ref[...]
    col = lax.broadcasted_iota(jnp.int32, (t, COMBINE_ROWS), 1)
    acc = x1_ref[...]
    for e in range(N_EXPERTS):
        r0 = r0_ref[e * nb + i]
        slot = jnp.broadcast_to(pos_t[:, e:e + 1], (t, COMBINE_ROWS))
        pick = jnp.where(slot == (col + r0).astype(F32), 1.0, 0.0).astype(BF16)
        contrib = jnp.dot(pick, ye_refs[e][...], preferred_element_type=F32)
        acc = acc + jnp.broadcast_to(aff[:, e:e + 1], contrib.shape) * contrib
    y_ref[...] = acc * lax.rsqrt(jnp.mean(acc * acc, axis=-1, keepdims=True) + EPS) * g_ref[...]


def _combine(x1, aff, pos3, off, ye, row0, cap, final_g):
    n, d = x1.shape
    nb = n // LANE
    assert row0 % 16 == 0
    r0 = jnp.minimum(off[:, 0, :] // 16 * 16, cap - COMBINE_ROWS)
    pos4 = pos3.reshape(N_EXPERTS, nb, 1, LANE)
    ye_spec = lambda e: pl.BlockSpec((None, pl.Element(COMBINE_ROWS), pl.Element(d)),
                                     lambda i, r0_ref: (e, pl.multiple_of(r0_ref[e * nb + i] + row0, 16), 0))
    grid_spec = pltpu.PrefetchScalarGridSpec(
        num_scalar_prefetch=1,
        grid=(nb,),
        in_specs=[pl.BlockSpec((LANE, d), lambda i, r: (i, 0)),
                  pl.BlockSpec((LANE, LANE), lambda i, r: (i, 0)),
                  pl.BlockSpec((N_EXPERTS, None, 1, LANE), lambda i, r: (0, i, 0, 0)),
                  pl.BlockSpec((1, d), lambda i, r: (0, 0))] + [ye_spec(e) for e in range(N_EXPERTS)],
        out_specs=pl.BlockSpec((LANE, d), lambda i, r: (i, 0)),
    )
    return pl.pallas_call(
        _combine_kernel,
        grid_spec=grid_spec,
        out_shape=jax.ShapeDtypeStruct((n, d), F32),
        compiler_params=_params("parallel"),
        name="moe_combine",
    )(r0.reshape(-1).astype(jnp.int32), x1, aff, pos4, final_g, *([ye] * N_EXPERTS))


def _dn_prep_kernel(x_ref, xp_ref, xn_ref, w_ref, q_ref, k_ref, v_ref, kt_ref):
    t = pl.program_id(1)
    nt = pl.num_programs(1)
    x = x_ref[0]
    rows = x.shape[0]
    row = lax.broadcasted_iota(jnp.int32, (rows, 1), 0)
    before = jnp.where(t > 0, xp_ref[0, SUBLANE - 1:SUBLANE, :], 0.0)
    after = jnp.where(t < nt - 1, xn_ref[0, 0:1, :], 0.0)
    x_prev = jnp.where(row == 0, before, pltpu.roll(x, 1, axis=0))
    x_next = jnp.where(row == rows - 1, after, pltpu.roll(x, rows - 1, axis=0))
    y = w_ref[0:1, :] * x_prev + w_ref[1:2, :] * x + w_ref[2:3, :] * x_next
    y = y * jax.nn.sigmoid(y)
    for h in range(DN_HEADS):
        for part, ref, scale in ((0, q_ref, DN_HEAD_DIM ** -0.5), (1, k_ref, 1.0)):
            c0 = part * DN_WIDTH + h * DN_HEAD_DIM
            a = y[:, c0:c0 + DN_HEAD_DIM]
            inv = lax.rsqrt(jnp.sum(a * a, axis=-1, keepdims=True) + EPS) * scale
            a = a * inv
            ref[0, :, h * DN_HEAD_DIM:(h + 1) * DN_HEAD_DIM] = a.astype(ref.dtype)
            if part == 1:
                for j in range(rows // CHUNK):
                    kt_ref[0, j, h * DN_HEAD_DIM:(h + 1) * DN_HEAD_DIM, :] = (
                        a[j * CHUNK:(j + 1) * CHUNK, :].T.astype(kt_ref.dtype))
    v_ref[0] = y[:, 2 * DN_WIDTH:].astype(v_ref.dtype)


def _dn_prep(qkv, conv_w, tile=512):
    b, s, c = qkv.shape
    tile = min(tile, s)
    nt = s // tile
    per = tile // SUBLANE
    nsub = s // SUBLANE
    out = jax.ShapeDtypeStruct((b, s, DN_WIDTH), BF16)
    ospec = pl.BlockSpec((1, tile, DN_WIDTH), lambda bi, ti: (bi, ti, 0))
    return pl.pallas_call(
        _dn_prep_kernel,
        grid=(b, nt),
        in_specs=[pl.BlockSpec((1, tile, c), lambda bi, ti: (bi, ti, 0)),
                  pl.BlockSpec((1, SUBLANE, c), lambda bi, ti: (bi, jnp.maximum(ti * per - 1, 0), 0)),
                  pl.BlockSpec((1, SUBLANE, c), lambda bi, ti: (bi, jnp.minimum((ti + 1) * per, nsub - 1), 0)),
                  pl.BlockSpec((CONV_K, c), lambda bi, ti: (0, 0))],
        out_specs=[ospec, ospec, ospec,
                   pl.BlockSpec((1, tile // CHUNK, DN_WIDTH, CHUNK), lambda bi, ti: (bi, ti, 0, 0))],
        out_shape=[out, out, out, jax.ShapeDtypeStruct((b, s // CHUNK, DN_WIDTH, CHUNK), BF16)],
        compiler_params=_params("parallel", "parallel"),
        name="dn_prep",
    )(qkv, qkv, qkv, conv_w)


def _lane_bcast(a, col, width=LANE):
    return jnp.broadcast_to(a[:, col:col + 1], (a.shape[0], width))


DN_TILE = 1024
INV_STEPS = 5
assert 2 ** (INV_STEPS + 1) == CHUNK
DN_UNROLL = 16


def _dn_scan_kernel(q_ref, k_ref, v_ref, kt_ref, ba_ref, alog_ref, dtb_ref, o_ref,
                    s_ref, xbuf, tbuf, abuf, rbuf, uwbuf, qgbuf, kdbuf, lbuf, nbuf, obuf, ecbuf, *, reverse):
    @pl.when(pl.program_id(1) == 0)
    def _():
        s_ref[...] = jnp.zeros_like(s_ref)

    tile = q_ref.shape[1]
    nc = tile // CHUNK
    n_prob = nc * DN_HEADS
    unroll = min(DN_UNROLL, n_prob)
    beta_col = DN_HEADS if reverse else 0
    a_col = 2 * DN_HEADS + (DN_HEADS if reverse else 0)
    ri = lax.broadcasted_iota(jnp.int32, (CHUNK, CHUNK), 0)
    ci = lax.broadcasted_iota(jnp.int32, (CHUNK, CHUNK), 1)
    tri = (ri <= ci) if reverse else (ri >= ci)
    strict = (ri < ci) if reverse else (ri > ci)
    eye = jnp.where(ri == ci, 1.0, 0.0)
    row = lax.broadcasted_iota(jnp.int32, (CHUNK, LANE), 0)
    a_scale = jnp.exp(alog_ref[...])
    dtb = dtb_ref[...]
    dot = lambda a, b: jnp.dot(a, b, preferred_element_type=F32)

    def build(c, carry):
        rows = pl.ds(pl.multiple_of(c * CHUNK, CHUNK), CHUNK)
        raw = ba_ref[0, rows, :]
        beta_all = jax.nn.sigmoid(raw)
        xs = raw + dtb
        g = -a_scale * (jnp.maximum(xs, 0.0) + jnp.log(1.0 + jnp.exp(-jnp.abs(xs))))
        gs = g
        sh = 1
        while sh < CHUNK:
            if reverse:
                gs = gs + jnp.where(row < CHUNK - sh, pltpu.roll(gs, CHUNK - sh, axis=0), 0.0)
            else:
                gs = gs + jnp.where(row >= sh, pltpu.roll(gs, sh, axis=0), 0.0)
            sh *= 2
        gs_t = gs.T
        g_last = gs[0:1, :] if reverse else gs[CHUNK - 1:CHUNK, :]
        eg_all = jnp.exp(gs)
        ek_t = jnp.exp(g_last.T - gs_t)
        ecbuf[c] = jnp.broadcast_to(jnp.exp(g_last), (SUBLANE, LANE))
        for h in range(DN_HEADS):
            p = c * DN_HEADS + h
            hs = slice(h * DN_HEAD_DIM, (h + 1) * DN_HEAD_DIM)
            q = q_ref[0, rows, hs].astype(F32)
            k = k_ref[0, rows, hs]
            v = v_ref[0, rows, hs].astype(F32)
            beta = _lane_bcast(beta_all, beta_col + h)
            eg = _lane_bcast(eg_all, a_col + h)
            diff = _lane_bcast(gs, a_col + h, CHUNK) - gs_t[a_col + h:a_col + h + 1, :]
            decay = jnp.where(tri, jnp.exp(jnp.where(tri, diff, 0.0)), 0.0)
            kb = k.astype(F32) * beta
            kq = lax.dot_general(jnp.concatenate([kb, q], axis=0).astype(BF16), k,
                                 (((1,), (1,)), ((), ())), preferred_element_type=F32)
            lower = jnp.where(strict, kq[:CHUNK] * decay, 0.0)
            xbuf[p] = (-lower).astype(BF16)
            tbuf[p] = eye - lower
            abuf[p] = (kq[CHUNK:] * decay).astype(BF16)
            rbuf[p] = jnp.concatenate([v * beta, kb * eg], axis=1).astype(BF16)
            qgbuf[p] = q * eg
            kdbuf[p] = (kt_ref[0, c, hs, :].astype(F32)
                        * ek_t[a_col + h:a_col + h + 1, :]).astype(BF16)
        return carry

    lax.fori_loop(0, nc, build, 0)

    for step in range(INV_STEPS + 1):
        def double(p, carry, first=(step == 0), last=(step == INV_STEPS)):
            x = xbuf[p]
            if not first:
                t = tbuf[p]
                tbuf[p] = t + dot(t.astype(BF16), x)
            if not last:
                xbuf[p] = dot(x, x).astype(BF16)
            return carry
        lax.fori_loop(0, n_prob, double, 0, unroll=unroll)

    def solve(p, carry):
        uwbuf[p] = dot(tbuf[p].astype(BF16), rbuf[p]).astype(BF16)
        return carry

    lax.fori_loop(0, n_prob, solve, 0, unroll=unroll)

    def fold(p, carry):
        uw = uwbuf[p]
        a_uw = dot(abuf[p], uw)
        k_uw = dot(kdbuf[p], uw)
        obuf[p] = a_uw[:, :DN_HEAD_DIM]
        nbuf[p] = k_uw[:, :DN_HEAD_DIM]
        lbuf[p, :DN_HEAD_DIM, :] = k_uw[:, DN_HEAD_DIM:].astype(BF16)
        lbuf[p, DN_HEAD_DIM:, :] = (qgbuf[p] - a_uw[:, DN_HEAD_DIM:]).astype(BF16)
        return carry

    lax.fori_loop(0, n_prob, fold, 0, unroll=unroll)

    def scan(jc, carry):
        c = (nc - 1 - jc) if reverse else jc
        rows = pl.ds(pl.multiple_of(c * CHUNK, CHUNK), CHUNK)
        ec_all = ecbuf[c]
        for h in range(DN_HEADS):
            p = c * DN_HEADS + h
            state = s_ref[h]
            r = dot(lbuf[p], state.astype(BF16))
            s_ref[h] = state * _lane_bcast(ec_all[0:1], a_col + h) - r[:DN_HEAD_DIM] + nbuf[p]
            o = r[DN_HEAD_DIM:] + obuf[p]
            o_ref[0, rows, h * DN_HEAD_DIM:(h + 1) * DN_HEAD_DIM] = o.astype(o_ref.dtype)
        return carry

    lax.fori_loop(0, nc, scan, 0)


def _dn_scan(q, k, v, kt, ba, a_log_row, dt_bias_row, reverse, tile=DN_TILE):
    b, s, _ = q.shape
    tile = min(tile, s)
    nt = s // tile
    nc = tile // CHUNK
    n_prob = nc * DN_HEADS
    tmap = (lambda bi, ti: (bi, nt - 1 - ti, 0)) if reverse else (lambda bi, ti: (bi, ti, 0))
    wide = pl.BlockSpec((1, tile, DN_WIDTH), tmap)
    par = pl.BlockSpec((1, LANE), lambda bi, ti: (0, 0))
    return pl.pallas_call(
        functools.partial(_dn_scan_kernel, reverse=reverse),
        grid=(b, nt),
        in_specs=[wide, wide, wide,
                  pl.BlockSpec((1, nc, DN_WIDTH, CHUNK), lambda bi, ti: tmap(bi, ti) + (0,)),
                  pl.BlockSpec((1, tile, LANE), tmap), par, par],
        out_specs=wide,
        out_shape=jax.ShapeDtypeStruct((b, s, DN_WIDTH), BF16),
        scratch_shapes=[pltpu.VMEM((DN_HEADS, DN_HEAD_DIM, DN_HEAD_DIM), F32),
                        pltpu.VMEM((n_prob, CHUNK, CHUNK), BF16),
                        pltpu.VMEM((n_prob, CHUNK, CHUNK), F32),
                        pltpu.VMEM((n_prob, CHUNK, CHUNK), BF16),
                        pltpu.VMEM((n_prob, CHUNK, 2 * DN_HEAD_DIM), BF16),
                        pltpu.VMEM((n_prob, CHUNK, 2 * DN_HEAD_DIM), BF16),
                        pltpu.VMEM((n_prob, CHUNK, DN_HEAD_DIM), F32),
                        pltpu.VMEM((n_prob, DN_HEAD_DIM, CHUNK), BF16),
                        pltpu.VMEM((n_prob, DN_HEAD_DIM + CHUNK, DN_HEAD_DIM), BF16),
                        pltpu.VMEM((n_prob, DN_HEAD_DIM, DN_HEAD_DIM), F32),
                        pltpu.VMEM((n_prob, CHUNK, DN_HEAD_DIM), F32),
                        pltpu.VMEM((nc, SUBLANE, LANE), F32)],
        compiler_params=_params("parallel", "arbitrary"),
        name="dn_scan_bwd" if reverse else "dn_scan_fwd",
    )(q, k, v, kt, ba, a_log_row, dt_bias_row)


def _gate_rows(a_log_fwd, a_log_bwd, dt_bias_fwd, dt_bias_bwd):
    pad = lambda f, b: jnp.pad(jnp.concatenate([f, b]).astype(F32), (2 * DN_HEADS, LANE - 4 * DN_HEADS))[None]
    return pad(a_log_fwd, a_log_bwd), pad(dt_bias_fwd, dt_bias_bwd)


def _deltanet(qkv, ba, conv_w, a_log_fwd, dt_bias_fwd, a_log_bwd, dt_bias_bwd):
    q, k, v, kt = _dn_prep(qkv, conv_w)
    alog, dtb = _gate_rows(a_log_fwd, a_log_bwd, dt_bias_fwd, dt_bias_bwd)
    o_f = _dn_scan(q, k, v, kt, ba, alog, dtb, reverse=False)
    o_b = _dn_scan(q, k, v, kt, ba, alog, dtb, reverse=True)
    return o_f, o_b


def _t5_bucket(rel):
    nb = NUM_BUCKETS // 2
    ret = jnp.where(rel > 0, nb, 0)
    n = jnp.abs(rel)
    max_exact = nb // 2
    nf = jnp.maximum(n, 1).astype(F32)
    large = max_exact + (jnp.log(nf / max_exact) / math.log(MAX_DISTANCE / max_exact)
                         * (nb - max_exact)).astype(jnp.int32)
    large = jnp.minimum(large, nb - 1)
    return ret + jnp.where(n < max_exact, n, large)


ATT_TILE = 2048
ATT_HALO = BAND * max(d for _, d in DILATED_PATTERNS)
QBLK = 2 * BAND
KBLK = QBLK + 2 * BAND


def _attn_bias_table(rel_bias):
    delta = (jnp.arange(KBLK) - BAND)[None, :] - jnp.arange(QBLK)[:, None]
    tabs = []
    for window, dil in DILATED_PATTERNS:
        half = window // (2 * dil)
        onehot = (_t5_bucket(delta * dil)[..., None] == jnp.arange(NUM_BUCKETS)).astype(F32)
        bias = jnp.einsum('qkb,bh->hqk', onehot, rel_bias.astype(F32),
                          precision=lax.Precision.HIGHEST)
        tabs.append(jnp.where((jnp.abs(delta) <= half)[None], bias, NEG))
    return jnp.stack(tabs)


def _attn_kernel(q_ref, kp_ref, kc_ref, kn_ref, vp_ref, vc_ref, vn_ref, bias_ref, o_ref,
                 qd, kd, vd, od, md, ld, o_sc, m_sc, l_sc, *, seq_len):
    t = pl.program_id(1)
    lane = lax.broadcasted_iota(jnp.int32, (QBLK, LANE), 1)
    low = lane < ATT_HEAD_DIM
    scale = ATT_HEAD_DIM ** -0.5
    kcol = lax.broadcasted_iota(jnp.int32, (1, KBLK), 1)
    ones = jnp.ones((KBLK, LANE), BF16)

    for p, (window, dil) in enumerate(DILATED_PATTERNS):
        lt = ATT_TILE // dil
        lh = ATT_HALO // dil
        span = lt + 2 * lh
        nblk = lt // QBLK
        n_pos = seq_len // dil
        for r in range(dil):
            sl = lambda n: pl.ds(r, n, stride=dil) if dil > 1 else pl.ds(0, n)
            qd[pl.ds(r * lt, lt), :] = q_ref[0, sl(lt), :] * scale
            off = r * span
            for (kr, vr, n) in ((kp_ref, vp_ref, lh), (kc_ref, vc_ref, lt), (kn_ref, vn_ref, lh)):
                kd[pl.ds(off, n), :] = kr[0, sl(n), :].astype(BF16)
                vd[pl.ds(off, n), :] = vr[0, sl(n), :].astype(BF16)
                off += n

        for i in range(dil * nblk):
            r, j = divmod(i, nblk)
            q0 = r * lt + j * QBLK
            k0 = r * span + lh + j * QBLK - BAND
            q = qd[pl.ds(q0, QBLK), :]
            k = kd[pl.ds(k0, KBLK), :]
            v1 = jnp.concatenate([vd[pl.ds(k0, KBLK), :], ones], axis=1)
            edge = j == 0 or j == nblk - 1
            if edge:
                pos = t * lt + j * QBLK - BAND + kcol
                valid = (pos >= 0) & (pos < n_pos)
            res = []
            for h in range(2):
                qh = jnp.where(low, q, 0.0) if h == 0 else jnp.where(low, 0.0, q)
                s = lax.dot_general(qh.astype(BF16), k, (((1,), (1,)), ((), ())),
                                    preferred_element_type=F32) + bias_ref[p, h]
                if edge:
                    s = jnp.where(valid, s, NEG)
                m = jnp.max(s, axis=-1, keepdims=True)
                e = jnp.exp(s - m).astype(BF16)
                res.append((jnp.dot(e, v1, preferred_element_type=F32), m))
            (pv0, m0), (pv1, m1) = res
            od[pl.ds(q0, QBLK), :] = jnp.where(low, pv0[:, :LANE], pv1[:, :LANE])
            ld[pl.ds(q0, QBLK), :] = jnp.where(low, pv0[:, LANE:], pv1[:, LANE:])
            md[pl.ds(q0, QBLK), :] = jnp.where(low, m0, m1)

        for r in range(dil):
            sl = pl.ds(r, lt, stride=dil) if dil > 1 else pl.ds(0, lt)
            src = pl.ds(r * lt, lt)
            o_sc[p, sl, :] = od[src, :]
            m_sc[p, sl, :] = md[src, :]
            l_sc[p, sl, :] = ld[src, :]

    n_p = len(DILATED_PATTERNS)
    mx = m_sc[0]
    for p in range(1, n_p):
        mx = jnp.maximum(mx, m_sc[p])
    num = jnp.zeros((ATT_TILE, LANE), F32)
    den = jnp.zeros((ATT_TILE, LANE), F32)
    for p in range(n_p):
        w = jnp.exp(m_sc[p] - mx)
        num += o_sc[p] * w
        den += l_sc[p] * w
    o_ref[0] = (num / den).astype(o_ref.dtype)


def _dilated_attention(att, bias):
    b, s, _ = att.shape
    assert s % ATT_TILE == 0
    nt = s // ATT_TILE
    nh = s // ATT_HALO
    per = ATT_TILE // ATT_HALO
    hp = ATT_WIDTH // LANE
    cur = lambda col0: pl.BlockSpec((1, ATT_TILE, LANE), lambda bi, ti, hi: (bi, ti, col0 + hi))
    prev = lambda col0: pl.BlockSpec(
        (1, ATT_HALO, LANE), lambda bi, ti, hi: (bi, jnp.maximum(ti * per - 1, 0), col0 + hi))
    nxt = lambda col0: pl.BlockSpec(
        (1, ATT_HALO, LANE), lambda bi, ti, hi: (bi, jnp.minimum((ti + 1) * per, nh - 1), col0 + hi))
    n_p = len(DILATED_PATTERNS)
    tile_f32 = pltpu.VMEM((ATT_TILE, LANE), F32)
    span_bf16 = pltpu.VMEM((ATT_TILE + 2 * ATT_HALO, LANE), BF16)
    per_pattern = pltpu.VMEM((n_p, ATT_TILE, LANE), F32)
    return pl.pallas_call(
        functools.partial(_attn_kernel, seq_len=s),
        grid=(b, nt, hp),
        in_specs=[cur(0), prev(hp), cur(hp), nxt(hp), prev(2 * hp), cur(2 * hp), nxt(2 * hp),
                  pl.BlockSpec((n_p, 2, QBLK, KBLK), lambda bi, ti, hi: (0, hi, 0, 0))],
        out_specs=pl.BlockSpec((1, ATT_TILE, LANE), lambda bi, ti, hi: (bi, ti, hi)),
        out_shape=jax.ShapeDtypeStruct((b, s, ATT_WIDTH), BF16),
        scratch_shapes=[tile_f32, span_bf16, span_bf16, tile_f32, tile_f32, tile_f32,
                        per_pattern, per_pattern, per_pattern],
        compiler_params=_params("parallel", "parallel", "parallel"),
        name="dilated_attention",
    )(att, att, att, att, att, att, att, bias)


def _mix(x, att_bias, norm1_g, w_in, conv_w, a_log_fwd, dt_bias_fwd, a_log_bwd, dt_bias_bwd,
         dn_norm_g, w_out, norm2_g, w_router):
    bsz, s, d = x.shape
    xf = x.reshape(bsz * s, d)
    qkv, z, ba, att = _in_proj(xf, norm1_g[None], w_in)
    shp = lambda a: a.reshape(bsz, s, a.shape[-1])
    o_f, o_b = _deltanet(shp(qkv), shp(ba), conv_w, a_log_fwd, dt_bias_fwd, a_log_bwd, dt_bias_bwd)
    att_out = _dilated_attention(shp(att), att_bias)
    flat = lambda a: a.reshape(bsz * s, a.shape[-1])
    return _out_proj(flat(o_f), flat(o_b), z, dn_norm_g[None], flat(att_out), xf, w_out, norm2_g[None],
                     w_router)


def kernel(x_prompt, x_sample, rel_bias, norm1_g, w_in, conv_w, a_log_fwd, dt_bias_fwd, a_log_bwd,
           dt_bias_bwd, dn_norm_g, w_out, norm2_g, w_router, w_gate, w_up, w_down, final_norm_g):
    groups = (x_prompt, x_sample)
    att_bias = _attn_bias_table(rel_bias)
    parts = []
    for x in groups:
        x1, h2, aff, afft = _mix(x, att_bias, norm1_g[0], w_in[0], conv_w[0], a_log_fwd[0], dt_bias_fwd[0],
                                 a_log_bwd[0], dt_bias_bwd[0], dn_norm_g[0], w_out[0], norm2_g[0],
                                 w_router[0])
        n_tok = x1.shape[0]
        cap = CAPACITY_FACTOR * n_tok // N_EXPERTS
        pos, idx, off = _route(afft, cap)
        xe = _sc_gather(h2, idx.reshape(-1))
        parts.append((x1, aff, pos, off, xe, cap))
    ye = _expert_ffn(parts[0][4], parts[1][4], w_gate[0], w_up[0], w_down[0])
    outs = []
    row0 = 0
    for x, (x1, aff, pos, off, _, cap) in zip(groups, parts):
        outs.append(_combine(x1, aff, pos, off, ye, row0, cap, final_norm_g[None]).reshape(x.shape))
        row0 += cap
    return tuple(outs)
```

```python
import functools
import math

import jax
import jax.numpy as jnp
from jax import lax
from jax.experimental import pallas as pl
from jax.experimental.pallas import tpu as pltpu
from jax.experimental.pallas import tpu_sc as plsc

D_MODEL = 1024
DN_HEADS = 4
DN_HEAD_DIM = 128
DN_WIDTH = DN_HEADS * DN_HEAD_DIM
ATT_HEADS = 8
ATT_HEAD_DIM = 64
ATT_WIDTH = ATT_HEADS * ATT_HEAD_DIM
CONV_K = 3
CHUNK = 64
DILATED_PATTERNS = ((128, 1), (512, 4), (2048, 16))
BAND = 64
NUM_BUCKETS = 32
MAX_DISTANCE = 1024
N_EXPERTS = 16
CAPACITY_FACTOR = 2
EXPERT_D_FF = 2816
EPS = 1e-6
NEG = -1e30
GATE_COLS = 4 * DN_HEADS
LANE = 128
SUBLANE = 8
GATHER_WINDOW = 128
GATHER_SPLIT = 4

F32 = jnp.float32
BF16 = jnp.bfloat16

VMEM_LIMIT = 56 * 1024 * 1024


def _params(*sem):
    return pltpu.CompilerParams(dimension_semantics=sem, vmem_limit_bytes=VMEM_LIMIT)


def _in_proj_kernel(x_ref, g_ref, wqkv_ref, wz_ref, wba_ref, watt_ref,
                    qkv_ref, z_ref, ba_ref, att_ref):
    x = x_ref[...]
    h = x * lax.rsqrt(jnp.mean(x * x, axis=-1, keepdims=True) + EPS) * g_ref[...]
    hb = h.astype(BF16)
    qkv_ref[...] = jnp.dot(hb, wqkv_ref[...], preferred_element_type=F32)
    z_ref[...] = jnp.dot(hb, wz_ref[...], preferred_element_type=F32)
    ba_ref[...] = jnp.dot(hb, wba_ref[...], preferred_element_type=F32)
    att_ref[...] = jnp.dot(hb, watt_ref[...], preferred_element_type=F32)


def _in_proj(x, g, w_in, tm=512):
    n = x.shape[0]
    o_qkv = 3 * DN_WIDTH
    o_z = o_qkv + DN_WIDTH
    o_ba = o_z + GATE_COLS
    wb = w_in.astype(BF16)
    wqkv = wb[:, :o_qkv]
    wz = wb[:, o_qkv:o_z]
    wba = jnp.pad(wb[:, o_z:o_ba], ((0, 0), (0, LANE - GATE_COLS)))
    watt = wb[:, o_ba:]
    full = lambda a: pl.BlockSpec(a.shape, lambda i: (0, 0))
    row = lambda w: pl.BlockSpec((tm, w), lambda i: (i, 0))
    return pl.pallas_call(
        _in_proj_kernel,
        grid=(n // tm,),
        in_specs=[row(D_MODEL), full(g), full(wqkv), full(wz), full(wba), full(watt)],
        out_specs=[row(o_qkv), row(DN_WIDTH), row(LANE), row(3 * ATT_WIDTH)],
        out_shape=[jax.ShapeDtypeStruct((n, o_qkv), F32),
                   jax.ShapeDtypeStruct((n, DN_WIDTH), F32),
                   jax.ShapeDtypeStruct((n, LANE), F32),
                   jax.ShapeDtypeStruct((n, 3 * ATT_WIDTH), F32)],
        compiler_params=_params("parallel"),
        name="in_proj",
    )(x, g, wqkv, wz, wba, watt)


def _split(a):
    hi = a.astype(BF16)
    return hi, (a - hi.astype(F32)).astype(BF16)


def _out_proj_kernel(of_ref, ob_ref, z_ref, gdn_ref, att_ref, x_ref, wdn_ref, watt_ref, g_ref, wr_ref,
                     x1_ref, h2_ref, aff_ref, afft_ref):
    dot = lambda a, b: jnp.dot(a, b, preferred_element_type=F32)
    o = of_ref[...].astype(F32) + ob_ref[...].astype(F32)
    z = z_ref[...]
    gated = []
    for h in range(DN_HEADS):
        hs = slice(h * DN_HEAD_DIM, (h + 1) * DN_HEAD_DIM)
        oh = o[:, hs]
        oh = oh * lax.rsqrt(jnp.mean(oh * oh, axis=-1, keepdims=True) + EPS) * gdn_ref[...]
        zh = z[:, hs]
        gated.append((oh * (zh * jax.nn.sigmoid(zh))).astype(BF16))
    dn = jnp.concatenate(gated, axis=1)
    x1 = x_ref[...] + dot(dn, wdn_ref[...]) + dot(att_ref[...], watt_ref[...])
    x1_ref[...] = x1
    h2 = x1 * lax.rsqrt(jnp.mean(x1 * x1, axis=-1, keepdims=True) + EPS) * g_ref[...]
    rows = h2.shape[0]
    piece = D_MODEL // GATHER_SPLIT
    for q in range(GATHER_SPLIT):
        h2_ref[q] = h2[:, q * piece:(q + 1) * piece]
    hh, hl = _split(h2)
    wh, wl = _split(wr_ref[...])
    logits = dot(hh, wh) + (dot(hh, wl) + dot(hl, wh))
    lane = lax.broadcasted_iota(jnp.int32, logits.shape, 1)
    logits = jnp.where(lane < N_EXPERTS, logits, NEG)
    p = jnp.exp(logits - jnp.max(logits, axis=-1, keepdims=True))
    aff = p / jnp.sum(p, axis=-1, keepdims=True)
    aff_ref[...] = aff
    for j in range(rows // LANE):
        afft_ref[:, j, :] = aff[j * LANE:(j + 1) * LANE, :].T[:N_EXPERTS, :]


def _out_proj(o_f, o_b, z, dn_norm_g, att, x, w_out, g, w_router, tm=1024):
    n = x.shape[0]
    wb = w_out.astype(BF16)
    wdn, watt = wb[:DN_WIDTH], wb[DN_WIDTH:]
    wr = jnp.pad(w_router.astype(F32), ((0, 0), (0, LANE - N_EXPERTS)))
    full = lambda a: pl.BlockSpec(a.shape, lambda i: (0, 0))
    row = lambda w: pl.BlockSpec((tm, w), lambda i: (i, 0))
    return pl.pallas_call(
        _out_proj_kernel,
        grid=(n // tm,),
        in_specs=[row(DN_WIDTH), row(DN_WIDTH), row(DN_WIDTH), full(dn_norm_g), row(ATT_WIDTH),
                  row(D_MODEL), full(wdn), full(watt), full(g), full(wr)],
        out_specs=[row(D_MODEL),
                   pl.BlockSpec((GATHER_SPLIT, tm, D_MODEL // GATHER_SPLIT), lambda i: (0, i, 0)),
                   row(LANE),
                   pl.BlockSpec((N_EXPERTS, tm // LANE, LANE), lambda i: (0, i, 0))],
        out_shape=[jax.ShapeDtypeStruct((n, D_MODEL), F32),
                   jax.ShapeDtypeStruct((GATHER_SPLIT, n, D_MODEL // GATHER_SPLIT), F32),
                   jax.ShapeDtypeStruct((n, LANE), F32),
                   jax.ShapeDtypeStruct((N_EXPERTS, n // LANE, LANE), F32)],
        compiler_params=_params("parallel"),
        name="out_proj",
    )(o_f, o_b, z, dn_norm_g, att, x, wdn, watt, g, wr)


def _route_kernel(aff_ref, pos_ref, idx_ref, off_ref, *, cap):
    x = aff_ref[0]
    nb = x.shape[0]
    bits = pltpu.bitcast(x, jnp.int32)
    total = lambda m: jnp.sum(jnp.sum(m, axis=0, keepdims=True), axis=1, keepdims=True)
    dot = lambda a, b: jnp.dot(a, b, preferred_element_type=F32)
    one_hot = lambda m: jnp.where(m, 1.0, 0.0)

    def bit_step(i, prefix):
        cand = prefix | lax.shift_left(jnp.int32(1), 30 - i)
        cnt = total(one_hot(bits >= cand))
        return jnp.where(cnt >= cap, cand, prefix)

    thr = lax.fori_loop(0, 31, bit_step, jnp.zeros((1, 1), jnp.int32))
    gt = bits > thr
    eq = bits == thr
    need = cap - total(one_hot(gt))

    li = lax.broadcasted_iota(jnp.int32, (LANE, LANE), 0)
    lj = lax.broadcasted_iota(jnp.int32, (LANE, LANE), 1)
    lane_incl = one_hot(li <= lj).astype(BF16)
    ones = jnp.ones((LANE, LANE), BF16)
    bi = lax.broadcasted_iota(jnp.int32, (nb, nb), 0)
    bj = lax.broadcasted_iota(jnp.int32, (nb, nb), 1)
    blk_before = one_hot(bj < bi).astype(BF16)

    def ranks(mask):
        m = one_hot(mask).astype(BF16)
        within = dot(m, lane_incl)
        tot = dot(m, ones)
        before = dot(blk_before, tot.astype(BF16))
        return within, tot, before

    w_eq, _, b_eq = ranks(eq)
    sel = gt | (eq & (w_eq + b_eq <= need))
    within, tot, before = ranks(sel)
    pos_ref[0] = jnp.where(sel, within + before - 1.0, -1.0)
    off_ref[0] = before.T[0:SUBLANE, :].astype(jnp.int32)

    within_t = within.T.astype(BF16)
    sel_t = one_hot(sel).T.astype(BF16)
    after = before + tot
    blk_id = lax.broadcasted_iota(jnp.int32, (nb, LANE), 0).astype(F32)
    lane_id = lax.broadcasted_iota(jnp.int32, (LANE, LANE), 0).astype(F32)
    for c in range(cap // LANE):
        j = (lax.broadcasted_iota(jnp.int32, (1, LANE), 1) + c * LANE).astype(F32)
        holds = one_hot((before <= j) & (j < after))
        local = j - jnp.sum(holds * before, axis=0, keepdims=True) + 1.0
        blk = jnp.sum(holds * blk_id, axis=0, keepdims=True)
        hb = holds.astype(BF16)
        cnt_in_blk = dot(within_t, hb)
        sel_in_blk = dot(sel_t, hb)
        hit = one_hot((sel_in_blk > 0.5) & (cnt_in_blk == local))
        lane_of = jnp.sum(hit * lane_id, axis=0, keepdims=True)
        idx_ref[0, c:c + 1, :] = (blk * LANE + lane_of).astype(jnp.int32)


def _route(aff3, cap):
    e, nb, _ = aff3.shape
    blk = lambda rows: pl.BlockSpec((1, rows, LANE), lambda i: (i, 0, 0))
    return pl.pallas_call(
        functools.partial(_route_kernel, cap=cap),
        grid=(e,),
        in_specs=[blk(nb)],
        out_specs=[blk(nb), blk(cap // LANE), pl.BlockSpec((1, SUBLANE, nb), lambda i: (i, 0, 0))],
        out_shape=[jax.ShapeDtypeStruct((e, nb, LANE), F32),
                   jax.ShapeDtypeStruct((e, cap // LANE, LANE), jnp.int32),
                   jax.ShapeDtypeStruct((e, SUBLANE, nb), jnp.int32)],
        compiler_params=_params("parallel"),
        name="route",
    )(aff3)


def _sc_gather(x, idx):
    pieces, n, d = x.shape
    x = x.reshape(pieces * n, d)
    idx = (jnp.arange(pieces, dtype=idx.dtype)[:, None] * n + idx[None, :]).reshape(-1)
    m = idx.shape[0]
    mesh = plsc.VectorSubcoreMesh(core_axis_name="c", subcore_axis_name="s")

    @pl.kernel(out_type=jax.ShapeDtypeStruct((m, d), x.dtype), mesh=mesh, scratch_types=[])
    def gather_kernel(x_hbm, i_hbm, o_hbm):
        def body(i_vmem, o_vmem):
            pltpu.sync_copy(x_hbm.at[i_vmem.at[0]], o_vmem)

        pltpu.emit_pipeline(
            body,
            grid=(m // GATHER_WINDOW,),
            in_specs=[pl.BlockSpec((1, GATHER_WINDOW), lambda i: (0, i))],
            out_specs=[pl.BlockSpec((GATHER_WINDOW, d), lambda i: (i, 0))],
            core_axis_name=("c", "s"),
            dimension_semantics=(pltpu.PARALLEL,),
        )(i_hbm, o_hbm)

    return gather_kernel(x, idx.reshape(1, m)).reshape(pieces, m // pieces, d)


def _ffn_kernel(x_ref, wg_ref, wu_ref, wd_ref, ye_ref, xe_ref, acc_ref):
    f = pl.program_id(2)
    pieces, _, piece = x_ref.shape

    @pl.when(f == 0)
    def _():
        acc_ref[...] = jnp.zeros_like(acc_ref)
        for q in range(pieces):
            xe_ref[:, q * piece:(q + 1) * piece] = x_ref[q].astype(BF16)

    xe = xe_ref[...]
    a = jnp.dot(xe, wg_ref[0].astype(BF16), preferred_element_type=F32)
    b = jnp.dot(xe, wu_ref[0].astype(BF16), preferred_element_type=F32)
    h = (a * jax.nn.sigmoid(a) * b).astype(BF16)
    acc_ref[...] += jnp.dot(h, wd_ref[0].astype(BF16), preferred_element_type=F32)

    @pl.when(f == pl.num_programs(2) - 1)
    def _():
        ye_ref[0] = acc_ref[...].astype(ye_ref.dtype)


def _expert_ffn(xe, w_gate, w_up, w_down, tm=2048, tf=256):
    e = w_gate.shape[0]
    pieces, rows, piece = xe.shape
    cap, d = rows // e, piece * pieces
    dff = w_gate.shape[-1]
    tiles = cap // tm
    return pl.pallas_call(
        _ffn_kernel,
        grid=(e, tiles, dff // tf),
        in_specs=[pl.BlockSpec((pieces, tm, piece), lambda i, r, f: (0, i * tiles + r, 0)),
                  pl.BlockSpec((1, d, tf), lambda i, r, f: (i, 0, f)),
                  pl.BlockSpec((1, d, tf), lambda i, r, f: (i, 0, f)),
                  pl.BlockSpec((1, tf, d), lambda i, r, f: (i, f, 0))],
        out_specs=pl.BlockSpec((1, tm, d), lambda i, r, f: (i, r, 0)),
        out_shape=jax.ShapeDtypeStruct((e, cap, d), BF16),
        scratch_shapes=[pltpu.VMEM((tm, d), BF16), pltpu.VMEM((tm, d), F32)],
        compiler_params=_params("parallel", "parallel", "arbitrary"),
        name="expert_ffn",
    )(xe, w_gate, w_up, w_down)


COMBINE_ROWS = LANE + 16


def _combine_kernel(r0_ref, x1_ref, aff_ref, pos_ref, g_ref, *rest):
    ye_refs, y_ref = rest[:N_EXPERTS], rest[N_EXPERTS]
    i = pl.program_id(0)
    nb = pl.num_programs(0)
    t = x1_ref.shape[0]
    pos_t = jnp.concatenate([pos_ref[:, 0, :], jnp.zeros((LANE - N_EXPERTS, LANE), F32)], axis=0).T
    aff = aff_ref[...]
    col = lax.broadcasted_iota(jnp.int32, (t, COMBINE_ROWS), 1)
    acc = x1_ref[...]
    for e in range(N_EXPERTS):
        r0 = r0_ref[e * nb + i]
        slot = jnp.broadcast_to(pos_t[:, e:e + 1], (t, COMBINE_ROWS))
        pick = jnp.where(slot == (col + r0).astype(F32), 1.0, 0.0).astype(BF16)
        contrib = jnp.dot(pick, ye_refs[e][...], preferred_element_type=F32)
        acc = acc + jnp.broadcast_to(aff[:, e:e + 1], contrib.shape) * contrib
    y_ref[...] = acc * lax.rsqrt(jnp.mean(acc * acc, axis=-1, keepdims=True) + EPS) * g_ref[...]


def _combine(x1, aff, pos3, off, ye, final_g):
    n, d = x1.shape
    nb = n // LANE
    cap = ye.shape[1]
    r0 = jnp.minimum(off[:, 0, :] // 16 * 16, cap - COMBINE_ROWS)
    pos4 = pos3.reshape(N_EXPERTS, nb, 1, LANE)
    ye_spec = lambda e: pl.BlockSpec((None, pl.Element(COMBINE_ROWS), pl.Element(d)),
                                     lambda i, r0_ref: (e, pl.multiple_of(r0_ref[e * nb + i], 16), 0))
    grid_spec = pltpu.PrefetchScalarGridSpec(
        num_scalar_prefetch=1,
        grid=(nb,),
        in_specs=[pl.BlockSpec((LANE, d), lambda i, r: (i, 0)),
                  pl.BlockSpec((LANE, LANE), lambda i, r: (i, 0)),
                  pl.BlockSpec((N_EXPERTS, None, 1, LANE), lambda i, r: (0, i, 0, 0)),
                  pl.BlockSpec((1, d), lambda i, r: (0, 0))] + [ye_spec(e) for e in range(N_EXPERTS)],
        out_specs=pl.BlockSpec((LANE, d), lambda i, r: (i, 0)),
    )
    return pl.pallas_call(
        _combine_kernel,
        grid_spec=grid_spec,
        out_shape=jax.ShapeDtypeStruct((n, d), F32),
        compiler_params=_params("parallel"),
        name="moe_combine",
    )(r0.reshape(-1).astype(jnp.int32), x1, aff, pos4, final_g, *([ye] * N_EXPERTS))


def _dn_prep_kernel(x_ref, xp_ref, xn_ref, w_ref, q_ref, k_ref, v_ref, kt_ref):
    t = pl.program_id(1)
    nt = pl.num_programs(1)
    x = x_ref[0]
    rows = x.shape[0]
    row = lax.broadcasted_iota(jnp.int32, (rows, 1), 0)
    before = jnp.where(t > 0, xp_ref[0, SUBLANE - 1:SUBLANE, :], 0.0)
    after = jnp.where(t < nt - 1, xn_ref[0, 0:1, :], 0.0)
    x_prev = jnp.where(row == 0, before, pltpu.roll(x, 1, axis=0))
    x_next = jnp.where(row == rows - 1, after, pltpu.roll(x, rows - 1, axis=0))
    y = w_ref[0:1, :] * x_prev + w_ref[1:2, :] * x + w_ref[2:3, :] * x_next
    y = y * jax.nn.sigmoid(y)
    for h in range(DN_HEADS):
        for part, ref, scale in ((0, q_ref, DN_HEAD_DIM ** -0.5), (1, k_ref, 1.0)):
            c0 = part * DN_WIDTH + h * DN_HEAD_DIM
            a = y[:, c0:c0 + DN_HEAD_DIM]
            inv = lax.rsqrt(jnp.sum(a * a, axis=-1, keepdims=True) + EPS) * scale
            a = a * inv
            ref[0, :, h * DN_HEAD_DIM:(h + 1) * DN_HEAD_DIM] = a.astype(ref.dtype)
            if part == 1:
                for j in range(rows // CHUNK):
                    kt_ref[0, j, h * DN_HEAD_DIM:(h + 1) * DN_HEAD_DIM, :] = (
                        a[j * CHUNK:(j + 1) * CHUNK, :].T.astype(kt_ref.dtype))
    v_ref[0] = y[:, 2 * DN_WIDTH:].astype(v_ref.dtype)


def _dn_prep(qkv, conv_w, tile=512):
    b, s, c = qkv.shape
    tile = min(tile, s)
    nt = s // tile
    per = tile // SUBLANE
    nsub = s // SUBLANE
    out = jax.ShapeDtypeStruct((b, s, DN_WIDTH), BF16)
    ospec = pl.BlockSpec((1, tile, DN_WIDTH), lambda bi, ti: (bi, ti, 0))
    return pl.pallas_call(
        _dn_prep_kernel,
        grid=(b, nt),
        in_specs=[pl.BlockSpec((1, tile, c), lambda bi, ti: (bi, ti, 0)),
                  pl.BlockSpec((1, SUBLANE, c), lambda bi, ti: (bi, jnp.maximum(ti * per - 1, 0), 0)),
                  pl.BlockSpec((1, SUBLANE, c), lambda bi, ti: (bi, jnp.minimum((ti + 1) * per, nsub - 1), 0)),
                  pl.BlockSpec((CONV_K, c), lambda bi, ti: (0, 0))],
        out_specs=[ospec, ospec, ospec,
                   pl.BlockSpec((1, tile // CHUNK, DN_WIDTH, CHUNK), lambda bi, ti: (bi, ti, 0, 0))],
        out_shape=[out, out, out, jax.ShapeDtypeStruct((b, s // CHUNK, DN_WIDTH, CHUNK), BF16)],
        compiler_params=_params("parallel", "parallel"),
        name="dn_prep",
    )(qkv, qkv, qkv, conv_w)


def _lane_bcast(a, col, width=LANE):
    return jnp.broadcast_to(a[:, col:col + 1], (a.shape[0], width))


DN_TILE = 1024
INV_STEPS = 5
assert 2 ** (INV_STEPS + 1) == CHUNK
DN_UNROLL = 16


def _dn_scan_kernel(q_ref, k_ref, v_ref, kt_ref, ba_ref, alog_ref, dtb_ref, o_ref,
                    s_ref, xbuf, tbuf, abuf, rbuf, uwbuf, qgbuf, kdbuf, lbuf, nbuf, obuf, ecbuf, *, reverse):
    @pl.when(pl.program_id(1) == 0)
    def _():
        s_ref[...] = jnp.zeros_like(s_ref)

    tile = q_ref.shape[1]
    nc = tile // CHUNK
    n_prob = nc * DN_HEADS
    unroll = min(DN_UNROLL, n_prob)
    beta_col = DN_HEADS if reverse else 0
    a_col = 2 * DN_HEADS + (DN_HEADS if reverse else 0)
    ri = lax.broadcasted_iota(jnp.int32, (CHUNK, CHUNK), 0)
    ci = lax.broadcasted_iota(jnp.int32, (CHUNK, CHUNK), 1)
    tri = (ri <= ci) if reverse else (ri >= ci)
    strict = (ri < ci) if reverse else (ri > ci)
    eye = jnp.where(ri == ci, 1.0, 0.0)
    row = lax.broadcasted_iota(jnp.int32, (CHUNK, LANE), 0)
    a_scale = jnp.exp(alog_ref[...])
    dtb = dtb_ref[...]
    dot = lambda a, b: jnp.dot(a, b, preferred_element_type=F32)

    def build(c, carry):
        rows = pl.ds(pl.multiple_of(c * CHUNK, CHUNK), CHUNK)
        raw = ba_ref[0, rows, :]
        beta_all = jax.nn.sigmoid(raw)
        xs = raw + dtb
        g = -a_scale * (jnp.maximum(xs, 0.0) + jnp.log(1.0 + jnp.exp(-jnp.abs(xs))))
        gs = g
        sh = 1
        while sh < CHUNK:
            if reverse:
                gs = gs + jnp.where(row < CHUNK - sh, pltpu.roll(gs, CHUNK - sh, axis=0), 0.0)
            else:
                gs = gs + jnp.where(row >= sh, pltpu.roll(gs, sh, axis=0), 0.0)
            sh *= 2
        gs_t = gs.T
        g_last = gs[0:1, :] if reverse else gs[CHUNK - 1:CHUNK, :]
        eg_all = jnp.exp(gs)
        ek_t = jnp.exp(g_last.T - gs_t)
        ecbuf[c] = jnp.broadcast_to(jnp.exp(g_last), (SUBLANE, LANE))
        for h in range(DN_HEADS):
            p = c * DN_HEADS + h
            hs = slice(h * DN_HEAD_DIM, (h + 1) * DN_HEAD_DIM)
            q = q_ref[0, rows, hs].astype(F32)
            k = k_ref[0, rows, hs]
            v = v_ref[0, rows, hs].astype(F32)
            beta = _lane_bcast(beta_all, beta_col + h)
            eg = _lane_bcast(eg_all, a_col + h)
            diff = _lane_bcast(gs, a_col + h, CHUNK) - gs_t[a_col + h:a_col + h + 1, :]
            decay = jnp.where(tri, jnp.exp(jnp.where(tri, diff, 0.0)), 0.0)
            kb = k.astype(F32) * beta
            kq = lax.dot_general(jnp.concatenate([kb, q], axis=0).astype(BF16), k,
                                 (((1,), (1,)), ((), ())), preferred_element_type=F32)
            lower = jnp.where(strict, kq[:CHUNK] * decay, 0.0)
            xbuf[p] = (-lower).astype(BF16)
            tbuf[p] = eye - lower
            abuf[p] = (kq[CHUNK:] * decay).astype(BF16)
            rbuf[p] = jnp.concatenate([v * beta, kb * eg], axis=1).astype(BF16)
            qgbuf[p] = q * eg
            kdbuf[p] = (kt_ref[0, c, hs, :].astype(F32)
                        * ek_t[a_col + h:a_col + h + 1, :]).astype(BF16)
        return carry

    lax.fori_loop(0, nc, build, 0)

    for step in range(INV_STEPS + 1):
        def double(p, carry, first=(step == 0), last=(step == INV_STEPS)):
            x = xbuf[p]
            if not first:
                t = tbuf[p]
                tbuf[p] = t + dot(t.astype(BF16), x)
            if not last:
                xbuf[p] = dot(x, x).astype(BF16)
            return carry
        lax.fori_loop(0, n_prob, double, 0, unroll=unroll)

    def solve(p, carry):
        uwbuf[p] = dot(tbuf[p].astype(BF16), rbuf[p]).astype(BF16)
        return carry

    lax.fori_loop(0, n_prob, solve, 0, unroll=unroll)

    def fold(p, carry):
        uw = uwbuf[p]
        a_uw = dot(abuf[p], uw)
        k_uw = dot(kdbuf[p], uw)
        obuf[p] = a_uw[:, :DN_HEAD_DIM]
        nbuf[p] = k_uw[:, :DN_HEAD_DIM]
        lbuf[p, :DN_HEAD_DIM, :] = k_uw[:, DN_HEAD_DIM:].astype(BF16)
        lbuf[p, DN_HEAD_DIM:, :] = (qgbuf[p] - a_uw[:, DN_HEAD_DIM:]).astype(BF16)
        return carry

    lax.fori_loop(0, n_prob, fold, 0, unroll=unroll)

    def scan(jc, carry):
        c = (nc - 1 - jc) if reverse else jc
        rows = pl.ds(pl.multiple_of(c * CHUNK, CHUNK), CHUNK)
        ec_all = ecbuf[c]
        for h in range(DN_HEADS):
            p = c * DN_HEADS + h
            state = s_ref[h]
            r = dot(lbuf[p], state.astype(BF16))
            s_ref[h] = state * _lane_bcast(ec_all[0:1], a_col + h) - r[:DN_HEAD_DIM] + nbuf[p]
            o = r[DN_HEAD_DIM:] + obuf[p]
            o_ref[0, rows, h * DN_HEAD_DIM:(h + 1) * DN_HEAD_DIM] = o.astype(o_ref.dtype)
        return carry

    lax.fori_loop(0, nc, scan, 0)


def _dn_scan(q, k, v, kt, ba, a_log_row, dt_bias_row, reverse, tile=DN_TILE):
    b, s, _ = q.shape
    tile = min(tile, s)
    nt = s // tile
    nc = tile // CHUNK
    n_prob = nc * DN_HEADS
    tmap = (lambda bi, ti: (bi, nt - 1 - ti, 0)) if reverse else (lambda bi, ti: (bi, ti, 0))
    wide = pl.BlockSpec((1, tile, DN_WIDTH), tmap)
    par = pl.BlockSpec((1, LANE), lambda bi, ti: (0, 0))
    return pl.pallas_call(
        functools.partial(_dn_scan_kernel, reverse=reverse),
        grid=(b, nt),
        in_specs=[wide, wide, wide,
                  pl.BlockSpec((1, nc, DN_WIDTH, CHUNK), lambda bi, ti: tmap(bi, ti) + (0,)),
                  pl.BlockSpec((1, tile, LANE), tmap), par, par],
        out_specs=wide,
        out_shape=jax.ShapeDtypeStruct((b, s, DN_WIDTH), BF16),
        scratch_shapes=[pltpu.VMEM((DN_HEADS, DN_HEAD_DIM, DN_HEAD_DIM), F32),
                        pltpu.VMEM((n_prob, CHUNK, CHUNK), BF16),
                        pltpu.VMEM((n_prob, CHUNK, CHUNK), F32),
                        pltpu.VMEM((n_prob, CHUNK, CHUNK), BF16),
                        pltpu.VMEM((n_prob, CHUNK, 2 * DN_HEAD_DIM), BF16),
                        pltpu.VMEM((n_prob, CHUNK, 2 * DN_HEAD_DIM), BF16),
                        pltpu.VMEM((n_prob, CHUNK, DN_HEAD_DIM), F32),
                        pltpu.VMEM((n_prob, DN_HEAD_DIM, CHUNK), BF16),
                        pltpu.VMEM((n_prob, DN_HEAD_DIM + CHUNK, DN_HEAD_DIM), BF16),
                        pltpu.VMEM((n_prob, DN_HEAD_DIM, DN_HEAD_DIM), F32),
                        pltpu.VMEM((n_prob, CHUNK, DN_HEAD_DIM), F32),
                        pltpu.VMEM((nc, SUBLANE, LANE), F32)],
        compiler_params=_params("parallel", "arbitrary"),
        name="dn_scan_bwd" if reverse else "dn_scan_fwd",
    )(q, k, v, kt, ba, a_log_row, dt_bias_row)


def _gate_rows(a_log_fwd, a_log_bwd, dt_bias_fwd, dt_bias_bwd):
    pad = lambda f, b: jnp.pad(jnp.concatenate([f, b]).astype(F32), (2 * DN_HEADS, LANE - 4 * DN_HEADS))[None]
    return pad(a_log_fwd, a_log_bwd), pad(dt_bias_fwd, dt_bias_bwd)


def _deltanet(qkv, ba, conv_w, a_log_fwd, dt_bias_fwd, a_log_bwd, dt_bias_bwd):
    q, k, v, kt = _dn_prep(qkv, conv_w)
    alog, dtb = _gate_rows(a_log_fwd, a_log_bwd, dt_bias_fwd, dt_bias_bwd)
    o_f = _dn_scan(q, k, v, kt, ba, alog, dtb, reverse=False)
    o_b = _dn_scan(q, k, v, kt, ba, alog, dtb, reverse=True)
    return o_f, o_b


def _t5_bucket(rel):
    nb = NUM_BUCKETS // 2
    ret = jnp.where(rel > 0, nb, 0)
    n = jnp.abs(rel)
    max_exact = nb // 2
    nf = jnp.maximum(n, 1).astype(F32)
    large = max_exact + (jnp.log(nf / max_exact) / math.log(MAX_DISTANCE / max_exact)
                         * (nb - max_exact)).astype(jnp.int32)
    large = jnp.minimum(large, nb - 1)
    return ret + jnp.where(n < max_exact, n, large)


ATT_TILE = 2048
ATT_HALO = BAND * max(d for _, d in DILATED_PATTERNS)
QBLK = 2 * BAND
KBLK = QBLK + 2 * BAND


def _attn_bias_table(rel_bias):
    delta = (jnp.arange(KBLK) - BAND)[None, :] - jnp.arange(QBLK)[:, None]
    tabs = []
    for window, dil in DILATED_PATTERNS:
        half = window // (2 * dil)
        onehot = (_t5_bucket(delta * dil)[..., None] == jnp.arange(NUM_BUCKETS)).astype(F32)
        bias = jnp.einsum('qkb,bh->hqk', onehot, rel_bias.astype(F32),
                          precision=lax.Precision.HIGHEST)
        tabs.append(jnp.where((jnp.abs(delta) <= half)[None], bias, NEG))
    return jnp.stack(tabs)


def _attn_kernel(q_ref, kp_ref, kc_ref, kn_ref, vp_ref, vc_ref, vn_ref, bias_ref, o_ref,
                 qd, kd, vd, od, md, ld, o_sc, m_sc, l_sc, *, seq_len):
    t = pl.program_id(1)
    lane = lax.broadcasted_iota(jnp.int32, (QBLK, LANE), 1)
    low = lane < ATT_HEAD_DIM
    scale = ATT_HEAD_DIM ** -0.5
    kcol = lax.broadcasted_iota(jnp.int32, (1, KBLK), 1)
    ones = jnp.ones((KBLK, LANE), BF16)

    for p, (window, dil) in enumerate(DILATED_PATTERNS):
        lt = ATT_TILE // dil
        lh = ATT_HALO // dil
        span = lt + 2 * lh
        nblk = lt // QBLK
        n_pos = seq_len // dil
        for r in range(dil):
            sl = lambda n: pl.ds(r, n, stride=dil) if dil > 1 else pl.ds(0, n)
            qd[pl.ds(r * lt, lt), :] = q_ref[0, sl(lt), :] * scale
            off = r * span
            for (kr, vr, n) in ((kp_ref, vp_ref, lh), (kc_ref, vc_ref, lt), (kn_ref, vn_ref, lh)):
                kd[pl.ds(off, n), :] = kr[0, sl(n), :].astype(BF16)
                vd[pl.ds(off, n), :] = vr[0, sl(n), :].astype(BF16)
                off += n

        for i in range(dil * nblk):
            r, j = divmod(i, nblk)
            q0 = r * lt + j * QBLK
            k0 = r * span + lh + j * QBLK - BAND
            q = qd[pl.ds(q0, QBLK), :]
            k = kd[pl.ds(k0, KBLK), :]
            v1 = jnp.concatenate([vd[pl.ds(k0, KBLK), :], ones], axis=1)
            edge = j == 0 or j == nblk - 1
            if edge:
                pos = t * lt + j * QBLK - BAND + kcol
                valid = (pos >= 0) & (pos < n_pos)
            res = []
            for h in range(2):
                qh = jnp.where(low, q, 0.0) if h == 0 else jnp.where(low, 0.0, q)
                s = lax.dot_general(qh.astype(BF16), k, (((1,), (1,)), ((), ())),
                                    preferred_element_type=F32) + bias_ref[p, h]
                if edge:
                    s = jnp.where(valid, s, NEG)
                m = jnp.max(s, axis=-1, keepdims=True)
                e = jnp.exp(s - m).astype(BF16)
                res.append((jnp.dot(e, v1, preferred_element_type=F32), m))
            (pv0, m0), (pv1, m1) = res
            od[pl.ds(q0, QBLK), :] = jnp.where(low, pv0[:, :LANE], pv1[:, :LANE])
            ld[pl.ds(q0, QBLK), :] = jnp.where(low, pv0[:, LANE:], pv1[:, LANE:])
            md[pl.ds(q0, QBLK), :] = jnp.where(low, m0, m1)

        for r in range(dil):
            sl = pl.ds(r, lt, stride=dil) if dil > 1 else pl.ds(0, lt)
            src = pl.ds(r * lt, lt)
            o_sc[p, sl, :] = od[src, :]
            m_sc[p, sl, :] = md[src, :]
            l_sc[p, sl, :] = ld[src, :]

    n_p = len(DILATED_PATTERNS)
    mx = m_sc[0]
    for p in range(1, n_p):
        mx = jnp.maximum(mx, m_sc[p])
    num = jnp.zeros((ATT_TILE, LANE), F32)
    den = jnp.zeros((ATT_TILE, LANE), F32)
    for p in range(n_p):
        w = jnp.exp(m_sc[p] - mx)
        num += o_sc[p] * w
        den += l_sc[p] * w
    o_ref[0] = (num / den).astype(o_ref.dtype)


def _dilated_attention(att, bias):
    b, s, _ = att.shape
    assert s % ATT_TILE == 0
    nt = s // ATT_TILE
    nh = s // ATT_HALO
    per = ATT_TILE // ATT_HALO
    hp = ATT_WIDTH // LANE
    cur = lambda col0: pl.BlockSpec((1, ATT_TILE, LANE), lambda bi, ti, hi: (bi, ti, col0 + hi))
    prev = lambda col0: pl.BlockSpec(
        (1, ATT_HALO, LANE), lambda bi, ti, hi: (bi, jnp.maximum(ti * per - 1, 0), col0 + hi))
    nxt = lambda col0: pl.BlockSpec(
        (1, ATT_HALO, LANE), lambda bi, ti, hi: (bi, jnp.minimum((ti + 1) * per, nh - 1), col0 + hi))
    n_p = len(DILATED_PATTERNS)
    tile_f32 = pltpu.VMEM((ATT_TILE, LANE), F32)
    span_bf16 = pltpu.VMEM((ATT_TILE + 2 * ATT_HALO, LANE), BF16)
    per_pattern = pltpu.VMEM((n_p, ATT_TILE, LANE), F32)
    return pl.pallas_call(
        functools.partial(_attn_kernel, seq_len=s),
        grid=(b, nt, hp),
        in_specs=[cur(0), prev(hp), cur(hp), nxt(hp), prev(2 * hp), cur(2 * hp), nxt(2 * hp),
                  pl.BlockSpec((n_p, 2, QBLK, KBLK), lambda bi, ti, hi: (0, hi, 0, 0))],
        out_specs=pl.BlockSpec((1, ATT_TILE, LANE), lambda bi, ti, hi: (bi, ti, hi)),
        out_shape=jax.ShapeDtypeStruct((b, s, ATT_WIDTH), BF16),
        scratch_shapes=[tile_f32, span_bf16, span_bf16, tile_f32, tile_f32, tile_f32,
                        per_pattern, per_pattern, per_pattern],
        compiler_params=_params("parallel", "parallel", "parallel"),
        name="dilated_attention",
    )(att, att, att, att, att, att, att, bias)


def _mix(x, att_bias, norm1_g, w_in, conv_w, a_log_fwd, dt_bias_fwd, a_log_bwd, dt_bias_bwd,
         dn_norm_g, w_out, norm2_g, w_router):
    bsz, s, d = x.shape
    xf = x.reshape(bsz * s, d)
    qkv, z, ba, att = _in_proj(xf, norm1_g[None], w_in)
    shp = lambda a: a.reshape(bsz, s, a.shape[-1])
    o_f, o_b = _deltanet(shp(qkv), shp(ba), conv_w, a_log_fwd, dt_bias_fwd, a_log_bwd, dt_bias_bwd)
    att_out = _dilated_attention(shp(att), att_bias)
    flat = lambda a: a.reshape(bsz * s, a.shape[-1])
    return _out_proj(flat(o_f), flat(o_b), z, dn_norm_g[None], flat(att_out), xf, w_out, norm2_g[None],
                     w_router)


def kernel(x_prompt, x_sample, rel_bias, norm1_g, w_in, conv_w, a_log_fwd, dt_bias_fwd, a_log_bwd,
           dt_bias_bwd, dn_norm_g, w_out, norm2_g, w_router, w_gate, w_up, w_down, final_norm_g):
    groups = (x_prompt, x_sample)
    att_bias = _attn_bias_table(rel_bias)
    parts = []
    for x in groups:
        x1, h2, aff, afft = _mix(x, att_bias, norm1_g[0], w_in[0], conv_w[0], a_log_fwd[0], dt_bias_fwd[0],
                                 a_log_bwd[0], dt_bias_bwd[0], dn_norm_g[0], w_out[0], norm2_g[0],
                                 w_router[0])
        n_tok = x1.shape[0]
        cap = CAPACITY_FACTOR * n_tok // N_EXPERTS
        pos, idx, off = _route(afft, cap)
        xe = _sc_gather(h2, idx.reshape(-1))
        parts.append((x, x1, aff, pos, off, xe))
    outs = []
    for x, x1, aff, pos, off, xe in parts:
        ye = _expert_ffn(xe, w_gate[0], w_up[0], w_down[0])
        outs.append(_combine(x1, aff, pos, off, ye, final_norm_g[None]).reshape(x.shape))
    return tuple(outs)
```

```python
import functools
import math

import jax
import jax.numpy as jnp
from jax import lax
from jax.experimental import pallas as pl
from jax.experimental.pallas import tpu as pltpu
from jax.experimental.pallas import tpu_sc as plsc

D_MODEL = 1024
DN_HEADS = 4
DN_HEAD_DIM = 128
DN_WIDTH = DN_HEADS * DN_HEAD_DIM
ATT_HEADS = 8
ATT_HEAD_DIM = 64
ATT_WIDTH = ATT_HEADS * ATT_HEAD_DIM
CONV_K = 3
CHUNK = 64
DILATED_PATTERNS = ((128, 1), (512, 4), (2048, 16))
BAND = 64
NUM_BUCKETS = 32
MAX_DISTANCE = 1024
N_EXPERTS = 16
CAPACITY_FACTOR = 2
EXPERT_D_FF = 2816
EPS = 1e-6
NEG = -1e30
GATE_COLS = 4 * DN_HEADS
LANE = 128
SUBLANE = 8
GATHER_WINDOW = 128
GATHER_SPLIT = 4

F32 = jnp.float32
BF16 = jnp.bfloat16

VMEM_LIMIT = 56 * 1024 * 1024


def _params(*sem):
    return pltpu.CompilerParams(dimension_semantics=sem, vmem_limit_bytes=VMEM_LIMIT)


def _in_proj_kernel(x_ref, g_ref, wqkv_ref, wz_ref, wba_ref, watt_ref,
                    qkv_ref, z_ref, ba_ref, att_ref):
    x = x_ref[...]
    h = x * lax.rsqrt(jnp.mean(x * x, axis=-1, keepdims=True) + EPS) * g_ref[...]
    hb = h.astype(BF16)
    qkv_ref[...] = jnp.dot(hb, wqkv_ref[...], preferred_element_type=F32)
    z_ref[...] = jnp.dot(hb, wz_ref[...], preferred_element_type=F32)
    ba_ref[...] = jnp.dot(hb, wba_ref[...], preferred_element_type=F32)
    att_ref[...] = jnp.dot(hb, watt_ref[...], preferred_element_type=F32)


def _in_proj(x, g, w_in, tm=512):
    n = x.shape[0]
    o_qkv = 3 * DN_WIDTH
    o_z = o_qkv + DN_WIDTH
    o_ba = o_z + GATE_COLS
    wb = w_in.astype(BF16)
    wqkv = wb[:, :o_qkv]
    wz = wb[:, o_qkv:o_z]
    wba = jnp.pad(wb[:, o_z:o_ba], ((0, 0), (0, LANE - GATE_COLS)))
    watt = wb[:, o_ba:]
    full = lambda a: pl.BlockSpec(a.shape, lambda i: (0, 0))
    row = lambda w: pl.BlockSpec((tm, w), lambda i: (i, 0))
    return pl.pallas_call(
        _in_proj_kernel,
        grid=(n // tm,),
        in_specs=[row(D_MODEL), full(g), full(wqkv), full(wz), full(wba), full(watt)],
        out_specs=[row(o_qkv), row(DN_WIDTH), row(LANE), row(3 * ATT_WIDTH)],
        out_shape=[jax.ShapeDtypeStruct((n, o_qkv), F32),
                   jax.ShapeDtypeStruct((n, DN_WIDTH), F32),
                   jax.ShapeDtypeStruct((n, LANE), F32),
                   jax.ShapeDtypeStruct((n, 3 * ATT_WIDTH), F32)],
        compiler_params=_params("parallel"),
        name="in_proj",
    )(x, g, wqkv, wz, wba, watt)


def _split(a):
    hi = a.astype(BF16)
    return hi, (a - hi.astype(F32)).astype(BF16)


def _out_proj_kernel(of_ref, ob_ref, z_ref, gdn_ref, att_ref, x_ref, wdn_ref, watt_ref, g_ref, wr_ref,
                     x1_ref, h2_ref, aff_ref, afft_ref):
    dot = lambda a, b: jnp.dot(a, b, preferred_element_type=F32)
    o = of_ref[...].astype(F32) + ob_ref[...].astype(F32)
    z = z_ref[...]
    gated = []
    for h in range(DN_HEADS):
        hs = slice(h * DN_HEAD_DIM, (h + 1) * DN_HEAD_DIM)
        oh = o[:, hs]
        oh = oh * lax.rsqrt(jnp.mean(oh * oh, axis=-1, keepdims=True) + EPS) * gdn_ref[...]
        zh = z[:, hs]
        gated.append((oh * (zh * jax.nn.sigmoid(zh))).astype(BF16))
    dn = jnp.concatenate(gated, axis=1)
    x1 = x_ref[...] + dot(dn, wdn_ref[...]) + dot(att_ref[...], watt_ref[...])
    x1_ref[...] = x1
    h2 = x1 * lax.rsqrt(jnp.mean(x1 * x1, axis=-1, keepdims=True) + EPS) * g_ref[...]
    rows = h2.shape[0]
    piece = D_MODEL // GATHER_SPLIT
    for q in range(GATHER_SPLIT):
        h2_ref[q] = h2[:, q * piece:(q + 1) * piece]
    hh, hl = _split(h2)
    wh, wl = _split(wr_ref[...])
    logits = dot(hh, wh) + (dot(hh, wl) + dot(hl, wh))
    lane = lax.broadcasted_iota(jnp.int32, logits.shape, 1)
    logits = jnp.where(lane < N_EXPERTS, logits, NEG)
    p = jnp.exp(logits - jnp.max(logits, axis=-1, keepdims=True))
    aff = p / jnp.sum(p, axis=-1, keepdims=True)
    aff_ref[...] = aff
    for j in range(rows // LANE):
        afft_ref[:, j, :] = aff[j * LANE:(j + 1) * LANE, :].T[:N_EXPERTS, :]


def _out_proj(o_f, o_b, z, dn_norm_g, att, x, w_out, g, w_router, tm=1024):
    n = x.shape[0]
    wb = w_out.astype(BF16)
    wdn, watt = wb[:DN_WIDTH], wb[DN_WIDTH:]
    wr = jnp.pad(w_router.astype(F32), ((0, 0), (0, LANE - N_EXPERTS)))
    full = lambda a: pl.BlockSpec(a.shape, lambda i: (0, 0))
    row = lambda w: pl.BlockSpec((tm, w), lambda i: (i, 0))
    return pl.pallas_call(
        _out_proj_kernel,
        grid=(n // tm,),
        in_specs=[row(DN_WIDTH), row(DN_WIDTH), row(DN_WIDTH), full(dn_norm_g), row(ATT_WIDTH),
                  row(D_MODEL), full(wdn), full(watt), full(g), full(wr)],
        out_specs=[row(D_MODEL),
                   pl.BlockSpec((GATHER_SPLIT, tm, D_MODEL // GATHER_SPLIT), lambda i: (0, i, 0)),
                   row(LANE),
                   pl.BlockSpec((N_EXPERTS, tm // LANE, LANE), lambda i: (0, i, 0))],
        out_shape=[jax.ShapeDtypeStruct((n, D_MODEL), F32),
                   jax.ShapeDtypeStruct((GATHER_SPLIT, n, D_MODEL // GATHER_SPLIT), F32),
                   jax.ShapeDtypeStruct((n, LANE), F32),
                   jax.ShapeDtypeStruct((N_EXPERTS, n // LANE, LANE), F32)],
        compiler_params=_params("parallel"),
        name="out_proj",
    )(o_f, o_b, z, dn_norm_g, att, x, wdn, watt, g, wr)


def _route_kernel(aff_ref, pos_ref, idx_ref, off_ref, *, cap):
    x = aff_ref[0]
    nb = x.shape[0]
    bits = pltpu.bitcast(x, jnp.int32)
    total = lambda m: jnp.sum(jnp.sum(m, axis=0, keepdims=True), axis=1, keepdims=True)
    dot = lambda a, b: jnp.dot(a, b, preferred_element_type=F32)
    one_hot = lambda m: jnp.where(m, 1.0, 0.0)

    def bit_step(i, prefix):
        cand = prefix | lax.shift_left(jnp.int32(1), 30 - i)
        cnt = total(one_hot(bits >= cand))
        return jnp.where(cnt >= cap, cand, prefix)

    thr = lax.fori_loop(0, 31, bit_step, jnp.zeros((1, 1), jnp.int32))
    gt = bits > thr
    eq = bits == thr
    need = cap - total(one_hot(gt))

    li = lax.broadcasted_iota(jnp.int32, (LANE, LANE), 0)
    lj = lax.broadcasted_iota(jnp.int32, (LANE, LANE), 1)
    lane_incl = one_hot(li <= lj).astype(BF16)
    ones = jnp.ones((LANE, LANE), BF16)
    bi = lax.broadcasted_iota(jnp.int32, (nb, nb), 0)
    bj = lax.broadcasted_iota(jnp.int32, (nb, nb), 1)
    blk_before = one_hot(bj < bi).astype(BF16)

    def ranks(mask):
        m = one_hot(mask).astype(BF16)
        within = dot(m, lane_incl)
        tot = dot(m, ones)
        before = dot(blk_before, tot.astype(BF16))
        return within, tot, before

    w_eq, _, b_eq = ranks(eq)
    sel = gt | (eq & (w_eq + b_eq <= need))
    within, tot, before = ranks(sel)
    pos_ref[0] = jnp.where(sel, within + before - 1.0, -1.0)
    off_ref[0] = before.T[0:SUBLANE, :].astype(jnp.int32)

    within_t = within.T.astype(BF16)
    sel_t = one_hot(sel).T.astype(BF16)
    after = before + tot
    blk_id = lax.broadcasted_iota(jnp.int32, (nb, LANE), 0).astype(F32)
    lane_id = lax.broadcasted_iota(jnp.int32, (LANE, LANE), 0).astype(F32)
    for c in range(cap // LANE):
        j = (lax.broadcasted_iota(jnp.int32, (1, LANE), 1) + c * LANE).astype(F32)
        holds = one_hot((before <= j) & (j < after))
        local = j - jnp.sum(holds * before, axis=0, keepdims=True) + 1.0
        blk = jnp.sum(holds * blk_id, axis=0, keepdims=True)
        hb = holds.astype(BF16)
        cnt_in_blk = dot(within_t, hb)
        sel_in_blk = dot(sel_t, hb)
        hit = one_hot((sel_in_blk > 0.5) & (cnt_in_blk == local))
        lane_of = jnp.sum(hit * lane_id, axis=0, keepdims=True)
        idx_ref[0, c:c + 1, :] = (blk * LANE + lane_of).astype(jnp.int32)


def _route(aff3, cap):
    e, nb, _ = aff3.shape
    blk = lambda rows: pl.BlockSpec((1, rows, LANE), lambda i: (i, 0, 0))
    return pl.pallas_call(
        functools.partial(_route_kernel, cap=cap),
        grid=(e,),
        in_specs=[blk(nb)],
        out_specs=[blk(nb), blk(cap // LANE), pl.BlockSpec((1, SUBLANE, nb), lambda i: (i, 0, 0))],
        out_shape=[jax.ShapeDtypeStruct((e, nb, LANE), F32),
                   jax.ShapeDtypeStruct((e, cap // LANE, LANE), jnp.int32),
                   jax.ShapeDtypeStruct((e, SUBLANE, nb), jnp.int32)],
        compiler_params=_params("parallel"),
        name="route",
    )(aff3)


def _sc_gather(x, idx):
    pieces, n, d = x.shape
    x = x.reshape(pieces * n, d)
    idx = (jnp.arange(pieces, dtype=idx.dtype)[:, None] * n + idx[None, :]).reshape(-1)
    m = idx.shape[0]
    mesh = plsc.VectorSubcoreMesh(core_axis_name="c", subcore_axis_name="s")

    @pl.kernel(out_type=jax.ShapeDtypeStruct((m, d), x.dtype), mesh=mesh, scratch_types=[])
    def gather_kernel(x_hbm, i_hbm, o_hbm):
        def body(i_vmem, o_vmem):
            pltpu.sync_copy(x_hbm.at[i_vmem.at[0]], o_vmem)

        pltpu.emit_pipeline(
            body,
            grid=(m // GATHER_WINDOW,),
            in_specs=[pl.BlockSpec((1, GATHER_WINDOW), lambda i: (0, i))],
            out_specs=[pl.BlockSpec((GATHER_WINDOW, d), lambda i: (i, 0))],
            core_axis_name=("c", "s"),
            dimension_semantics=(pltpu.PARALLEL,),
        )(i_hbm, o_hbm)

    return gather_kernel(x, idx.reshape(1, m)).reshape(pieces, m // pieces, d)


def _ffn_kernel(x_ref, wg_ref, wu_ref, wd_ref, ye_ref, xe_ref, acc_ref):
    f = pl.program_id(2)
    pieces, _, piece = x_ref.shape

    @pl.when(f == 0)
    def _():
        acc_ref[...] = jnp.zeros_like(acc_ref)
        for q in range(pieces):
            xe_ref[:, q * piece:(q + 1) * piece] = x_ref[q].astype(BF16)

    xe = xe_ref[...]
    a = jnp.dot(xe, wg_ref[0].astype(BF16), preferred_element_type=F32)
    b = jnp.dot(xe, wu_ref[0].astype(BF16), preferred_element_type=F32)
    h = (a * jax.nn.sigmoid(a) * b).astype(BF16)
    acc_ref[...] += jnp.dot(h, wd_ref[0].astype(BF16), preferred_element_type=F32)

    @pl.when(f == pl.num_programs(2) - 1)
    def _():
        ye_ref[0] = acc_ref[...].astype(ye_ref.dtype)


def _expert_ffn(xe, w_gate, w_up, w_down, tm=2048, tf=256):
    e = w_gate.shape[0]
    pieces, rows, piece = xe.shape
    cap, d = rows // e, piece * pieces
    dff = w_gate.shape[-1]
    tiles = cap // tm
    return pl.pallas_call(
        _ffn_kernel,
        grid=(e, tiles, dff // tf),
        in_specs=[pl.BlockSpec((pieces, tm, piece), lambda i, r, f: (0, i * tiles + r, 0)),
                  pl.BlockSpec((1, d, tf), lambda i, r, f: (i, 0, f)),
                  pl.BlockSpec((1, d, tf), lambda i, r, f: (i, 0, f)),
                  pl.BlockSpec((1, tf, d), lambda i, r, f: (i, f, 0))],
        out_specs=pl.BlockSpec((1, tm, d), lambda i, r, f: (i, r, 0)),
        out_shape=jax.ShapeDtypeStruct((e, cap, d), BF16),
        scratch_shapes=[pltpu.VMEM((tm, d), BF16), pltpu.VMEM((tm, d), F32)],
        compiler_params=_params("parallel", "parallel", "arbitrary"),
        name="expert_ffn",
    )(xe, w_gate, w_up, w_down)


COMBINE_ROWS = LANE + 16


def _combine_kernel(r0_ref, x1_ref, aff_ref, pos_ref, g_ref, *rest):
    ye_refs, y_ref = rest[:N_EXPERTS], rest[N_EXPERTS]
    i = pl.program_id(0)
    nb = pl.num_programs(0)
    t = x1_ref.shape[0]
    pos_t = jnp.concatenate([pos_ref[:, 0, :], jnp.zeros((LANE - N_EXPERTS, LANE), F32)], axis=0).T
    aff = aff_ref[...]
    col = lax.broadcasted_iota(jnp.int32, (t, COMBINE_ROWS), 1)
    acc = x1_ref[...]
    for e in range(N_EXPERTS):
        r0 = r0_ref[e * nb + i]
        slot = jnp.broadcast_to(pos_t[:, e:e + 1], (t, COMBINE_ROWS))
        pick = jnp.where(slot == (col + r0).astype(F32), 1.0, 0.0).astype(BF16)
        contrib = jnp.dot(pick, ye_refs[e][...], preferred_element_type=F32)
        acc = acc + jnp.broadcast_to(aff[:, e:e + 1], contrib.shape) * contrib
    y_ref[...] = acc * lax.rsqrt(jnp.mean(acc * acc, axis=-1, keepdims=True) + EPS) * g_ref[...]


def _combine(x1, aff, pos3, off, ye, final_g):
    n, d = x1.shape
    nb = n // LANE
    cap = ye.shape[1]
    r0 = jnp.minimum(off[:, 0, :] // 16 * 16, cap - COMBINE_ROWS)
    pos4 = pos3.reshape(N_EXPERTS, nb, 1, LANE)
    ye_spec = lambda e: pl.BlockSpec((None, pl.Element(COMBINE_ROWS), pl.Element(d)),
                                     lambda i, r0_ref: (e, pl.multiple_of(r0_ref[e * nb + i], 16), 0))
    grid_spec = pltpu.PrefetchScalarGridSpec(
        num_scalar_prefetch=1,
        grid=(nb,),
        in_specs=[pl.BlockSpec((LANE, d), lambda i, r: (i, 0)),
                  pl.BlockSpec((LANE, LANE), lambda i, r: (i, 0)),
                  pl.BlockSpec((N_EXPERTS, None, 1, LANE), lambda i, r: (0, i, 0, 0)),
                  pl.BlockSpec((1, d), lambda i, r: (0, 0))] + [ye_spec(e) for e in range(N_EXPERTS)],
        out_specs=pl.BlockSpec((LANE, d), lambda i, r: (i, 0)),
    )
    return pl.pallas_call(
        _combine_kernel,
        grid_spec=grid_spec,
        out_shape=jax.ShapeDtypeStruct((n, d), F32),
        compiler_params=_params("parallel"),
        name="moe_combine",
    )(r0.reshape(-1).astype(jnp.int32), x1, aff, pos4, final_g, *([ye] * N_EXPERTS))


def _dn_prep_kernel(x_ref, xp_ref, xn_ref, w_ref, q_ref, k_ref, v_ref, kt_ref):
    t = pl.program_id(1)
    nt = pl.num_programs(1)
    x = x_ref[0]
    rows = x.shape[0]
    row = lax.broadcasted_iota(jnp.int32, (rows, 1), 0)
    before = jnp.where(t > 0, xp_ref[0, SUBLANE - 1:SUBLANE, :], 0.0)
    after = jnp.where(t < nt - 1, xn_ref[0, 0:1, :], 0.0)
    x_prev = jnp.where(row == 0, before, pltpu.roll(x, 1, axis=0))
    x_next = jnp.where(row == rows - 1, after, pltpu.roll(x, rows - 1, axis=0))
    y = w_ref[0:1, :] * x_prev + w_ref[1:2, :] * x + w_ref[2:3, :] * x_next
    y = y * jax.nn.sigmoid(y)
    for h in range(DN_HEADS):
        for part, ref, scale in ((0, q_ref, DN_HEAD_DIM ** -0.5), (1, k_ref, 1.0)):
            c0 = part * DN_WIDTH + h * DN_HEAD_DIM
            a = y[:, c0:c0 + DN_HEAD_DIM]
            inv = lax.rsqrt(jnp.sum(a * a, axis=-1, keepdims=True) + EPS) * scale
            a = a * inv
            ref[0, :, h * DN_HEAD_DIM:(h + 1) * DN_HEAD_DIM] = a.astype(ref.dtype)
            if part == 1:
                for j in range(rows // CHUNK):
                    kt_ref[0, j, h * DN_HEAD_DIM:(h + 1) * DN_HEAD_DIM, :] = (
                        a[j * CHUNK:(j + 1) * CHUNK, :].T.astype(kt_ref.dtype))
    v_ref[0] = y[:, 2 * DN_WIDTH:].astype(v_ref.dtype)


def _dn_prep(qkv, conv_w, tile=512):
    b, s, c = qkv.shape
    tile = min(tile, s)
    nt = s // tile
    per = tile // SUBLANE
    nsub = s // SUBLANE
    out = jax.ShapeDtypeStruct((b, s, DN_WIDTH), BF16)
    ospec = pl.BlockSpec((1, tile, DN_WIDTH), lambda bi, ti: (bi, ti, 0))
    return pl.pallas_call(
        _dn_prep_kernel,
        grid=(b, nt),
        in_specs=[pl.BlockSpec((1, tile, c), lambda bi, ti: (bi, ti, 0)),
                  pl.BlockSpec((1, SUBLANE, c), lambda bi, ti: (bi, jnp.maximum(ti * per - 1, 0), 0)),
                  pl.BlockSpec((1, SUBLANE, c), lambda bi, ti: (bi, jnp.minimum((ti + 1) * per, nsub - 1), 0)),
                  pl.BlockSpec((CONV_K, c), lambda bi, ti: (0, 0))],
        out_specs=[ospec, ospec, ospec,
                   pl.BlockSpec((1, tile // CHUNK, DN_WIDTH, CHUNK), lambda bi, ti: (bi, ti, 0, 0))],
        out_shape=[out, out, out, jax.ShapeDtypeStruct((b, s // CHUNK, DN_WIDTH, CHUNK), BF16)],
        compiler_params=_params("parallel", "parallel"),
        name="dn_prep",
    )(qkv, qkv, qkv, conv_w)


def _lane_bcast(a, col, width=LANE):
    return jnp.broadcast_to(a[:, col:col + 1], (a.shape[0], width))


DN_TILE = 1024
INV_STEPS = 5
assert 2 ** (INV_STEPS + 1) == CHUNK
DN_UNROLL = 64
DN_BUILD_UNROLL = 4


def _dn_scan_kernel(q_ref, k_ref, v_ref, kt_ref, ba_ref, alog_ref, dtb_ref, o_ref,
                    s_ref, xbuf, tbuf, abuf, rbuf, uwbuf, qgbuf, kdbuf, lbuf, nbuf, obuf, ecbuf, *, reverse):
    @pl.when(pl.program_id(1) == 0)
    def _():
        s_ref[...] = jnp.zeros_like(s_ref)

    tile = q_ref.shape[1]
    nc = tile // CHUNK
    n_prob = nc * DN_HEADS
    unroll = min(DN_UNROLL, n_prob)
    beta_col = DN_HEADS if reverse else 0
    a_col = 2 * DN_HEADS + (DN_HEADS if reverse else 0)
    ri = lax.broadcasted_iota(jnp.int32, (CHUNK, CHUNK), 0)
    ci = lax.broadcasted_iota(jnp.int32, (CHUNK, CHUNK), 1)
    tri = (ri <= ci) if reverse else (ri >= ci)
    strict = (ri < ci) if reverse else (ri > ci)
    eye = jnp.where(ri == ci, 1.0, 0.0)
    row = lax.broadcasted_iota(jnp.int32, (CHUNK, LANE), 0)
    a_scale = jnp.exp(alog_ref[...])
    dtb = dtb_ref[...]
    dot = lambda a, b: jnp.dot(a, b, preferred_element_type=F32)

    def build(c, carry):
        rows = pl.ds(pl.multiple_of(c * CHUNK, CHUNK), CHUNK)
        raw = ba_ref[0, rows, :]
        beta_all = jax.nn.sigmoid(raw)
        xs = raw + dtb
        g = -a_scale * (jnp.maximum(xs, 0.0) + jnp.log(1.0 + jnp.exp(-jnp.abs(xs))))
        gs = g
        sh = 1
        while sh < CHUNK:
            if reverse:
                gs = gs + jnp.where(row < CHUNK - sh, pltpu.roll(gs, CHUNK - sh, axis=0), 0.0)
            else:
                gs = gs + jnp.where(row >= sh, pltpu.roll(gs, sh, axis=0), 0.0)
            sh *= 2
        gs_t = gs.T
        g_last = gs[0:1, :] if reverse else gs[CHUNK - 1:CHUNK, :]
        eg_all = jnp.exp(gs)
        ek_t = jnp.exp(g_last.T - gs_t)
        ecbuf[c] = jnp.broadcast_to(jnp.exp(g_last), (SUBLANE, LANE))
        for h in range(DN_HEADS):
            p = c * DN_HEADS + h
            hs = slice(h * DN_HEAD_DIM, (h + 1) * DN_HEAD_DIM)
            q = q_ref[0, rows, hs].astype(F32)
            k = k_ref[0, rows, hs]
            v = v_ref[0, rows, hs].astype(F32)
            beta = _lane_bcast(beta_all, beta_col + h)
            eg = _lane_bcast(eg_all, a_col + h)
            diff = _lane_bcast(gs, a_col + h, CHUNK) - gs_t[a_col + h:a_col + h + 1, :]
            decay = jnp.where(tri, jnp.exp(jnp.where(tri, diff, 0.0)), 0.0)
            kb = k.astype(F32) * beta
            kq = lax.dot_general(jnp.concatenate([kb, q], axis=0).astype(BF16), k,
                                 (((1,), (1,)), ((), ())), preferred_element_type=F32)
            lower = jnp.where(strict, kq[:CHUNK] * decay, 0.0)
            xbuf[p] = (-lower).astype(BF16)
            tbuf[p] = eye - lower
            abuf[p] = (kq[CHUNK:] * decay).astype(BF16)
            rbuf[p] = jnp.concatenate([v * beta, kb * eg], axis=1).astype(BF16)
            qgbuf[p] = q * eg
            kdbuf[p] = (kt_ref[0, c, hs, :].astype(F32)
                        * ek_t[a_col + h:a_col + h + 1, :]).astype(BF16)
        return carry

    lax.fori_loop(0, nc, build, 0, unroll=min(DN_BUILD_UNROLL, nc))

    for step in range(INV_STEPS + 1):
        def double(p, carry, first=(step == 0), last=(step == INV_STEPS)):
            x = xbuf[p]
            if not first:
                t = tbuf[p]
                tbuf[p] = t + dot(t.astype(BF16), x)
            if not last:
                xbuf[p] = dot(x, x).astype(BF16)
            return carry
        lax.fori_loop(0, n_prob, double, 0, unroll=unroll)

    def solve(p, carry):
        uwbuf[p] = dot(tbuf[p].astype(BF16), rbuf[p]).astype(BF16)
        return carry

    lax.fori_loop(0, n_prob, solve, 0, unroll=unroll)

    def fold(p, carry):
        uw = uwbuf[p]
        a_uw = dot(abuf[p], uw)
        k_uw = dot(kdbuf[p], uw)
        obuf[p] = a_uw[:, :DN_HEAD_DIM]
        nbuf[p] = k_uw[:, :DN_HEAD_DIM]
        lbuf[p, :DN_HEAD_DIM, :] = k_uw[:, DN_HEAD_DIM:].astype(BF16)
        lbuf[p, DN_HEAD_DIM:, :] = (qgbuf[p] - a_uw[:, DN_HEAD_DIM:]).astype(BF16)
        return carry

    lax.fori_loop(0, n_prob, fold, 0, unroll=unroll)

    def scan(jc, carry):
        c = (nc - 1 - jc) if reverse else jc
        rows = pl.ds(pl.multiple_of(c * CHUNK, CHUNK), CHUNK)
        ec_all = ecbuf[c]
        for h in range(DN_HEADS):
            p = c * DN_HEADS + h
            state = s_ref[h]
            r = dot(lbuf[p], state.astype(BF16))
            s_ref[h] = state * _lane_bcast(ec_all[0:1], a_col + h) - r[:DN_HEAD_DIM] + nbuf[p]
            o = r[DN_HEAD_DIM:] + obuf[p]
            o_ref[0, rows, h * DN_HEAD_DIM:(h + 1) * DN_HEAD_DIM] = o.astype(o_ref.dtype)
        return carry

    lax.fori_loop(0, nc, scan, 0)


def _dn_scan(q, k, v, kt, ba, a_log_row, dt_bias_row, reverse, tile=DN_TILE):
    b, s, _ = q.shape
    tile = min(tile, s)
    nt = s // tile
    nc = tile // CHUNK
    n_prob = nc * DN_HEADS
    tmap = (lambda bi, ti: (bi, nt - 1 - ti, 0)) if reverse else (lambda bi, ti: (bi, ti, 0))
    wide = pl.BlockSpec((1, tile, DN_WIDTH), tmap)
    par = pl.BlockSpec((1, LANE), lambda bi, ti: (0, 0))
    return pl.pallas_call(
        functools.partial(_dn_scan_kernel, reverse=reverse),
        grid=(b, nt),
        in_specs=[wide, wide, wide,
                  pl.BlockSpec((1, nc, DN_WIDTH, CHUNK), lambda bi, ti: tmap(bi, ti) + (0,)),
                  pl.BlockSpec((1, tile, LANE), tmap), par, par],
        out_specs=wide,
        out_shape=jax.ShapeDtypeStruct((b, s, DN_WIDTH), BF16),
        scratch_shapes=[pltpu.VMEM((DN_HEADS, DN_HEAD_DIM, DN_HEAD_DIM), F32),
                        pltpu.VMEM((n_prob, CHUNK, CHUNK), BF16),
                        pltpu.VMEM((n_prob, CHUNK, CHUNK), F32),
                        pltpu.VMEM((n_prob, CHUNK, CHUNK), BF16),
                        pltpu.VMEM((n_prob, CHUNK, 2 * DN_HEAD_DIM), BF16),
                        pltpu.VMEM((n_prob, CHUNK, 2 * DN_HEAD_DIM), BF16),
                        pltpu.VMEM((n_prob, CHUNK, DN_HEAD_DIM), F32),
                        pltpu.VMEM((n_prob, DN_HEAD_DIM, CHUNK), BF16),
                        pltpu.VMEM((n_prob, DN_HEAD_DIM + CHUNK, DN_HEAD_DIM), BF16),
                        pltpu.VMEM((n_prob, DN_HEAD_DIM, DN_HEAD_DIM), F32),
                        pltpu.VMEM((n_prob, CHUNK, DN_HEAD_DIM), F32),
                        pltpu.VMEM((nc, SUBLANE, LANE), F32)],
        compiler_params=_params("parallel", "arbitrary"),
        name="dn_scan_bwd" if reverse else "dn_scan_fwd",
    )(q, k, v, kt, ba, a_log_row, dt_bias_row)


def _gate_rows(a_log_fwd, a_log_bwd, dt_bias_fwd, dt_bias_bwd):
    pad = lambda f, b: jnp.pad(jnp.concatenate([f, b]).astype(F32), (2 * DN_HEADS, LANE - 4 * DN_HEADS))[None]
    return pad(a_log_fwd, a_log_bwd), pad(dt_bias_fwd, dt_bias_bwd)


def _deltanet(qkv, ba, conv_w, a_log_fwd, dt_bias_fwd, a_log_bwd, dt_bias_bwd):
    q, k, v, kt = _dn_prep(qkv, conv_w)
    alog, dtb = _gate_rows(a_log_fwd, a_log_bwd, dt_bias_fwd, dt_bias_bwd)
    o_f = _dn_scan(q, k, v, kt, ba, alog, dtb, reverse=False)
    o_b = _dn_scan(q, k, v, kt, ba, alog, dtb, reverse=True)
    return o_f, o_b


def _t5_bucket(rel):
    nb = NUM_BUCKETS // 2
    ret = jnp.where(rel > 0, nb, 0)
    n = jnp.abs(rel)
    max_exact = nb // 2
    nf = jnp.maximum(n, 1).astype(F32)
    large = max_exact + (jnp.log(nf / max_exact) / math.log(MAX_DISTANCE / max_exact)
                         * (nb - max_exact)).astype(jnp.int32)
    large = jnp.minimum(large, nb - 1)
    return ret + jnp.where(n < max_exact, n, large)


ATT_TILE = 2048
ATT_HALO = BAND * max(d for _, d in DILATED_PATTERNS)
QBLK = 2 * BAND
KBLK = QBLK + 2 * BAND


def _attn_bias_table(rel_bias):
    delta = (jnp.arange(KBLK) - BAND)[None, :] - jnp.arange(QBLK)[:, None]
    tabs = []
    for window, dil in DILATED_PATTERNS:
        half = window // (2 * dil)
        onehot = (_t5_bucket(delta * dil)[..., None] == jnp.arange(NUM_BUCKETS)).astype(F32)
        bias = jnp.einsum('qkb,bh->hqk', onehot, rel_bias.astype(F32),
                          precision=lax.Precision.HIGHEST)
        tabs.append(jnp.where((jnp.abs(delta) <= half)[None], bias, NEG))
    return jnp.stack(tabs)


def _attn_kernel(q_ref, kp_ref, kc_ref, kn_ref, vp_ref, vc_ref, vn_ref, bias_ref, o_ref,
                 qd, kd, vd, od, md, ld, o_sc, m_sc, l_sc, *, seq_len):
    t = pl.program_id(1)
    lane = lax.broadcasted_iota(jnp.int32, (QBLK, LANE), 1)
    low = lane < ATT_HEAD_DIM
    scale = ATT_HEAD_DIM ** -0.5
    kcol = lax.broadcasted_iota(jnp.int32, (1, KBLK), 1)
    ones = jnp.ones((KBLK, LANE), BF16)

    for p, (window, dil) in enumerate(DILATED_PATTERNS):
        lt = ATT_TILE // dil
        lh = ATT_HALO // dil
        span = lt + 2 * lh
        nblk = lt // QBLK
        n_pos = seq_len // dil
        for r in range(dil):
            sl = lambda n: pl.ds(r, n, stride=dil) if dil > 1 else pl.ds(0, n)
            qd[pl.ds(r * lt, lt), :] = q_ref[0, sl(lt), :] * scale
            off = r * span
            for (kr, vr, n) in ((kp_ref, vp_ref, lh), (kc_ref, vc_ref, lt), (kn_ref, vn_ref, lh)):
                kd[pl.ds(off, n), :] = kr[0, sl(n), :].astype(BF16)
                vd[pl.ds(off, n), :] = vr[0, sl(n), :].astype(BF16)
                off += n

        for i in range(dil * nblk):
            r, j = divmod(i, nblk)
            q0 = r * lt + j * QBLK
            k0 = r * span + lh + j * QBLK - BAND
            q = qd[pl.ds(q0, QBLK), :]
            k = kd[pl.ds(k0, KBLK), :]
            v1 = jnp.concatenate([vd[pl.ds(k0, KBLK), :], ones], axis=1)
            edge = j == 0 or j == nblk - 1
            if edge:
                pos = t * lt + j * QBLK - BAND + kcol
                valid = (pos >= 0) & (pos < n_pos)
            res = []
            for h in range(2):
                qh = jnp.where(low, q, 0.0) if h == 0 else jnp.where(low, 0.0, q)
                s = lax.dot_general(qh.astype(BF16), k, (((1,), (1,)), ((), ())),
                                    preferred_element_type=F32) + bias_ref[p, h]
                if edge:
                    s = jnp.where(valid, s, NEG)
                m = jnp.max(s, axis=-1, keepdims=True)
                e = jnp.exp(s - m).astype(BF16)
                res.append((jnp.dot(e, v1, preferred_element_type=F32), m))
            (pv0, m0), (pv1, m1) = res
            od[pl.ds(q0, QBLK), :] = jnp.where(low, pv0[:, :LANE], pv1[:, :LANE])
            ld[pl.ds(q0, QBLK), :] = jnp.where(low, pv0[:, LANE:], pv1[:, LANE:])
            md[pl.ds(q0, QBLK), :] = jnp.where(low, m0, m1)

        for r in range(dil):
            sl = pl.ds(r, lt, stride=dil) if dil > 1 else pl.ds(0, lt)
            src = pl.ds(r * lt, lt)
            o_sc[p, sl, :] = od[src, :]
            m_sc[p, sl, :] = md[src, :]
            l_sc[p, sl, :] = ld[src, :]

    n_p = len(DILATED_PATTERNS)
    mx = m_sc[0]
    for p in range(1, n_p):
        mx = jnp.maximum(mx, m_sc[p])
    num = jnp.zeros((ATT_TILE, LANE), F32)
    den = jnp.zeros((ATT_TILE, LANE), F32)
    for p in range(n_p):
        w = jnp.exp(m_sc[p] - mx)
        num += o_sc[p] * w
        den += l_sc[p] * w
    o_ref[0] = (num / den).astype(o_ref.dtype)


def _dilated_attention(att, bias):
    b, s, _ = att.shape
    assert s % ATT_TILE == 0
    nt = s // ATT_TILE
    nh = s // ATT_HALO
    per = ATT_TILE // ATT_HALO
    hp = ATT_WIDTH // LANE
    cur = lambda col0: pl.BlockSpec((1, ATT_TILE, LANE), lambda bi, ti, hi: (bi, ti, col0 + hi))
    prev = lambda col0: pl.BlockSpec(
        (1, ATT_HALO, LANE), lambda bi, ti, hi: (bi, jnp.maximum(ti * per - 1, 0), col0 + hi))
    nxt = lambda col0: pl.BlockSpec(
        (1, ATT_HALO, LANE), lambda bi, ti, hi: (bi, jnp.minimum((ti + 1) * per, nh - 1), col0 + hi))
    n_p = len(DILATED_PATTERNS)
    tile_f32 = pltpu.VMEM((ATT_TILE, LANE), F32)
    span_bf16 = pltpu.VMEM((ATT_TILE + 2 * ATT_HALO, LANE), BF16)
    per_pattern = pltpu.VMEM((n_p, ATT_TILE, LANE), F32)
    return pl.pallas_call(
        functools.partial(_attn_kernel, seq_len=s),
        grid=(b, nt, hp),
        in_specs=[cur(0), prev(hp), cur(hp), nxt(hp), prev(2 * hp), cur(2 * hp), nxt(2 * hp),
                  pl.BlockSpec((n_p, 2, QBLK, KBLK), lambda bi, ti, hi: (0, hi, 0, 0))],
        out_specs=pl.BlockSpec((1, ATT_TILE, LANE), lambda bi, ti, hi: (bi, ti, hi)),
        out_shape=jax.ShapeDtypeStruct((b, s, ATT_WIDTH), BF16),
        scratch_shapes=[tile_f32, span_bf16, span_bf16, tile_f32, tile_f32, tile_f32,
                        per_pattern, per_pattern, per_pattern],
        compiler_params=_params("parallel", "parallel", "parallel"),
        name="dilated_attention",
    )(att, att, att, att, att, att, att, bias)


def _mix(x, att_bias, norm1_g, w_in, conv_w, a_log_fwd, dt_bias_fwd, a_log_bwd, dt_bias_bwd,
         dn_norm_g, w_out, norm2_g, w_router):
    bsz, s, d = x.shape
    xf = x.reshape(bsz * s, d)
    qkv, z, ba, att = _in_proj(xf, norm1_g[None], w_in)
    shp = lambda a: a.reshape(bsz, s, a.shape[-1])
    o_f, o_b = _deltanet(shp(qkv), shp(ba), conv_w, a_log_fwd, dt_bias_fwd, a_log_bwd, dt_bias_bwd)
    att_out = _dilated_attention(shp(att), att_bias)
    flat = lambda a: a.reshape(bsz * s, a.shape[-1])
    return _out_proj(flat(o_f), flat(o_b), z, dn_norm_g[None], flat(att_out), xf, w_out, norm2_g[None],
                     w_router)


def kernel(x_prompt, x_sample, rel_bias, norm1_g, w_in, conv_w, a_log_fwd, dt_bias_fwd, a_log_bwd,
           dt_bias_bwd, dn_norm_g, w_out, norm2_g, w_router, w_gate, w_up, w_down, final_norm_g):
    groups = (x_prompt, x_sample)
    att_bias = _attn_bias_table(rel_bias)
    parts = []
    for x in groups:
        x1, h2, aff, afft = _mix(x, att_bias, norm1_g[0], w_in[0], conv_w[0], a_log_fwd[0], dt_bias_fwd[0],
                                 a_log_bwd[0], dt_bias_bwd[0], dn_norm_g[0], w_out[0], norm2_g[0],
                                 w_router[0])
        n_tok = x1.shape[0]
        cap = CAPACITY_FACTOR * n_tok // N_EXPERTS
        pos, idx, off = _route(afft, cap)
        xe = _sc_gather(h2, idx.reshape(-1))
        parts.append((x, x1, aff, pos, off, xe))
    outs = []
    for x, x1, aff, pos, off, xe in parts:
        ye = _expert_ffn(xe, w_gate[0], w_up[0], w_down[0])
        outs.append(_combine(x1, aff, pos, off, ye, final_norm_g[None]).reshape(x.shape))
    return tuple(outs)
```

```python
import functools
import math

import jax
import jax.numpy as jnp
from jax import lax
from jax.experimental import pallas as pl
from jax.experimental.pallas import tpu as pltpu
from jax.experimental.pallas import tpu_sc as plsc

D_MODEL = 1024
DN_HEADS = 4
DN_HEAD_DIM = 128
DN_WIDTH = DN_HEADS * DN_HEAD_DIM
ATT_HEADS = 8
ATT_HEAD_DIM = 64
ATT_WIDTH = ATT_HEADS * ATT_HEAD_DIM
CONV_K = 3
CHUNK = 64
DILATED_PATTERNS = ((128, 1), (512, 4), (2048, 16))
BAND = 64
NUM_BUCKETS = 32
MAX_DISTANCE = 1024
N_EXPERTS = 16
CAPACITY_FACTOR = 2
EXPERT_D_FF = 2816
EPS = 1e-6
NEG = -1e30
GATE_COLS = 4 * DN_HEADS
LANE = 128
SUBLANE = 8
GATHER_WINDOW = 128
GATHER_SPLIT = 4

F32 = jnp.float32
BF16 = jnp.bfloat16

VMEM_LIMIT = 56 * 1024 * 1024


def _params(*sem):
    return pltpu.CompilerParams(dimension_semantics=sem, vmem_limit_bytes=VMEM_LIMIT)


def _in_proj_kernel(x_ref, g_ref, wqkv_ref, wz_ref, wba_ref, watt_ref,
                    qkv_ref, z_ref, ba_ref, att_ref):
    x = x_ref[...]
    h = x * lax.rsqrt(jnp.mean(x * x, axis=-1, keepdims=True) + EPS) * g_ref[...]
    hb = h.astype(BF16)
    qkv_ref[...] = jnp.dot(hb, wqkv_ref[...], preferred_element_type=F32)
    z_ref[...] = jnp.dot(hb, wz_ref[...], preferred_element_type=F32)
    ba_ref[...] = jnp.dot(hb, wba_ref[...], preferred_element_type=F32)
    att_ref[...] = jnp.dot(hb, watt_ref[...], preferred_element_type=F32)


def _in_proj(x, g, w_in, tm=512):
    n = x.shape[0]
    o_qkv = 3 * DN_WIDTH
    o_z = o_qkv + DN_WIDTH
    o_ba = o_z + GATE_COLS
    wb = w_in.astype(BF16)
    wqkv = wb[:, :o_qkv]
    wz = wb[:, o_qkv:o_z]
    wba = jnp.pad(wb[:, o_z:o_ba], ((0, 0), (0, LANE - GATE_COLS)))
    watt = wb[:, o_ba:]
    full = lambda a: pl.BlockSpec(a.shape, lambda i: (0, 0))
    row = lambda w: pl.BlockSpec((tm, w), lambda i: (i, 0))
    return pl.pallas_call(
        _in_proj_kernel,
        grid=(n // tm,),
        in_specs=[row(D_MODEL), full(g), full(wqkv), full(wz), full(wba), full(watt)],
        out_specs=[row(o_qkv), row(DN_WIDTH), row(LANE), row(3 * ATT_WIDTH)],
        out_shape=[jax.ShapeDtypeStruct((n, o_qkv), F32),
                   jax.ShapeDtypeStruct((n, DN_WIDTH), F32),
                   jax.ShapeDtypeStruct((n, LANE), F32),
                   jax.ShapeDtypeStruct((n, 3 * ATT_WIDTH), F32)],
        compiler_params=_params("parallel"),
        name="in_proj",
    )(x, g, wqkv, wz, wba, watt)


def _split(a):
    hi = a.astype(BF16)
    return hi, (a - hi.astype(F32)).astype(BF16)


def _out_proj_kernel(of_ref, ob_ref, z_ref, gdn_ref, att_ref, x_ref, wdn_ref, watt_ref, g_ref, wr_ref,
                     x1_ref, h2_ref, afft_ref):
    dot = lambda a, b: jnp.dot(a, b, preferred_element_type=F32)
    o = of_ref[...].astype(F32) + ob_ref[...].astype(F32)
    z = z_ref[...]
    gated = []
    for h in range(DN_HEADS):
        hs = slice(h * DN_HEAD_DIM, (h + 1) * DN_HEAD_DIM)
        oh = o[:, hs]
        oh = oh * lax.rsqrt(jnp.mean(oh * oh, axis=-1, keepdims=True) + EPS) * gdn_ref[...]
        zh = z[:, hs]
        gated.append((oh * (zh * jax.nn.sigmoid(zh))).astype(BF16))
    dn = jnp.concatenate(gated, axis=1)
    x1 = x_ref[...] + dot(dn, wdn_ref[...]) + dot(att_ref[...], watt_ref[...])
    x1_ref[...] = x1
    h2 = x1 * lax.rsqrt(jnp.mean(x1 * x1, axis=-1, keepdims=True) + EPS) * g_ref[...]
    rows = h2.shape[0]
    piece = D_MODEL // GATHER_SPLIT
    for q in range(GATHER_SPLIT):
        h2_ref[q] = h2[:, q * piece:(q + 1) * piece]
    hh, hl = _split(h2)
    wh, wl = _split(wr_ref[...])
    logits = dot(hh, wh) + (dot(hh, wl) + dot(hl, wh))
    lane = lax.broadcasted_iota(jnp.int32, logits.shape, 1)
    logits = jnp.where(lane < N_EXPERTS, logits, NEG)
    p = jnp.exp(logits - jnp.max(logits, axis=-1, keepdims=True))
    aff = p / jnp.sum(p, axis=-1, keepdims=True)
    for j in range(rows // LANE):
        afft_ref[:, j, :] = aff[j * LANE:(j + 1) * LANE, :].T[:N_EXPERTS, :]


def _out_proj(o_f, o_b, z, dn_norm_g, att, x, w_out, g, w_router, tm=1024):
    n = x.shape[0]
    wb = w_out.astype(BF16)
    wdn, watt = wb[:DN_WIDTH], wb[DN_WIDTH:]
    wr = jnp.pad(w_router.astype(F32), ((0, 0), (0, LANE - N_EXPERTS)))
    full = lambda a: pl.BlockSpec(a.shape, lambda i: (0, 0))
    row = lambda w: pl.BlockSpec((tm, w), lambda i: (i, 0))
    return pl.pallas_call(
        _out_proj_kernel,
        grid=(n // tm,),
        in_specs=[row(DN_WIDTH), row(DN_WIDTH), row(DN_WIDTH), full(dn_norm_g), row(ATT_WIDTH),
                  row(D_MODEL), full(wdn), full(watt), full(g), full(wr)],
        out_specs=[row(D_MODEL),
                   pl.BlockSpec((GATHER_SPLIT, tm, D_MODEL // GATHER_SPLIT), lambda i: (0, i, 0)),
                   pl.BlockSpec((N_EXPERTS, tm // LANE, LANE), lambda i: (0, i, 0))],
        out_shape=[jax.ShapeDtypeStruct((n, D_MODEL), F32),
                   jax.ShapeDtypeStruct((GATHER_SPLIT, n, D_MODEL // GATHER_SPLIT), F32),
                   jax.ShapeDtypeStruct((N_EXPERTS, n // LANE, LANE), F32)],
        compiler_params=_params("parallel"),
        name="out_proj",
    )(o_f, o_b, z, dn_norm_g, att, x, wdn, watt, g, wr)


def _route_kernel(aff_ref, pos_ref, idx_ref, off_ref, gate_ref, *, cap):
    x = aff_ref[0]
    nb = x.shape[0]
    bits = pltpu.bitcast(x, jnp.int32)
    total = lambda m: jnp.sum(jnp.sum(m, axis=0, keepdims=True), axis=1, keepdims=True)
    dot = lambda a, b: jnp.dot(a, b, preferred_element_type=F32)
    one_hot = lambda m: jnp.where(m, 1.0, 0.0)

    def bit_step(i, prefix):
        cand = prefix | lax.shift_left(jnp.int32(1), 30 - i)
        cnt = total(one_hot(bits >= cand))
        return jnp.where(cnt >= cap, cand, prefix)

    thr = lax.fori_loop(0, 31, bit_step, jnp.zeros((1, 1), jnp.int32))
    gt = bits > thr
    eq = bits == thr
    need = cap - total(one_hot(gt))

    li = lax.broadcasted_iota(jnp.int32, (LANE, LANE), 0)
    lj = lax.broadcasted_iota(jnp.int32, (LANE, LANE), 1)
    lane_incl = one_hot(li <= lj).astype(BF16)
    ones = jnp.ones((LANE, LANE), BF16)
    bi = lax.broadcasted_iota(jnp.int32, (nb, nb), 0)
    bj = lax.broadcasted_iota(jnp.int32, (nb, nb), 1)
    blk_before = one_hot(bj < bi).astype(BF16)

    def ranks(mask):
        m = one_hot(mask).astype(BF16)
        within = dot(m, lane_incl)
        tot = dot(m, ones)
        before = dot(blk_before, tot.astype(BF16))
        return within, tot, before

    w_eq, _, b_eq = ranks(eq)
    sel = gt | (eq & (w_eq + b_eq <= need))
    within, tot, before = ranks(sel)
    pos_ref[0] = jnp.where(sel, within + before - 1.0, -1.0)
    off_ref[0] = before.T[0:SUBLANE, :].astype(jnp.int32)

    within_t = within.T.astype(BF16)
    sel_t = one_hot(sel).T.astype(BF16)
    x_t = x.T
    x_parts = []
    for _ in range(3):
        part = x_t.astype(BF16)
        x_parts.append(part)
        x_t = x_t - part.astype(F32)
    after = before + tot
    blk_id = lax.broadcasted_iota(jnp.int32, (nb, LANE), 0).astype(F32)
    lane_id = lax.broadcasted_iota(jnp.int32, (LANE, LANE), 0).astype(F32)
    for c in range(cap // LANE):
        j = (lax.broadcasted_iota(jnp.int32, (1, LANE), 1) + c * LANE).astype(F32)
        holds = one_hot((before <= j) & (j < after))
        local = j - jnp.sum(holds * before, axis=0, keepdims=True) + 1.0
        blk = jnp.sum(holds * blk_id, axis=0, keepdims=True)
        hb = holds.astype(BF16)
        cnt_in_blk = dot(within_t, hb)
        sel_in_blk = dot(sel_t, hb)
        hit = one_hot((sel_in_blk > 0.5) & (cnt_in_blk == local))
        lane_of = jnp.sum(hit * lane_id, axis=0, keepdims=True)
        idx_ref[0, c:c + 1, :] = (blk * LANE + lane_of).astype(jnp.int32)
        aff_in_blk = dot(x_parts[0], hb) + dot(x_parts[1], hb) + dot(x_parts[2], hb)
        gate_ref[0, c:c + 1, :] = jnp.sum(hit * aff_in_blk, axis=0, keepdims=True)


def _route(aff3, cap):
    e, nb, _ = aff3.shape
    blk = lambda rows: pl.BlockSpec((1, rows, LANE), lambda i: (i, 0, 0))
    return pl.pallas_call(
        functools.partial(_route_kernel, cap=cap),
        grid=(e,),
        in_specs=[blk(nb)],
        out_specs=[blk(nb), blk(cap // LANE), pl.BlockSpec((1, SUBLANE, nb), lambda i: (i, 0, 0)),
                   blk(cap // LANE)],
        out_shape=[jax.ShapeDtypeStruct((e, nb, LANE), F32),
                   jax.ShapeDtypeStruct((e, cap // LANE, LANE), jnp.int32),
                   jax.ShapeDtypeStruct((e, SUBLANE, nb), jnp.int32),
                   jax.ShapeDtypeStruct((e, cap // LANE, LANE), F32)],
        compiler_params=_params("parallel"),
        name="route",
    )(aff3)


def _sc_gather(x, idx):
    pieces, n, d = x.shape
    x = x.reshape(pieces * n, d)
    idx = (jnp.arange(pieces, dtype=idx.dtype)[:, None] * n + idx[None, :]).reshape(-1)
    m = idx.shape[0]
    mesh = plsc.VectorSubcoreMesh(core_axis_name="c", subcore_axis_name="s")

    @pl.kernel(out_type=jax.ShapeDtypeStruct((m, d), x.dtype), mesh=mesh, scratch_types=[])
    def gather_kernel(x_hbm, i_hbm, o_hbm):
        def body(i_vmem, o_vmem):
            pltpu.sync_copy(x_hbm.at[i_vmem.at[0]], o_vmem)

        pltpu.emit_pipeline(
            body,
            grid=(m // GATHER_WINDOW,),
            in_specs=[pl.BlockSpec((1, GATHER_WINDOW), lambda i: (0, i))],
            out_specs=[pl.BlockSpec((GATHER_WINDOW, d), lambda i: (i, 0))],
            core_axis_name=("c", "s"),
            dimension_semantics=(pltpu.PARALLEL,),
        )(i_hbm, o_hbm)

    return gather_kernel(x, idx.reshape(1, m)).reshape(pieces, m // pieces, d)


def _ffn_kernel(x_ref, gate_ref, wg_ref, wu_ref, wd_ref, ye_ref, xe_ref, acc_ref):
    f = pl.program_id(2)
    pieces, _, piece = x_ref.shape

    @pl.when(f == 0)
    def _():
        acc_ref[...] = jnp.zeros_like(acc_ref)
        for q in range(pieces):
            xe_ref[:, q * piece:(q + 1) * piece] = x_ref[q].astype(BF16)

    xe = xe_ref[...]
    a = jnp.dot(xe, wg_ref[0].astype(BF16), preferred_element_type=F32)
    b = jnp.dot(xe, wu_ref[0].astype(BF16), preferred_element_type=F32)
    h = (a * jax.nn.sigmoid(a) * b).astype(BF16)
    acc_ref[...] += jnp.dot(h, wd_ref[0].astype(BF16), preferred_element_type=F32)

    @pl.when(f == pl.num_programs(2) - 1)
    def _():
        for c in range(gate_ref.shape[1]):
            rows = slice(c * LANE, (c + 1) * LANE)
            gate = jnp.broadcast_to(gate_ref[0, c:c + 1, :], (LANE, LANE)).T
            ye_ref[0, rows, :] = (acc_ref[rows, :] * jnp.concatenate([gate] * (acc_ref.shape[1] // LANE), axis=1)
                                  ).astype(ye_ref.dtype)


def _expert_ffn(xe, gate, w_gate, w_up, w_down, tm=2048, tf=256):
    e = w_gate.shape[0]
    pieces, rows, piece = xe.shape
    cap, d = rows // e, piece * pieces
    dff = w_gate.shape[-1]
    tiles = cap // tm
    return pl.pallas_call(
        _ffn_kernel,
        grid=(e, tiles, dff // tf),
        in_specs=[pl.BlockSpec((pieces, tm, piece), lambda i, r, f: (0, i * tiles + r, 0)),
                  pl.BlockSpec((1, tm // LANE, LANE), lambda i, r, f: (i, r, 0)),
                  pl.BlockSpec((1, d, tf), lambda i, r, f: (i, 0, f)),
                  pl.BlockSpec((1, d, tf), lambda i, r, f: (i, 0, f)),
                  pl.BlockSpec((1, tf, d), lambda i, r, f: (i, f, 0))],
        out_specs=pl.BlockSpec((1, tm, d), lambda i, r, f: (i, r, 0)),
        out_shape=jax.ShapeDtypeStruct((e, cap, d), BF16),
        scratch_shapes=[pltpu.VMEM((tm, d), BF16), pltpu.VMEM((tm, d), F32)],
        compiler_params=_params("parallel", "parallel", "arbitrary"),
        name="expert_ffn",
    )(xe, gate, w_gate, w_up, w_down)


COMBINE_TAIL = 16


def _combine_kernel(r0_ref, spill_ref, x1_ref, pos_ref, g_ref, *rest):
    ye_refs, y_ref = rest[:N_EXPERTS], rest[N_EXPERTS]
    i = pl.program_id(0)
    nb = pl.num_programs(0)
    t, d = x1_ref.shape
    pos_t = jnp.concatenate([pos_ref[:, 0, :], jnp.zeros((LANE - N_EXPERTS, LANE), F32)], axis=0).T
    col = lax.broadcasted_iota(jnp.int32, (t, LANE), 1)

    def gathered(rows, shift):
        out = jnp.zeros((t, d), F32)
        for e0 in range(0, N_EXPERTS, 2):
            picks = []
            for e in (e0, e0 + 1):
                first = r0_ref[e * nb + i] + shift
                slot = jnp.broadcast_to(pos_t[:, e:e + 1], (t, LANE))
                picks.append(jnp.where(slot == (col + first).astype(F32), 1.0, 0.0).astype(BF16))
            out = out + jnp.dot(jnp.concatenate(picks, axis=1), jnp.concatenate([rows(e0), rows(e0 + 1)], axis=0),
                                preferred_element_type=F32)
        return out

    def finish(x):
        y_ref[...] = x * lax.rsqrt(jnp.mean(x * x, axis=-1, keepdims=True) + EPS) * g_ref[...]

    acc = x1_ref[...] + gathered(lambda e: ye_refs[e][:LANE, :], 0)
    finish(acc)

    @pl.when(spill_ref[i] != 0)
    def _():
        pad = jnp.zeros((LANE - COMBINE_TAIL, d), BF16)
        finish(acc + gathered(lambda e: jnp.concatenate([ye_refs[e][LANE:, :], pad], axis=0), LANE))


def _combine(x1, pos3, off, ye, final_g):
    n, d = x1.shape
    nb = n // LANE
    cap = ye.shape[1]
    first = off[:, 0, :]
    end = jnp.concatenate([first[:, 1:], jnp.full((N_EXPERTS, 1), cap, first.dtype)], axis=1)
    r0 = jnp.minimum(first // 16 * 16, cap - LANE - COMBINE_TAIL)
    spill = jnp.any(end > r0 + LANE, axis=0).astype(jnp.int32)
    pos4 = pos3.reshape(N_EXPERTS, nb, 1, LANE)
    window = lambda e, rows, shift: pl.BlockSpec(
        (None, pl.Element(rows), pl.Element(d)),
        lambda i, r0_ref, spill_ref: (e, pl.multiple_of(r0_ref[e * nb + i] + shift, 16), 0))
    grid_spec = pltpu.PrefetchScalarGridSpec(
        num_scalar_prefetch=2,
        grid=(nb,),
        in_specs=[pl.BlockSpec((LANE, d), lambda i, r, s: (i, 0)),
                  pl.BlockSpec((N_EXPERTS, None, 1, LANE), lambda i, r, s: (0, i, 0, 0)),
                  pl.BlockSpec((1, d), lambda i, r, s: (0, 0))]
                 + [window(e, LANE + COMBINE_TAIL, 0) for e in range(N_EXPERTS)],
        out_specs=pl.BlockSpec((LANE, d), lambda i, r, s: (i, 0)),
    )
    return pl.pallas_call(
        _combine_kernel,
        grid_spec=grid_spec,
        out_shape=jax.ShapeDtypeStruct((n, d), F32),
        compiler_params=_params("parallel"),
        name="moe_combine",
    )(r0.reshape(-1).astype(jnp.int32), spill, x1, pos4, final_g, *([ye] * N_EXPERTS))


def _dn_prep_kernel(x_ref, xp_ref, xn_ref, w_ref, q_ref, k_ref, v_ref, kt_ref):
    t = pl.program_id(1)
    nt = pl.num_programs(1)
    x = x_ref[0]
    rows = x.shape[0]
    row = lax.broadcasted_iota(jnp.int32, (rows, 1), 0)
    before = jnp.where(t > 0, xp_ref[0, SUBLANE - 1:SUBLANE, :], 0.0)
    after = jnp.where(t < nt - 1, xn_ref[0, 0:1, :], 0.0)
    x_prev = jnp.where(row == 0, before, pltpu.roll(x, 1, axis=0))
    x_next = jnp.where(row == rows - 1, after, pltpu.roll(x, rows - 1, axis=0))
    y = w_ref[0:1, :] * x_prev + w_ref[1:2, :] * x + w_ref[2:3, :] * x_next
    y = y * jax.nn.sigmoid(y)
    for h in range(DN_HEADS):
        for part, ref, scale in ((0, q_ref, DN_HEAD_DIM ** -0.5), (1, k_ref, 1.0)):
            c0 = part * DN_WIDTH + h * DN_HEAD_DIM
            a = y[:, c0:c0 + DN_HEAD_DIM]
            inv = lax.rsqrt(jnp.sum(a * a, axis=-1, keepdims=True) + EPS) * scale
            a = a * inv
            ref[0, :, h * DN_HEAD_DIM:(h + 1) * DN_HEAD_DIM] = a.astype(ref.dtype)
            if part == 1:
                for j in range(rows // CHUNK):
                    kt_ref[0, j, h * DN_HEAD_DIM:(h + 1) * DN_HEAD_DIM, :] = (
                        a[j * CHUNK:(j + 1) * CHUNK, :].T.astype(kt_ref.dtype))
    v_ref[0] = y[:, 2 * DN_WIDTH:].astype(v_ref.dtype)


def _dn_prep(qkv, conv_w, tile=512):
    b, s, c = qkv.shape
    tile = min(tile, s)
    nt = s // tile
    per = tile // SUBLANE
    nsub = s // SUBLANE
    out = jax.ShapeDtypeStruct((b, s, DN_WIDTH), BF16)
    ospec = pl.BlockSpec((1, tile, DN_WIDTH), lambda bi, ti: (bi, ti, 0))
    return pl.pallas_call(
        _dn_prep_kernel,
        grid=(b, nt),
        in_specs=[pl.BlockSpec((1, tile, c), lambda bi, ti: (bi, ti, 0)),
                  pl.BlockSpec((1, SUBLANE, c), lambda bi, ti: (bi, jnp.maximum(ti * per - 1, 0), 0)),
                  pl.BlockSpec((1, SUBLANE, c), lambda bi, ti: (bi, jnp.minimum((ti + 1) * per, nsub - 1), 0)),
                  pl.BlockSpec((CONV_K, c), lambda bi, ti: (0, 0))],
        out_specs=[ospec, ospec, ospec,
                   pl.BlockSpec((1, tile // CHUNK, DN_WIDTH, CHUNK), lambda bi, ti: (bi, ti, 0, 0))],
        out_shape=[out, out, out, jax.ShapeDtypeStruct((b, s // CHUNK, DN_WIDTH, CHUNK), BF16)],
        compiler_params=_params("parallel", "parallel"),
        name="dn_prep",
    )(qkv, qkv, qkv, conv_w)


def _lane_bcast(a, col, width=LANE):
    return jnp.broadcast_to(a[:, col:col + 1], (a.shape[0], width))


DN_TILE = 1024
INV_STEPS = 5
assert 2 ** (INV_STEPS + 1) == CHUNK
DN_UNROLL = 64
DN_BUILD_UNROLL = 4


def _dn_scan_kernel(q_ref, k_ref, v_ref, kt_ref, ba_ref, alog_ref, dtb_ref, o_ref,
                    s_ref, xbuf, tbuf, abuf, rbuf, uwbuf, qgbuf, kdbuf, lbuf, nbuf, obuf, ecbuf, *, reverse):
    @pl.when(pl.program_id(1) == 0)
    def _():
        s_ref[...] = jnp.zeros_like(s_ref)

    tile = q_ref.shape[1]
    nc = tile // CHUNK
    n_prob = nc * DN_HEADS
    unroll = min(DN_UNROLL, n_prob)
    beta_col = DN_HEADS if reverse else 0
    a_col = 2 * DN_HEADS + (DN_HEADS if reverse else 0)
    ri = lax.broadcasted_iota(jnp.int32, (CHUNK, CHUNK), 0)
    ci = lax.broadcasted_iota(jnp.int32, (CHUNK, CHUNK), 1)
    tri = (ri <= ci) if reverse else (ri >= ci)
    strict = (ri < ci) if reverse else (ri > ci)
    eye = jnp.where(ri == ci, 1.0, 0.0)
    row = lax.broadcasted_iota(jnp.int32, (CHUNK, LANE), 0)
    a_scale = jnp.exp(alog_ref[...])
    dtb = dtb_ref[...]
    dot = lambda a, b: jnp.dot(a, b, preferred_element_type=F32)

    def build(c, carry):
        rows = pl.ds(pl.multiple_of(c * CHUNK, CHUNK), CHUNK)
        raw = ba_ref[0, rows, :]
        beta_all = jax.nn.sigmoid(raw)
        xs = raw + dtb
        g = -a_scale * (jnp.maximum(xs, 0.0) + jnp.log(1.0 + jnp.exp(-jnp.abs(xs))))
        gs = g
        sh = 1
        while sh < CHUNK:
            if reverse:
                gs = gs + jnp.where(row < CHUNK - sh, pltpu.roll(gs, CHUNK - sh, axis=0), 0.0)
            else:
                gs = gs + jnp.where(row >= sh, pltpu.roll(gs, sh, axis=0), 0.0)
            sh *= 2
        gs_t = gs.T
        g_last = gs[0:1, :] if reverse else gs[CHUNK - 1:CHUNK, :]
        eg_all = jnp.exp(gs)
        ek_t = jnp.exp(g_last.T - gs_t)
        ecbuf[c] = jnp.broadcast_to(jnp.exp(g_last), (SUBLANE, LANE))
        for h in range(DN_HEADS):
            p = c * DN_HEADS + h
            hs = slice(h * DN_HEAD_DIM, (h + 1) * DN_HEAD_DIM)
            q = q_ref[0, rows, hs].astype(F32)
            k = k_ref[0, rows, hs]
            v = v_ref[0, rows, hs].astype(F32)
            beta = _lane_bcast(beta_all, beta_col + h)
            eg = _lane_bcast(eg_all, a_col + h)
            diff = _lane_bcast(gs, a_col + h, CHUNK) - gs_t[a_col + h:a_col + h + 1, :]
            decay = jnp.where(tri, jnp.exp(jnp.where(tri, diff, 0.0)), 0.0)
            kb = k.astype(F32) * beta
            kq = lax.dot_general(jnp.concatenate([kb, q], axis=0).astype(BF16), k,
                                 (((1,), (1,)), ((), ())), preferred_element_type=F32)
            lower = jnp.where(strict, kq[:CHUNK] * decay, 0.0)
            xbuf[p] = (-lower).astype(BF16)
            tbuf[p] = eye - lower
            abuf[p] = (kq[CHUNK:] * decay).astype(BF16)
            rbuf[p] = jnp.concatenate([v * beta, kb * eg], axis=1).astype(BF16)
            qgbuf[p] = q * eg
            kdbuf[p] = (kt_ref[0, c, hs, :].astype(F32)
                        * ek_t[a_col + h:a_col + h + 1, :]).astype(BF16)
        return carry

    lax.fori_loop(0, nc, build, 0, unroll=min(DN_BUILD_UNROLL, nc))

    for step in range(INV_STEPS + 1):
        def double(p, carry, first=(step == 0), last=(step == INV_STEPS)):
            x = xbuf[p]
            if not first:
                t = tbuf[p]
                tbuf[p] = t + dot(t.astype(BF16), x)
            if not last:
                xbuf[p] = dot(x, x).astype(BF16)
            return carry
        lax.fori_loop(0, n_prob, double, 0, unroll=unroll)

    def solve(p, carry):
        uwbuf[p] = dot(tbuf[p].astype(BF16), rbuf[p]).astype(BF16)
        return carry

    lax.fori_loop(0, n_prob, solve, 0, unroll=unroll)

    def fold(p, carry):
        uw = uwbuf[p]
        a_uw = dot(abuf[p], uw)
        k_uw = dot(kdbuf[p], uw)
        obuf[p] = a_uw[:, :DN_HEAD_DIM]
        nbuf[p] = k_uw[:, :DN_HEAD_DIM]
        lbuf[p, :DN_HEAD_DIM, :] = k_uw[:, DN_HEAD_DIM:].astype(BF16)
        lbuf[p, DN_HEAD_DIM:, :] = (qgbuf[p] - a_uw[:, DN_HEAD_DIM:]).astype(BF16)
        return carry

    lax.fori_loop(0, n_prob, fold, 0, unroll=unroll)

    def scan(jc, carry):
        c = (nc - 1 - jc) if reverse else jc
        rows = pl.ds(pl.multiple_of(c * CHUNK, CHUNK), CHUNK)
        ec_all = ecbuf[c]
        for h in range(DN_HEADS):
            p = c * DN_HEADS + h
            state = s_ref[h]
            r = dot(lbuf[p], state.astype(BF16))
            s_ref[h] = state * _lane_bcast(ec_all[0:1], a_col + h) - r[:DN_HEAD_DIM] + nbuf[p]
            o = r[DN_HEAD_DIM:] + obuf[p]
            o_ref[0, rows, h * DN_HEAD_DIM:(h + 1) * DN_HEAD_DIM] = o.astype(o_ref.dtype)
        return carry

    lax.fori_loop(0, nc, scan, 0)


def _dn_scan(q, k, v, kt, ba, a_log_row, dt_bias_row, reverse, tile=DN_TILE):
    b, s, _ = q.shape
    tile = min(tile, s)
    nt = s // tile
    nc = tile // CHUNK
    n_prob = nc * DN_HEADS
    tmap = (lambda bi, ti: (bi, nt - 1 - ti, 0)) if reverse else (lambda bi, ti: (bi, ti, 0))
    wide = pl.BlockSpec((1, tile, DN_WIDTH), tmap)
    par = pl.BlockSpec((1, LANE), lambda bi, ti: (0, 0))
    return pl.pallas_call(
        functools.partial(_dn_scan_kernel, reverse=reverse),
        grid=(b, nt),
        in_specs=[wide, wide, wide,
                  pl.BlockSpec((1, nc, DN_WIDTH, CHUNK), lambda bi, ti: tmap(bi, ti) + (0,)),
                  pl.BlockSpec((1, tile, LANE), tmap), par, par],
        out_specs=wide,
        out_shape=jax.ShapeDtypeStruct((b, s, DN_WIDTH), BF16),
        scratch_shapes=[pltpu.VMEM((DN_HEADS, DN_HEAD_DIM, DN_HEAD_DIM), F32),
                        pltpu.VMEM((n_prob, CHUNK, CHUNK), BF16),
                        pltpu.VMEM((n_prob, CHUNK, CHUNK), F32),
                        pltpu.VMEM((n_prob, CHUNK, CHUNK), BF16),
                        pltpu.VMEM((n_prob, CHUNK, 2 * DN_HEAD_DIM), BF16),
                        pltpu.VMEM((n_prob, CHUNK, 2 * DN_HEAD_DIM), BF16),
                        pltpu.VMEM((n_prob, CHUNK, DN_HEAD_DIM), F32),
                        pltpu.VMEM((n_prob, DN_HEAD_DIM, CHUNK), BF16),
                        pltpu.VMEM((n_prob, DN_HEAD_DIM + CHUNK, DN_HEAD_DIM), BF16),
                        pltpu.VMEM((n_prob, DN_HEAD_DIM, DN_HEAD_DIM), F32),
                        pltpu.VMEM((n_prob, CHUNK, DN_HEAD_DIM), F32),
                        pltpu.VMEM((nc, SUBLANE, LANE), F32)],
        compiler_params=_params("parallel", "arbitrary"),
        name="dn_scan_bwd" if reverse else "dn_scan_fwd",
    )(q, k, v, kt, ba, a_log_row, dt_bias_row)


def _gate_rows(a_log_fwd, a_log_bwd, dt_bias_fwd, dt_bias_bwd):
    pad = lambda f, b: jnp.pad(jnp.concatenate([f, b]).astype(F32), (2 * DN_HEADS, LANE - 4 * DN_HEADS))[None]
    return pad(a_log_fwd, a_log_bwd), pad(dt_bias_fwd, dt_bias_bwd)


def _deltanet(qkv, ba, conv_w, a_log_fwd, dt_bias_fwd, a_log_bwd, dt_bias_bwd):
    q, k, v, kt = _dn_prep(qkv, conv_w)
    alog, dtb = _gate_rows(a_log_fwd, a_log_bwd, dt_bias_fwd, dt_bias_bwd)
    o_f = _dn_scan(q, k, v, kt, ba, alog, dtb, reverse=False)
    o_b = _dn_scan(q, k, v, kt, ba, alog, dtb, reverse=True)
    return o_f, o_b


def _t5_bucket(rel):
    nb = NUM_BUCKETS // 2
    ret = jnp.where(rel > 0, nb, 0)
    n = jnp.abs(rel)
    max_exact = nb // 2
    nf = jnp.maximum(n, 1).astype(F32)
    large = max_exact + (jnp.log(nf / max_exact) / math.log(MAX_DISTANCE / max_exact)
                         * (nb - max_exact)).astype(jnp.int32)
    large = jnp.minimum(large, nb - 1)
    return ret + jnp.where(n < max_exact, n, large)


ATT_TILE = 2048
ATT_HALO = BAND * max(d for _, d in DILATED_PATTERNS)
QBLK = 2 * BAND
KBLK = QBLK + 2 * BAND


def _attn_bias_table(rel_bias):
    delta = (jnp.arange(KBLK) - BAND)[None, :] - jnp.arange(QBLK)[:, None]
    tabs = []
    for window, dil in DILATED_PATTERNS:
        half = window // (2 * dil)
        onehot = (_t5_bucket(delta * dil)[..., None] == jnp.arange(NUM_BUCKETS)).astype(F32)
        bias = jnp.einsum('qkb,bh->hqk', onehot, rel_bias.astype(F32),
                          precision=lax.Precision.HIGHEST)
        tabs.append(jnp.where((jnp.abs(delta) <= half)[None], bias, NEG))
    return jnp.stack(tabs)


def _attn_kernel(q_ref, kp_ref, kc_ref, kn_ref, vp_ref, vc_ref, vn_ref, bias_ref, o_ref,
                 qd, kd, vd, od, md, ld, o_sc, m_sc, l_sc, *, seq_len):
    t = pl.program_id(1)
    lane = lax.broadcasted_iota(jnp.int32, (QBLK, LANE), 1)
    low = lane < ATT_HEAD_DIM
    scale = ATT_HEAD_DIM ** -0.5
    kcol = lax.broadcasted_iota(jnp.int32, (1, KBLK), 1)
    ones = jnp.ones((KBLK, LANE), BF16)

    for p, (window, dil) in enumerate(DILATED_PATTERNS):
        lt = ATT_TILE // dil
        lh = ATT_HALO // dil
        span = lt + 2 * lh
        nblk = lt // QBLK
        n_pos = seq_len // dil
        for r in range(dil):
            sl = lambda n: pl.ds(r, n, stride=dil) if dil > 1 else pl.ds(0, n)
            qd[pl.ds(r * lt, lt), :] = q_ref[0, sl(lt), :] * scale
            off = r * span
            for (kr, vr, n) in ((kp_ref, vp_ref, lh), (kc_ref, vc_ref, lt), (kn_ref, vn_ref, lh)):
                kd[pl.ds(off, n), :] = kr[0, sl(n), :].astype(BF16)
                vd[pl.ds(off, n), :] = vr[0, sl(n), :].astype(BF16)
                off += n

        for i in range(dil * nblk):
            r, j = divmod(i, nblk)
            q0 = r * lt + j * QBLK
            k0 = r * span + lh + j * QBLK - BAND
            q = qd[pl.ds(q0, QBLK), :]
            k = kd[pl.ds(k0, KBLK), :]
            v1 = jnp.concatenate([vd[pl.ds(k0, KBLK), :], ones], axis=1)
            edge = j == 0 or j == nblk - 1
            if edge:
                pos = t * lt + j * QBLK - BAND + kcol
                valid = (pos >= 0) & (pos < n_pos)
            res = []
            for h in range(2):
                qh = jnp.where(low, q, 0.0) if h == 0 else jnp.where(low, 0.0, q)
                s = lax.dot_general(qh.astype(BF16), k, (((1,), (1,)), ((), ())),
                                    preferred_element_type=F32) + bias_ref[p, h]
                if edge:
                    s = jnp.where(valid, s, NEG)
                m = jnp.max(s, axis=-1, keepdims=True)
                e = jnp.exp(s - m).astype(BF16)
                res.append((jnp.dot(e, v1, preferred_element_type=F32), m))
            (pv0, m0), (pv1, m1) = res
            od[pl.ds(q0, QBLK), :] = jnp.where(low, pv0[:, :LANE], pv1[:, :LANE])
            ld[pl.ds(q0, QBLK), :] = jnp.where(low, pv0[:, LANE:], pv1[:, LANE:])
            md[pl.ds(q0, QBLK), :] = jnp.where(low, m0, m1)

        for r in range(dil):
            sl = pl.ds(r, lt, stride=dil) if dil > 1 else pl.ds(0, lt)
            src = pl.ds(r * lt, lt)
            o_sc[p, sl, :] = od[src, :]
            m_sc[p, sl, :] = md[src, :]
            l_sc[p, sl, :] = ld[src, :]

    n_p = len(DILATED_PATTERNS)
    mx = m_sc[0]
    for p in range(1, n_p):
        mx = jnp.maximum(mx, m_sc[p])
    num = jnp.zeros((ATT_TILE, LANE), F32)
    den = jnp.zeros((ATT_TILE, LANE), F32)
    for p in range(n_p):
        w = jnp.exp(m_sc[p] - mx)
        num += o_sc[p] * w
        den += l_sc[p] * w
    o_ref[0] = (num / den).astype(o_ref.dtype)


def _dilated_attention(att, bias):
    b, s, _ = att.shape
    assert s % ATT_TILE == 0
    nt = s // ATT_TILE
    nh = s // ATT_HALO
    per = ATT_TILE // ATT_HALO
    hp = ATT_WIDTH // LANE
    cur = lambda col0: pl.BlockSpec((1, ATT_TILE, LANE), lambda bi, ti, hi: (bi, ti, col0 + hi))
    prev = lambda col0: pl.BlockSpec(
        (1, ATT_HALO, LANE), lambda bi, ti, hi: (bi, jnp.maximum(ti * per - 1, 0), col0 + hi))
    nxt = lambda col0: pl.BlockSpec(
        (1, ATT_HALO, LANE), lambda bi, ti, hi: (bi, jnp.minimum((ti + 1) * per, nh - 1), col0 + hi))
    n_p = len(DILATED_PATTERNS)
    tile_f32 = pltpu.VMEM((ATT_TILE, LANE), F32)
    span_bf16 = pltpu.VMEM((ATT_TILE + 2 * ATT_HALO, LANE), BF16)
    per_pattern = pltpu.VMEM((n_p, ATT_TILE, LANE), F32)
    return pl.pallas_call(
        functools.partial(_attn_kernel, seq_len=s),
        grid=(b, nt, hp),
        in_specs=[cur(0), prev(hp), cur(hp), nxt(hp), prev(2 * hp), cur(2 * hp), nxt(2 * hp),
                  pl.BlockSpec((n_p, 2, QBLK, KBLK), lambda bi, ti, hi: (0, hi, 0, 0))],
        out_specs=pl.BlockSpec((1, ATT_TILE, LANE), lambda bi, ti, hi: (bi, ti, hi)),
        out_shape=jax.ShapeDtypeStruct((b, s, ATT_WIDTH), BF16),
        scratch_shapes=[tile_f32, span_bf16, span_bf16, tile_f32, tile_f32, tile_f32,
                        per_pattern, per_pattern, per_pattern],
        compiler_params=_params("parallel", "parallel", "parallel"),
        name="dilated_attention",
    )(att, att, att, att, att, att, att, bias)


def _mix(x, att_bias, norm1_g, w_in, conv_w, a_log_fwd, dt_bias_fwd, a_log_bwd, dt_bias_bwd,
         dn_norm_g, w_out, norm2_g, w_router):
    bsz, s, d = x.shape
    xf = x.reshape(bsz * s, d)
    qkv, z, ba, att = _in_proj(xf, norm1_g[None], w_in)
    shp = lambda a: a.reshape(bsz, s, a.shape[-1])
    o_f, o_b = _deltanet(shp(qkv), shp(ba), conv_w, a_log_fwd, dt_bias_fwd, a_log_bwd, dt_bias_bwd)
    att_out = _dilated_attention(shp(att), att_bias)
    flat = lambda a: a.reshape(bsz * s, a.shape[-1])
    return _out_proj(flat(o_f), flat(o_b), z, dn_norm_g[None], flat(att_out), xf, w_out, norm2_g[None],
                     w_router)


def kernel(x_prompt, x_sample, rel_bias, norm1_g, w_in, conv_w, a_log_fwd, dt_bias_fwd, a_log_bwd,
           dt_bias_bwd, dn_norm_g, w_out, norm2_g, w_router, w_gate, w_up, w_down, final_norm_g):
    groups = (x_prompt, x_sample)
    att_bias = _attn_bias_table(rel_bias)
    parts = []
    for x in groups:
        x1, h2, afft = _mix(x, att_bias, norm1_g[0], w_in[0], conv_w[0], a_log_fwd[0], dt_bias_fwd[0],
                                 a_log_bwd[0], dt_bias_bwd[0], dn_norm_g[0], w_out[0], norm2_g[0],
                                 w_router[0])
        n_tok = x1.shape[0]
        cap = CAPACITY_FACTOR * n_tok // N_EXPERTS
        pos, idx, off, gate = _route(afft, cap)
        xe = _sc_gather(h2, idx.reshape(-1))
        parts.append((x, x1, pos, off, gate, xe))
    outs = []
    for x, x1, pos, off, gate, xe in parts:
        ye = _expert_ffn(xe, gate, w_gate[0], w_up[0], w_down[0])
        outs.append(_combine(x1, pos, off, ye, final_norm_g[None]).reshape(x.shape))
    return tuple(outs)
```

```python
import functools
import math

import jax
import jax.numpy as jnp
from jax import lax
from jax.experimental import pallas as pl
from jax.experimental.pallas import tpu as pltpu
from jax.experimental.pallas import tpu_sc as plsc

D_MODEL = 1024
DN_HEADS = 4
DN_HEAD_DIM = 128
DN_WIDTH = DN_HEADS * DN_HEAD_DIM
ATT_HEADS = 8
ATT_HEAD_DIM = 64
ATT_WIDTH = ATT_HEADS * ATT_HEAD_DIM
CONV_K = 3
CHUNK = 64
DILATED_PATTERNS = ((128, 1), (512, 4), (2048, 16))
BAND = 64
NUM_BUCKETS = 32
MAX_DISTANCE = 1024
N_EXPERTS = 16
CAPACITY_FACTOR = 2
EXPERT_D_FF = 2816
EPS = 1e-6
NEG = -1e30
GATE_COLS = 4 * DN_HEADS
LANE = 128
SUBLANE = 8
GATHER_WINDOW = 128
GATHER_SPLIT = 4

F32 = jnp.float32
BF16 = jnp.bfloat16

VMEM_LIMIT = 56 * 1024 * 1024


def _params(*sem):
    return pltpu.CompilerParams(dimension_semantics=sem, vmem_limit_bytes=VMEM_LIMIT)


def _in_proj_kernel(x_ref, g_ref, wqkv_ref, wz_ref, wba_ref, watt_ref,
                    qkv_ref, z_ref, ba_ref, att_ref):
    x = x_ref[...]
    h = x * lax.rsqrt(jnp.mean(x * x, axis=-1, keepdims=True) + EPS) * g_ref[...]
    hb = h.astype(BF16)
    qkv_ref[...] = jnp.dot(hb, wqkv_ref[...], preferred_element_type=F32)
    z_ref[...] = jnp.dot(hb, wz_ref[...], preferred_element_type=F32)
    ba_ref[...] = jnp.dot(hb, wba_ref[...], preferred_element_type=F32)
    att_ref[...] = jnp.dot(hb, watt_ref[...], preferred_element_type=F32)


def _in_proj(x, g, w_in, tm=512):
    n = x.shape[0]
    o_qkv = 3 * DN_WIDTH
    o_z = o_qkv + DN_WIDTH
    o_ba = o_z + GATE_COLS
    wb = w_in.astype(BF16)
    wqkv = wb[:, :o_qkv]
    wz = wb[:, o_qkv:o_z]
    wba = jnp.pad(wb[:, o_z:o_ba], ((0, 0), (0, LANE - GATE_COLS)))
    watt = wb[:, o_ba:]
    full = lambda a: pl.BlockSpec(a.shape, lambda i: (0, 0))
    row = lambda w: pl.BlockSpec((tm, w), lambda i: (i, 0))
    return pl.pallas_call(
        _in_proj_kernel,
        grid=(n // tm,),
        in_specs=[row(D_MODEL), full(g), full(wqkv), full(wz), full(wba), full(watt)],
        out_specs=[row(o_qkv), row(DN_WIDTH), row(LANE), row(3 * ATT_WIDTH)],
        out_shape=[jax.ShapeDtypeStruct((n, o_qkv), F32),
                   jax.ShapeDtypeStruct((n, DN_WIDTH), F32),
                   jax.ShapeDtypeStruct((n, LANE), F32),
                   jax.ShapeDtypeStruct((n, 3 * ATT_WIDTH), F32)],
        compiler_params=_params("parallel"),
        name="in_proj",
    )(x, g, wqkv, wz, wba, watt)


def _split(a):
    hi = a.astype(BF16)
    return hi, (a - hi.astype(F32)).astype(BF16)


def _out_proj_kernel(of_ref, ob_ref, z_ref, gdn_ref, att_ref, x_ref, wdn_ref, watt_ref, g_ref, wr_ref,
                     x1_ref, h2_ref, afft_ref):
    dot = lambda a, b: jnp.dot(a, b, preferred_element_type=F32)
    o = of_ref[...].astype(F32) + ob_ref[...].astype(F32)
    z = z_ref[...]
    gated = []
    for h in range(DN_HEADS):
        hs = slice(h * DN_HEAD_DIM, (h + 1) * DN_HEAD_DIM)
        oh = o[:, hs]
        oh = oh * lax.rsqrt(jnp.mean(oh * oh, axis=-1, keepdims=True) + EPS) * gdn_ref[...]
        zh = z[:, hs]
        gated.append((oh * (zh * jax.nn.sigmoid(zh))).astype(BF16))
    dn = jnp.concatenate(gated, axis=1)
    x1 = x_ref[...] + dot(dn, wdn_ref[...]) + dot(att_ref[...], watt_ref[...])
    x1_ref[...] = x1
    h2 = x1 * lax.rsqrt(jnp.mean(x1 * x1, axis=-1, keepdims=True) + EPS) * g_ref[...]
    rows = h2.shape[0]
    piece = D_MODEL // GATHER_SPLIT
    for q in range(GATHER_SPLIT):
        h2_ref[q] = h2[:, q * piece:(q + 1) * piece]
    hh, hl = _split(h2)
    wh, wl = _split(wr_ref[...])
    logits = dot(hh, wh) + (dot(hh, wl) + dot(hl, wh))
    lane = lax.broadcasted_iota(jnp.int32, logits.shape, 1)
    logits = jnp.where(lane < N_EXPERTS, logits, NEG)
    p = jnp.exp(logits - jnp.max(logits, axis=-1, keepdims=True))
    aff = p / jnp.sum(p, axis=-1, keepdims=True)
    for j in range(rows // LANE):
        afft_ref[:, j, :] = aff[j * LANE:(j + 1) * LANE, :].T[:N_EXPERTS, :]


def _out_proj(o_f, o_b, z, dn_norm_g, att, x, w_out, g, w_router, tm=1024):
    n = x.shape[0]
    wb = w_out.astype(BF16)
    wdn, watt = wb[:DN_WIDTH], wb[DN_WIDTH:]
    wr = jnp.pad(w_router.astype(F32), ((0, 0), (0, LANE - N_EXPERTS)))
    full = lambda a: pl.BlockSpec(a.shape, lambda i: (0, 0))
    row = lambda w: pl.BlockSpec((tm, w), lambda i: (i, 0))
    return pl.pallas_call(
        _out_proj_kernel,
        grid=(n // tm,),
        in_specs=[row(DN_WIDTH), row(DN_WIDTH), row(DN_WIDTH), full(dn_norm_g), row(ATT_WIDTH),
                  row(D_MODEL), full(wdn), full(watt), full(g), full(wr)],
        out_specs=[row(D_MODEL),
                   pl.BlockSpec((GATHER_SPLIT, tm, D_MODEL // GATHER_SPLIT), lambda i: (0, i, 0)),
                   pl.BlockSpec((N_EXPERTS, tm // LANE, LANE), lambda i: (0, i, 0))],
        out_shape=[jax.ShapeDtypeStruct((n, D_MODEL), F32),
                   jax.ShapeDtypeStruct((GATHER_SPLIT, n, D_MODEL // GATHER_SPLIT), F32),
                   jax.ShapeDtypeStruct((N_EXPERTS, n // LANE, LANE), F32)],
        compiler_params=_params("parallel"),
        name="out_proj",
    )(o_f, o_b, z, dn_norm_g, att, x, wdn, watt, g, wr)


def _route_kernel(aff_ref, pos_ref, idx_ref, off_ref, gate_ref, *, cap):
    x = aff_ref[0]
    nb = x.shape[0]
    bits = pltpu.bitcast(x, jnp.int32)
    total = lambda m: jnp.sum(jnp.sum(m, axis=0, keepdims=True), axis=1, keepdims=True)
    dot = lambda a, b: jnp.dot(a, b, preferred_element_type=F32)
    one_hot = lambda m: jnp.where(m, 1.0, 0.0)

    def bit_step(i, prefix):
        cand = prefix | lax.shift_left(jnp.int32(1), 30 - i)
        cnt = total(one_hot(bits >= cand))
        return jnp.where(cnt >= cap, cand, prefix)

    thr = lax.fori_loop(0, 31, bit_step, jnp.zeros((1, 1), jnp.int32))
    gt = bits > thr
    eq = bits == thr
    need = cap - total(one_hot(gt))

    li = lax.broadcasted_iota(jnp.int32, (LANE, LANE), 0)
    lj = lax.broadcasted_iota(jnp.int32, (LANE, LANE), 1)
    lane_incl = one_hot(li <= lj).astype(BF16)
    ones = jnp.ones((LANE, LANE), BF16)
    bi = lax.broadcasted_iota(jnp.int32, (nb, nb), 0)
    bj = lax.broadcasted_iota(jnp.int32, (nb, nb), 1)
    blk_before = one_hot(bj < bi).astype(BF16)

    def ranks(mask):
        m = one_hot(mask).astype(BF16)
        within = dot(m, lane_incl)
        tot = dot(m, ones)
        before = dot(blk_before, tot.astype(BF16))
        return within, tot, before

    w_eq, _, b_eq = ranks(eq)
    sel = gt | (eq & (w_eq + b_eq <= need))
    within, tot, before = ranks(sel)
    pos_ref[0] = jnp.where(sel, within + before - 1.0, -1.0)
    off_ref[0] = before.T[0:SUBLANE, :].astype(jnp.int32)

    within_t = within.T.astype(BF16)
    sel_t = one_hot(sel).T.astype(BF16)
    x_t = x.T
    x_parts = []
    for _ in range(3):
        part = x_t.astype(BF16)
        x_parts.append(part)
        x_t = x_t - part.astype(F32)
    after = before + tot
    blk_id = lax.broadcasted_iota(jnp.int32, (nb, LANE), 0).astype(F32)
    lane_id = lax.broadcasted_iota(jnp.int32, (LANE, LANE), 0).astype(F32)
    for c in range(cap // LANE):
        j = (lax.broadcasted_iota(jnp.int32, (1, LANE), 1) + c * LANE).astype(F32)
        holds = one_hot((before <= j) & (j < after))
        local = j - jnp.sum(holds * before, axis=0, keepdims=True) + 1.0
        blk = jnp.sum(holds * blk_id, axis=0, keepdims=True)
        hb = holds.astype(BF16)
        cnt_in_blk = dot(within_t, hb)
        sel_in_blk = dot(sel_t, hb)
        hit = one_hot((sel_in_blk > 0.5) & (cnt_in_blk == local))
        lane_of = jnp.sum(hit * lane_id, axis=0, keepdims=True)
        idx_ref[0, c:c + 1, :] = (blk * LANE + lane_of).astype(jnp.int32)
        aff_in_blk = dot(x_parts[0], hb) + dot(x_parts[1], hb) + dot(x_parts[2], hb)
        gate_ref[0, c:c + 1, :] = jnp.sum(hit * aff_in_blk, axis=0, keepdims=True)


def _route(aff3, cap):
    e, nb, _ = aff3.shape
    blk = lambda rows: pl.BlockSpec((1, rows, LANE), lambda i: (i, 0, 0))
    return pl.pallas_call(
        functools.partial(_route_kernel, cap=cap),
        grid=(e,),
        in_specs=[blk(nb)],
        out_specs=[blk(nb), blk(cap // LANE), pl.BlockSpec((1, SUBLANE, nb), lambda i: (i, 0, 0)),
                   blk(cap // LANE)],
        out_shape=[jax.ShapeDtypeStruct((e, nb, LANE), F32),
                   jax.ShapeDtypeStruct((e, cap // LANE, LANE), jnp.int32),
                   jax.ShapeDtypeStruct((e, SUBLANE, nb), jnp.int32),
                   jax.ShapeDtypeStruct((e, cap // LANE, LANE), F32)],
        compiler_params=_params("parallel"),
        name="route",
    )(aff3)


def _sc_gather(x, idx):
    pieces, n, d = x.shape
    x = x.reshape(pieces * n, d)
    idx = (jnp.arange(pieces, dtype=idx.dtype)[:, None] * n + idx[None, :]).reshape(-1)
    m = idx.shape[0]
    mesh = plsc.VectorSubcoreMesh(core_axis_name="c", subcore_axis_name="s")

    @pl.kernel(out_type=jax.ShapeDtypeStruct((m, d), x.dtype), mesh=mesh, scratch_types=[])
    def gather_kernel(x_hbm, i_hbm, o_hbm):
        def body(i_vmem, o_vmem):
            pltpu.sync_copy(x_hbm.at[i_vmem.at[0]], o_vmem)

        pltpu.emit_pipeline(
            body,
            grid=(m // GATHER_WINDOW,),
            in_specs=[pl.BlockSpec((1, GATHER_WINDOW), lambda i: (0, i))],
            out_specs=[pl.BlockSpec((GATHER_WINDOW, d), lambda i: (i, 0))],
            core_axis_name=("c", "s"),
            dimension_semantics=(pltpu.PARALLEL,),
        )(i_hbm, o_hbm)

    return gather_kernel(x, idx.reshape(1, m)).reshape(pieces, m // pieces, d)


def _ffn_kernel(x_ref, gate_ref, wg_ref, wu_ref, wd_ref, ye_ref, xe_ref, acc_ref):
    f = pl.program_id(2)
    pieces, _, piece = x_ref.shape

    @pl.when(f == 0)
    def _():
        acc_ref[...] = jnp.zeros_like(acc_ref)
        for q in range(pieces):
            xe_ref[:, q * piece:(q + 1) * piece] = x_ref[q].astype(BF16)

    xe = xe_ref[...]
    a = jnp.dot(xe, wg_ref[0].astype(BF16), preferred_element_type=F32)
    b = jnp.dot(xe, wu_ref[0].astype(BF16), preferred_element_type=F32)
    h = (a * jax.nn.sigmoid(a) * b).astype(BF16)
    acc_ref[...] += jnp.dot(h, wd_ref[0].astype(BF16), preferred_element_type=F32)

    @pl.when(f == pl.num_programs(2) - 1)
    def _():
        for c in range(gate_ref.shape[1]):
            rows = slice(c * LANE, (c + 1) * LANE)
            gate = jnp.broadcast_to(gate_ref[0, c:c + 1, :], (LANE, LANE)).T
            ye_ref[0, rows, :] = (acc_ref[rows, :] * jnp.concatenate([gate] * (acc_ref.shape[1] // LANE), axis=1)
                                  ).astype(ye_ref.dtype)


def _expert_ffn(xe, gate, w_gate, w_up, w_down, tm=2048, tf=256):
    e = w_gate.shape[0]
    pieces, rows, piece = xe.shape
    cap, d = rows // e, piece * pieces
    dff = w_gate.shape[-1]
    tiles = cap // tm
    return pl.pallas_call(
        _ffn_kernel,
        grid=(e, tiles, dff // tf),
        in_specs=[pl.BlockSpec((pieces, tm, piece), lambda i, r, f: (0, i * tiles + r, 0)),
                  pl.BlockSpec((1, tm // LANE, LANE), lambda i, r, f: (i, r, 0)),
                  pl.BlockSpec((1, d, tf), lambda i, r, f: (i, 0, f)),
                  pl.BlockSpec((1, d, tf), lambda i, r, f: (i, 0, f)),
                  pl.BlockSpec((1, tf, d), lambda i, r, f: (i, f, 0))],
        out_specs=pl.BlockSpec((1, tm, d), lambda i, r, f: (i, r, 0)),
        out_shape=jax.ShapeDtypeStruct((e, cap, d), BF16),
        scratch_shapes=[pltpu.VMEM((tm, d), BF16), pltpu.VMEM((tm, d), F32)],
        compiler_params=_params("parallel", "parallel", "arbitrary"),
        name="expert_ffn",
    )(xe, gate, w_gate, w_up, w_down)


COMBINE_TAIL = 16


def _combine_kernel(r0_ref, spill_ref, x1_ref, pos_ref, g_ref, *rest):
    ye_refs, y_ref = rest[:N_EXPERTS], rest[N_EXPERTS]
    i = pl.program_id(0)
    nb = pl.num_programs(0)
    t, d = x1_ref.shape
    pos_t = jnp.concatenate([pos_ref[:, 0, :], jnp.zeros((LANE - N_EXPERTS, LANE), F32)], axis=0).T
    col = lax.broadcasted_iota(jnp.int32, (t, LANE), 1)

    def gathered(rows, shift):
        out = jnp.zeros((t, d), F32)
        for e0 in range(0, N_EXPERTS, 2):
            picks = []
            for e in (e0, e0 + 1):
                first = r0_ref[e * nb + i] + shift
                slot = jnp.broadcast_to(pos_t[:, e:e + 1], (t, LANE))
                picks.append(jnp.where(slot == (col + first).astype(F32), 1.0, 0.0).astype(BF16))
            out = out + jnp.dot(jnp.concatenate(picks, axis=1), jnp.concatenate([rows(e0), rows(e0 + 1)], axis=0),
                                preferred_element_type=F32)
        return out

    def finish(x):
        y_ref[...] = x * lax.rsqrt(jnp.mean(x * x, axis=-1, keepdims=True) + EPS) * g_ref[...]

    acc = x1_ref[...] + gathered(lambda e: ye_refs[e][:LANE, :], 0)
    finish(acc)

    @pl.when(spill_ref[i] != 0)
    def _():
        pad = jnp.zeros((LANE - COMBINE_TAIL, d), BF16)
        finish(acc + gathered(lambda e: jnp.concatenate([ye_refs[e][LANE:, :], pad], axis=0), LANE))


def _combine(x1, pos3, off, ye, final_g):
    n, d = x1.shape
    nb = n // LANE
    cap = ye.shape[1]
    first = off[:, 0, :]
    end = jnp.concatenate([first[:, 1:], jnp.full((N_EXPERTS, 1), cap, first.dtype)], axis=1)
    r0 = jnp.minimum(first // 16 * 16, cap - LANE - COMBINE_TAIL)
    spill = jnp.any(end > r0 + LANE, axis=0).astype(jnp.int32)
    pos4 = pos3.reshape(N_EXPERTS, nb, 1, LANE)
    window = lambda e, rows, shift: pl.BlockSpec(
        (None, pl.Element(rows), pl.Element(d)),
        lambda i, r0_ref, spill_ref: (e, pl.multiple_of(r0_ref[e * nb + i] + shift, 16), 0))
    grid_spec = pltpu.PrefetchScalarGridSpec(
        num_scalar_prefetch=2,
        grid=(nb,),
        in_specs=[pl.BlockSpec((LANE, d), lambda i, r, s: (i, 0)),
                  pl.BlockSpec((N_EXPERTS, None, 1, LANE), lambda i, r, s: (0, i, 0, 0)),
                  pl.BlockSpec((1, d), lambda i, r, s: (0, 0))]
                 + [window(e, LANE + COMBINE_TAIL, 0) for e in range(N_EXPERTS)],
        out_specs=pl.BlockSpec((LANE, d), lambda i, r, s: (i, 0)),
    )
    return pl.pallas_call(
        _combine_kernel,
        grid_spec=grid_spec,
        out_shape=jax.ShapeDtypeStruct((n, d), F32),
        compiler_params=_params("parallel"),
        name="moe_combine",
    )(r0.reshape(-1).astype(jnp.int32), spill, x1, pos4, final_g, *([ye] * N_EXPERTS))


def _dn_prep_kernel(x_ref, xp_ref, xn_ref, w_ref, q_ref, k_ref, v_ref, kt_ref):
    t = pl.program_id(1)
    nt = pl.num_programs(1)
    x = x_ref[0]
    rows = x.shape[0]
    row = lax.broadcasted_iota(jnp.int32, (rows, 1), 0)
    before = jnp.where(t > 0, xp_ref[0, SUBLANE - 1:SUBLANE, :], 0.0)
    after = jnp.where(t < nt - 1, xn_ref[0, 0:1, :], 0.0)
    x_prev = jnp.where(row == 0, before, pltpu.roll(x, 1, axis=0))
    x_next = jnp.where(row == rows - 1, after, pltpu.roll(x, rows - 1, axis=0))
    y = w_ref[0:1, :] * x_prev + w_ref[1:2, :] * x + w_ref[2:3, :] * x_next
    y = y * jax.nn.sigmoid(y)
    for h in range(DN_HEADS):
        for part, ref, scale in ((0, q_ref, DN_HEAD_DIM ** -0.5), (1, k_ref, 1.0)):
            c0 = part * DN_WIDTH + h * DN_HEAD_DIM
            a = y[:, c0:c0 + DN_HEAD_DIM]
            inv = lax.rsqrt(jnp.sum(a * a, axis=-1, keepdims=True) + EPS) * scale
            a = a * inv
            ref[0, :, h * DN_HEAD_DIM:(h + 1) * DN_HEAD_DIM] = a.astype(ref.dtype)
            if part == 1:
                for j in range(rows // CHUNK):
                    kt_ref[0, j, h * DN_HEAD_DIM:(h + 1) * DN_HEAD_DIM, :] = (
                        a[j * CHUNK:(j + 1) * CHUNK, :].T.astype(kt_ref.dtype))
    v_ref[0] = y[:, 2 * DN_WIDTH:].astype(v_ref.dtype)


def _dn_prep(qkv, conv_w, tile=512):
    b, s, c = qkv.shape
    tile = min(tile, s)
    nt = s // tile
    per = tile // SUBLANE
    nsub = s // SUBLANE
    out = jax.ShapeDtypeStruct((b, s, DN_WIDTH), BF16)
    ospec = pl.BlockSpec((1, tile, DN_WIDTH), lambda bi, ti: (bi, ti, 0))
    return pl.pallas_call(
        _dn_prep_kernel,
        grid=(b, nt),
        in_specs=[pl.BlockSpec((1, tile, c), lambda bi, ti: (bi, ti, 0)),
                  pl.BlockSpec((1, SUBLANE, c), lambda bi, ti: (bi, jnp.maximum(ti * per - 1, 0), 0)),
                  pl.BlockSpec((1, SUBLANE, c), lambda bi, ti: (bi, jnp.minimum((ti + 1) * per, nsub - 1), 0)),
                  pl.BlockSpec((CONV_K, c), lambda bi, ti: (0, 0))],
        out_specs=[ospec, ospec, ospec,
                   pl.BlockSpec((1, tile // CHUNK, DN_WIDTH, CHUNK), lambda bi, ti: (bi, ti, 0, 0))],
        out_shape=[out, out, out, jax.ShapeDtypeStruct((b, s // CHUNK, DN_WIDTH, CHUNK), BF16)],
        compiler_params=_params("parallel", "parallel"),
        name="dn_prep",
    )(qkv, qkv, qkv, conv_w)


def _lane_bcast(a, col, width=LANE):
    return jnp.broadcast_to(a[:, col:col + 1], (a.shape[0], width))


DN_TILE = 1024
INV_STEPS = 5
assert 2 ** (INV_STEPS + 1) == CHUNK
DN_UNROLL = 64
DN_BUILD_UNROLL = 4


def _dn_scan_kernel(q_ref, k_ref, v_ref, kt_ref, ba_ref, alog_ref, dtb_ref, o_ref,
                    s_ref, xbuf, tbuf, abuf, rbuf, uwbuf, qgbuf, kdbuf, lbuf, nbuf, obuf, ecbuf, *, reverse):
    @pl.when(pl.program_id(1) == 0)
    def _():
        s_ref[...] = jnp.zeros_like(s_ref)

    tile = q_ref.shape[1]
    nc = tile // CHUNK
    n_prob = nc * DN_HEADS
    unroll = min(DN_UNROLL, n_prob)
    beta_col = DN_HEADS if reverse else 0
    a_col = 2 * DN_HEADS + (DN_HEADS if reverse else 0)
    ri = lax.broadcasted_iota(jnp.int32, (CHUNK, CHUNK), 0)
    ci = lax.broadcasted_iota(jnp.int32, (CHUNK, CHUNK), 1)
    tri = (ri <= ci) if reverse else (ri >= ci)
    strict = (ri < ci) if reverse else (ri > ci)
    eye = jnp.where(ri == ci, 1.0, 0.0)
    row = lax.broadcasted_iota(jnp.int32, (CHUNK, LANE), 0)
    a_scale = jnp.exp(alog_ref[...])
    dtb = dtb_ref[...]
    dot = lambda a, b: jnp.dot(a, b, preferred_element_type=F32)

    def build(c, carry):
        rows = pl.ds(pl.multiple_of(c * CHUNK, CHUNK), CHUNK)
        raw = ba_ref[0, rows, :]
        beta_all = jax.nn.sigmoid(raw)
        xs = raw + dtb
        g = -a_scale * (jnp.maximum(xs, 0.0) + jnp.log(1.0 + jnp.exp(-jnp.abs(xs))))
        gs = g
        sh = 1
        while sh < CHUNK:
            if reverse:
                gs = gs + jnp.where(row < CHUNK - sh, pltpu.roll(gs, CHUNK - sh, axis=0), 0.0)
            else:
                gs = gs + jnp.where(row >= sh, pltpu.roll(gs, sh, axis=0), 0.0)
            sh *= 2
        gs_t = gs.T
        g_last = gs[0:1, :] if reverse else gs[CHUNK - 1:CHUNK, :]
        eg_all = jnp.exp(gs)
        ek_t = jnp.exp(g_last.T - gs_t)
        ecbuf[c] = jnp.broadcast_to(jnp.exp(g_last), (SUBLANE, LANE))
        for h in range(DN_HEADS):
            p = c * DN_HEADS + h
            hs = slice(h * DN_HEAD_DIM, (h + 1) * DN_HEAD_DIM)
            q = q_ref[0, rows, hs].astype(F32)
            k = k_ref[0, rows, hs]
            v = v_ref[0, rows, hs].astype(F32)
            beta = _lane_bcast(beta_all, beta_col + h)
            eg = _lane_bcast(eg_all, a_col + h)
            diff = _lane_bcast(gs, a_col + h, CHUNK) - gs_t[a_col + h:a_col + h + 1, :]
            decay = jnp.where(tri, jnp.exp(jnp.where(tri, diff, 0.0)), 0.0)
            kb = k.astype(F32) * beta
            kq = lax.dot_general(jnp.concatenate([kb, q], axis=0).astype(BF16), k,
                                 (((1,), (1,)), ((), ())), preferred_element_type=F32)
            lower = jnp.where(strict, kq[:CHUNK] * decay, 0.0)
            xbuf[p] = (-lower).astype(BF16)
            tbuf[p] = eye - lower
            abuf[p] = (kq[CHUNK:] * decay).astype(BF16)
            rbuf[p] = jnp.concatenate([v * beta, kb * eg], axis=1).astype(BF16)
            qgbuf[p] = q * eg
            kdbuf[p] = (kt_ref[0, c, hs, :].astype(F32)
                        * ek_t[a_col + h:a_col + h + 1, :]).astype(BF16)
        return carry

    lax.fori_loop(0, nc, build, 0, unroll=min(DN_BUILD_UNROLL, nc))

    for step in range(INV_STEPS + 1):
        def double(p, carry, first=(step == 0), last=(step == INV_STEPS)):
            x = xbuf[p]
            if not first:
                t = tbuf[p]
                tbuf[p] = t + dot(t.astype(BF16), x)
            if not last:
                xbuf[p] = dot(x, x).astype(BF16)
            return carry
        lax.fori_loop(0, n_prob, double, 0, unroll=unroll)

    def solve(p, carry):
        uwbuf[p] = dot(tbuf[p].astype(BF16), rbuf[p]).astype(BF16)
        return carry

    lax.fori_loop(0, n_prob, solve, 0, unroll=unroll)

    def fold(p, carry):
        uw = uwbuf[p]
        a_uw = dot(abuf[p], uw)
        k_uw = dot(kdbuf[p], uw)
        obuf[p] = a_uw[:, :DN_HEAD_DIM]
        nbuf[p] = k_uw[:, :DN_HEAD_DIM]
        lbuf[p, :DN_HEAD_DIM, :] = k_uw[:, DN_HEAD_DIM:].astype(BF16)
        lbuf[p, DN_HEAD_DIM:, :] = (qgbuf[p] - a_uw[:, DN_HEAD_DIM:]).astype(BF16)
        return carry

    lax.fori_loop(0, n_prob, fold, 0, unroll=unroll)

    def scan(jc, carry):
        c = (nc - 1 - jc) if reverse else jc
        rows = pl.ds(pl.multiple_of(c * CHUNK, CHUNK), CHUNK)
        ec_all = ecbuf[c]
        for h in range(DN_HEADS):
            p = c * DN_HEADS + h
            state = s_ref[h]
            r = dot(lbuf[p], state.astype(BF16))
            s_ref[h] = state * _lane_bcast(ec_all[0:1], a_col + h) - r[:DN_HEAD_DIM] + nbuf[p]
            o = r[DN_HEAD_DIM:] + obuf[p]
            o_ref[0, rows, h * DN_HEAD_DIM:(h + 1) * DN_HEAD_DIM] = o.astype(o_ref.dtype)
        return carry

    lax.fori_loop(0, nc, scan, 0)


def _dn_scan(q, k, v, kt, ba, a_log_row, dt_bias_row, reverse, tile=DN_TILE):
    b, s, _ = q.shape
    tile = min(tile, s)
    nt = s // tile
    nc = tile // CHUNK
    n_prob = nc * DN_HEADS
    tmap = (lambda bi, ti: (bi, nt - 1 - ti, 0)) if reverse else (lambda bi, ti: (bi, ti, 0))
    wide = pl.BlockSpec((1, tile, DN_WIDTH), tmap)
    par = pl.BlockSpec((1, LANE), lambda bi, ti: (0, 0))
    return pl.pallas_call(
        functools.partial(_dn_scan_kernel, reverse=reverse),
        grid=(b, nt),
        in_specs=[wide, wide, wide,
                  pl.BlockSpec((1, nc, DN_WIDTH, CHUNK), lambda bi, ti: tmap(bi, ti) + (0,)),
                  pl.BlockSpec((1, tile, LANE), tmap), par, par],
        out_specs=wide,
        out_shape=jax.ShapeDtypeStruct((b, s, DN_WIDTH), BF16),
        scratch_shapes=[pltpu.VMEM((DN_HEADS, DN_HEAD_DIM, DN_HEAD_DIM), F32),
                        pltpu.VMEM((n_prob, CHUNK, CHUNK), BF16),
                        pltpu.VMEM((n_prob, CHUNK, CHUNK), F32),
                        pltpu.VMEM((n_prob, CHUNK, CHUNK), BF16),
                        pltpu.VMEM((n_prob, CHUNK, 2 * DN_HEAD_DIM), BF16),
                        pltpu.VMEM((n_prob, CHUNK, 2 * DN_HEAD_DIM), BF16),
                        pltpu.VMEM((n_prob, CHUNK, DN_HEAD_DIM), F32),
                        pltpu.VMEM((n_prob, DN_HEAD_DIM, CHUNK), BF16),
                        pltpu.VMEM((n_prob, DN_HEAD_DIM + CHUNK, DN_HEAD_DIM), BF16),
                        pltpu.VMEM((n_prob, DN_HEAD_DIM, DN_HEAD_DIM), F32),
                        pltpu.VMEM((n_prob, CHUNK, DN_HEAD_DIM), F32),
                        pltpu.VMEM((nc, SUBLANE, LANE), F32)],
        compiler_params=_params("parallel", "arbitrary"),
        name="dn_scan_bwd" if reverse else "dn_scan_fwd",
    )(q, k, v, kt, ba, a_log_row, dt_bias_row)


def _gate_rows(a_log_fwd, a_log_bwd, dt_bias_fwd, dt_bias_bwd):
    pad = lambda f, b: jnp.pad(jnp.concatenate([f, b]).astype(F32), (2 * DN_HEADS, LANE - 4 * DN_HEADS))[None]
    return pad(a_log_fwd, a_log_bwd), pad(dt_bias_fwd, dt_bias_bwd)


def _deltanet(qkv, ba, conv_w, a_log_fwd, dt_bias_fwd, a_log_bwd, dt_bias_bwd):
    q, k, v, kt = _dn_prep(qkv, conv_w)
    alog, dtb = _gate_rows(a_log_fwd, a_log_bwd, dt_bias_fwd, dt_bias_bwd)
    o_f = _dn_scan(q, k, v, kt, ba, alog, dtb, reverse=False)
    o_b = _dn_scan(q, k, v, kt, ba, alog, dtb, reverse=True)
    return o_f, o_b


def _t5_bucket(rel):
    nb = NUM_BUCKETS // 2
    ret = jnp.where(rel > 0, nb, 0)
    n = jnp.abs(rel)
    max_exact = nb // 2
    nf = jnp.maximum(n, 1).astype(F32)
    large = max_exact + (jnp.log(nf / max_exact) / math.log(MAX_DISTANCE / max_exact)
                         * (nb - max_exact)).astype(jnp.int32)
    large = jnp.minimum(large, nb - 1)
    return ret + jnp.where(n < max_exact, n, large)


ATT_TILE = 2048
ATT_HALO = BAND * max(d for _, d in DILATED_PATTERNS)
QBLK = 2 * BAND
KBLK = QBLK + 2 * BAND


def _attn_bias_table(rel_bias):
    delta = (jnp.arange(KBLK) - BAND)[None, :] - jnp.arange(QBLK)[:, None]
    tabs = []
    for window, dil in DILATED_PATTERNS:
        half = window // (2 * dil)
        onehot = (_t5_bucket(delta * dil)[..., None] == jnp.arange(NUM_BUCKETS)).astype(F32)
        bias = jnp.einsum('qkb,bh->hqk', onehot, rel_bias.astype(F32),
                          precision=lax.Precision.HIGHEST)
        tabs.append(jnp.where((jnp.abs(delta) <= half)[None], bias, NEG))
    return jnp.stack(tabs)


def _attn_kernel(q_ref, kp_ref, kc_ref, kn_ref, vp_ref, vc_ref, vn_ref, bias_ref, o_ref,
                 qd_a, qd_b, kf, vf, kd, vd, od, md, ld, o_sc, m_sc, l_sc, *, seq_len):
    t = pl.program_id(1)
    lane = lax.broadcasted_iota(jnp.int32, (QBLK, LANE), 1)
    low = lane < ATT_HEAD_DIM
    scale = ATT_HEAD_DIM ** -0.5
    kcol = lax.broadcasted_iota(jnp.int32, (1, KBLK), 1)
    ones = jnp.ones((KBLK, LANE), BF16)

    for p, (window, dil) in enumerate(DILATED_PATTERNS):
        lt = ATT_TILE // dil
        lh = ATT_HALO // dil
        span = lt + 2 * lh
        nblk = lt // QBLK
        n_pos = seq_len // dil
        qd, q_prev = (qd_a, qd_b) if p % 2 == 0 else (qd_b, qd_a)
        prev_dil = DILATED_PATTERNS[p - 1][1] if p > 0 else 1
        nested = p > 0 and prev_dil > 1 and dil % prev_dil == 0
        step = dil // prev_dil
        keep_f32 = p + 1 < len(DILATED_PATTERNS) and dil > 1 and DILATED_PATTERNS[p + 1][1] % dil == 0
        for r in range(dil):
            if nested:
                rp, a = r % prev_dil, r // prev_dil
                lt_p, span_p = ATT_TILE // prev_dil, (ATT_TILE + 2 * ATT_HALO) // prev_dil
                qd[pl.ds(r * lt, lt), :] = q_prev[pl.ds(rp * lt_p + a, lt, stride=step), :]
                src = pl.ds(rp * span_p + a, span, stride=step)
                kd[pl.ds(r * span, span), :] = kf[src, :].astype(BF16)
                vd[pl.ds(r * span, span), :] = vf[src, :].astype(BF16)
                continue
            sl = lambda n: pl.ds(r, n, stride=dil) if dil > 1 else pl.ds(0, n)
            qd[pl.ds(r * lt, lt), :] = q_ref[0, sl(lt), :] * scale
            off = r * span
            for (kr, vr, n) in ((kp_ref, vp_ref, lh), (kc_ref, vc_ref, lt), (kn_ref, vn_ref, lh)):
                kx, vx = kr[0, sl(n), :], vr[0, sl(n), :]
                if keep_f32:
                    kf[pl.ds(off, n), :] = kx
                    vf[pl.ds(off, n), :] = vx
                kd[pl.ds(off, n), :] = kx.astype(BF16)
                vd[pl.ds(off, n), :] = vx.astype(BF16)
                off += n


        for i in range(dil * nblk):
            r, j = divmod(i, nblk)
            q0 = r * lt + j * QBLK
            k0 = r * span + lh + j * QBLK - BAND
            q = qd[pl.ds(q0, QBLK), :]
            k = kd[pl.ds(k0, KBLK), :]
            v1 = jnp.concatenate([vd[pl.ds(k0, KBLK), :], ones], axis=1)
            edge = j == 0 or j == nblk - 1
            if edge:
                pos = t * lt + j * QBLK - BAND + kcol
                valid = (pos >= 0) & (pos < n_pos)
            res = []
            for h in range(2):
                qh = jnp.where(low, q, 0.0) if h == 0 else jnp.where(low, 0.0, q)
                s = lax.dot_general(qh.astype(BF16), k, (((1,), (1,)), ((), ())),
                                    preferred_element_type=F32) + bias_ref[p, h]
                if edge:
                    s = jnp.where(valid, s, NEG)
                m = jnp.max(s, axis=-1, keepdims=True)
                e = jnp.exp(s - m).astype(BF16)
                res.append((jnp.dot(e, v1, preferred_element_type=F32), m))
            (pv0, m0), (pv1, m1) = res
            od[pl.ds(q0, QBLK), :] = jnp.where(low, pv0[:, :LANE], pv1[:, :LANE])
            ld[pl.ds(q0, QBLK), :] = jnp.where(low, pv0[:, LANE:], pv1[:, LANE:])
            md[pl.ds(q0, QBLK), :] = jnp.where(low, m0, m1)

        for r in range(dil):
            sl = pl.ds(r, lt, stride=dil) if dil > 1 else pl.ds(0, lt)
            src = pl.ds(r * lt, lt)
            o_sc[p, sl, :] = od[src, :]
            m_sc[p, sl, :] = md[src, :]
            l_sc[p, sl, :] = ld[src, :]

    n_p = len(DILATED_PATTERNS)
    mx = m_sc[0]
    for p in range(1, n_p):
        mx = jnp.maximum(mx, m_sc[p])
    num = jnp.zeros((ATT_TILE, LANE), F32)
    den = jnp.zeros((ATT_TILE, LANE), F32)
    for p in range(n_p):
        w = jnp.exp(m_sc[p] - mx)
        num += o_sc[p] * w
        den += l_sc[p] * w
    o_ref[0] = (num / den).astype(o_ref.dtype)


def _dilated_attention(att, bias):
    b, s, _ = att.shape
    assert s % ATT_TILE == 0
    nt = s // ATT_TILE
    nh = s // ATT_HALO
    per = ATT_TILE // ATT_HALO
    hp = ATT_WIDTH // LANE
    cur = lambda col0: pl.BlockSpec((1, ATT_TILE, LANE), lambda bi, ti, hi: (bi, ti, col0 + hi))
    prev = lambda col0: pl.BlockSpec(
        (1, ATT_HALO, LANE), lambda bi, ti, hi: (bi, jnp.maximum(ti * per - 1, 0), col0 + hi))
    nxt = lambda col0: pl.BlockSpec(
        (1, ATT_HALO, LANE), lambda bi, ti, hi: (bi, jnp.minimum((ti + 1) * per, nh - 1), col0 + hi))
    n_p = len(DILATED_PATTERNS)
    tile_f32 = pltpu.VMEM((ATT_TILE, LANE), F32)
    span_bf16 = pltpu.VMEM((ATT_TILE + 2 * ATT_HALO, LANE), BF16)
    span_f32 = pltpu.VMEM((ATT_TILE + 2 * ATT_HALO, LANE), F32)
    per_pattern = pltpu.VMEM((n_p, ATT_TILE, LANE), F32)
    return pl.pallas_call(
        functools.partial(_attn_kernel, seq_len=s),
        grid=(b, nt, hp),
        in_specs=[cur(0), prev(hp), cur(hp), nxt(hp), prev(2 * hp), cur(2 * hp), nxt(2 * hp),
                  pl.BlockSpec((n_p, 2, QBLK, KBLK), lambda bi, ti, hi: (0, hi, 0, 0))],
        out_specs=pl.BlockSpec((1, ATT_TILE, LANE), lambda bi, ti, hi: (bi, ti, hi)),
        out_shape=jax.ShapeDtypeStruct((b, s, ATT_WIDTH), BF16),
        scratch_shapes=[tile_f32, tile_f32, span_f32, span_f32, span_bf16, span_bf16, tile_f32, tile_f32, tile_f32,
                        per_pattern, per_pattern, per_pattern],
        compiler_params=_params("parallel", "parallel", "parallel"),
        name="dilated_attention",
    )(att, att, att, att, att, att, att, bias)


def _mix(x, att_bias, norm1_g, w_in, conv_w, a_log_fwd, dt_bias_fwd, a_log_bwd, dt_bias_bwd,
         dn_norm_g, w_out, norm2_g, w_router):
    bsz, s, d = x.shape
    xf = x.reshape(bsz * s, d)
    qkv, z, ba, att = _in_proj(xf, norm1_g[None], w_in)
    shp = lambda a: a.reshape(bsz, s, a.shape[-1])
    o_f, o_b = _deltanet(shp(qkv), shp(ba), conv_w, a_log_fwd, dt_bias_fwd, a_log_bwd, dt_bias_bwd)
    att_out = _dilated_attention(shp(att), att_bias)
    flat = lambda a: a.reshape(bsz * s, a.shape[-1])
    return _out_proj(flat(o_f), flat(o_b), z, dn_norm_g[None], flat(att_out), xf, w_out, norm2_g[None],
                     w_router)


def kernel(x_prompt, x_sample, rel_bias, norm1_g, w_in, conv_w, a_log_fwd, dt_bias_fwd, a_log_bwd,
           dt_bias_bwd, dn_norm_g, w_out, norm2_g, w_router, w_gate, w_up, w_down, final_norm_g):
    groups = (x_prompt, x_sample)
    att_bias = _attn_bias_table(rel_bias)
    parts = []
    for x in groups:
        x1, h2, afft = _mix(x, att_bias, norm1_g[0], w_in[0], conv_w[0], a_log_fwd[0], dt_bias_fwd[0],
                                 a_log_bwd[0], dt_bias_bwd[0], dn_norm_g[0], w_out[0], norm2_g[0],
                                 w_router[0])
        n_tok = x1.shape[0]
        cap = CAPACITY_FACTOR * n_tok // N_EXPERTS
        pos, idx, off, gate = _route(afft, cap)
        xe = _sc_gather(h2, idx.reshape(-1))
        parts.append((x, x1, pos, off, gate, xe))
    outs = []
    xes = [p[-1] for p in parts]
    for k, (x, x1, pos, off, gate, _) in enumerate(parts):
        ye = _expert_ffn(xes[k], gate, w_gate[0], w_up[0], w_down[0])
        if k + 1 < len(parts):
            xes[k + 1], ye = lax.optimization_barrier((xes[k + 1], ye))
        outs.append(_combine(x1, pos, off, ye, final_norm_g[None]).reshape(x.shape))
    return tuple(outs)
```

```python
import functools
import math

import jax
import jax.numpy as jnp
from jax import lax
from jax.experimental import pallas as pl
from jax.experimental.pallas import tpu as pltpu
from jax.experimental.pallas import tpu_sc as plsc

D_MODEL = 1024
DN_HEADS = 4
DN_HEAD_DIM = 128
DN_WIDTH = DN_HEADS * DN_HEAD_DIM
ATT_HEADS = 8
ATT_HEAD_DIM = 64
ATT_WIDTH = ATT_HEADS * ATT_HEAD_DIM
CONV_K = 3
CHUNK = 64
DILATED_PATTERNS = ((128, 1), (512, 4), (2048, 16))
BAND = 64
NUM_BUCKETS = 32
MAX_DISTANCE = 1024
N_EXPERTS = 16
CAPACITY_FACTOR = 2
EXPERT_D_FF = 2816
EPS = 1e-6
NEG = -1e30
GATE_COLS = 4 * DN_HEADS
LANE = 128
SUBLANE = 8
GATHER_WINDOW = 128
GATHER_SPLIT = 4

F32 = jnp.float32
BF16 = jnp.bfloat16

VMEM_LIMIT = 56 * 1024 * 1024


def _params(*sem):
    return pltpu.CompilerParams(dimension_semantics=sem, vmem_limit_bytes=VMEM_LIMIT)


def _in_proj_kernel(x_ref, g_ref, wqkv_ref, wz_ref, wba_ref, watt_ref,
                    qkv_ref, z_ref, ba_ref, att_ref):
    x = x_ref[...]
    h = x * lax.rsqrt(jnp.mean(x * x, axis=-1, keepdims=True) + EPS) * g_ref[...]
    hb = h.astype(BF16)
    qkv_ref[...] = jnp.dot(hb, wqkv_ref[...], preferred_element_type=F32)
    z_ref[...] = jnp.dot(hb, wz_ref[...], preferred_element_type=F32)
    ba_ref[...] = jnp.dot(hb, wba_ref[...], preferred_element_type=F32)
    att_ref[...] = jnp.dot(hb, watt_ref[...], preferred_element_type=F32)


def _in_proj(x, g, w_in, tm=512):
    n = x.shape[0]
    o_qkv = 3 * DN_WIDTH
    o_z = o_qkv + DN_WIDTH
    o_ba = o_z + GATE_COLS
    wb = w_in.astype(BF16)
    wqkv = wb[:, :o_qkv]
    wz = wb[:, o_qkv:o_z]
    wba = jnp.pad(wb[:, o_z:o_ba], ((0, 0), (0, LANE - GATE_COLS)))
    watt = wb[:, o_ba:]
    full = lambda a: pl.BlockSpec(a.shape, lambda i: (0, 0))
    row = lambda w: pl.BlockSpec((tm, w), lambda i: (i, 0))
    return pl.pallas_call(
        _in_proj_kernel,
        grid=(n // tm,),
        in_specs=[row(D_MODEL), full(g), full(wqkv), full(wz), full(wba), full(watt)],
        out_specs=[row(o_qkv), row(DN_WIDTH), row(LANE), row(3 * ATT_WIDTH)],
        out_shape=[jax.ShapeDtypeStruct((n, o_qkv), F32),
                   jax.ShapeDtypeStruct((n, DN_WIDTH), F32),
                   jax.ShapeDtypeStruct((n, LANE), F32),
                   jax.ShapeDtypeStruct((n, 3 * ATT_WIDTH), F32)],
        compiler_params=_params("parallel"),
        name="in_proj",
    )(x, g, wqkv, wz, wba, watt)


def _split(a):
    hi = a.astype(BF16)
    return hi, (a - hi.astype(F32)).astype(BF16)


def _out_proj_kernel(of_ref, ob_ref, z_ref, gdn_ref, att_ref, x_ref, wdn_ref, watt_ref, g_ref, wr_ref,
                     x1_ref, h2_ref, afft_ref):
    dot = lambda a, b: jnp.dot(a, b, preferred_element_type=F32)
    o = of_ref[...].astype(F32) + ob_ref[...].astype(F32)
    z = z_ref[...]
    gated = []
    for h in range(DN_HEADS):
        hs = slice(h * DN_HEAD_DIM, (h + 1) * DN_HEAD_DIM)
        oh = o[:, hs]
        oh = oh * lax.rsqrt(jnp.mean(oh * oh, axis=-1, keepdims=True) + EPS) * gdn_ref[...]
        zh = z[:, hs]
        gated.append((oh * (zh * jax.nn.sigmoid(zh))).astype(BF16))
    dn = jnp.concatenate(gated, axis=1)
    x1 = x_ref[...] + dot(dn, wdn_ref[...]) + dot(att_ref[...], watt_ref[...])
    x1_ref[...] = x1
    h2 = x1 * lax.rsqrt(jnp.mean(x1 * x1, axis=-1, keepdims=True) + EPS) * g_ref[...]
    rows = h2.shape[0]
    piece = D_MODEL // GATHER_SPLIT
    for q in range(GATHER_SPLIT):
        h2_ref[q] = h2[:, q * piece:(q + 1) * piece]
    hh, hl = _split(h2)
    wh, wl = _split(wr_ref[...])
    logits = dot(hh, wh) + (dot(hh, wl) + dot(hl, wh))
    lane = lax.broadcasted_iota(jnp.int32, logits.shape, 1)
    logits = jnp.where(lane < N_EXPERTS, logits, NEG)
    p = jnp.exp(logits - jnp.max(logits, axis=-1, keepdims=True))
    aff = p / jnp.sum(p, axis=-1, keepdims=True)
    for j in range(rows // LANE):
        afft_ref[:, j, :] = aff[j * LANE:(j + 1) * LANE, :].T[:N_EXPERTS, :]


def _out_proj(o_f, o_b, z, dn_norm_g, att, x, w_out, g, w_router, tm=1024):
    n = x.shape[0]
    wb = w_out.astype(BF16)
    wdn, watt = wb[:DN_WIDTH], wb[DN_WIDTH:]
    wr = jnp.pad(w_router.astype(F32), ((0, 0), (0, LANE - N_EXPERTS)))
    full = lambda a: pl.BlockSpec(a.shape, lambda i: (0, 0))
    row = lambda w: pl.BlockSpec((tm, w), lambda i: (i, 0))
    return pl.pallas_call(
        _out_proj_kernel,
        grid=(n // tm,),
        in_specs=[row(DN_WIDTH), row(DN_WIDTH), row(DN_WIDTH), full(dn_norm_g), row(ATT_WIDTH),
                  row(D_MODEL), full(wdn), full(watt), full(g), full(wr)],
        out_specs=[row(D_MODEL),
                   pl.BlockSpec((GATHER_SPLIT, tm, D_MODEL // GATHER_SPLIT), lambda i: (0, i, 0)),
                   pl.BlockSpec((N_EXPERTS, tm // LANE, LANE), lambda i: (0, i, 0))],
        out_shape=[jax.ShapeDtypeStruct((n, D_MODEL), F32),
                   jax.ShapeDtypeStruct((GATHER_SPLIT, n, D_MODEL // GATHER_SPLIT), F32),
                   jax.ShapeDtypeStruct((N_EXPERTS, n // LANE, LANE), F32)],
        compiler_params=_params("parallel"),
        name="out_proj",
    )(o_f, o_b, z, dn_norm_g, att, x, wdn, watt, g, wr)


def _route_kernel(aff_ref, pos_ref, idx_ref, off_ref, gate_ref, *, cap):
    x = aff_ref[0]
    nb = x.shape[0]
    bits = pltpu.bitcast(x, jnp.int32)
    total = lambda m: jnp.sum(jnp.sum(m, axis=0, keepdims=True), axis=1, keepdims=True)
    dot = lambda a, b: jnp.dot(a, b, preferred_element_type=F32)
    one_hot = lambda m: jnp.where(m, 1.0, 0.0)

    def bit_step(i, prefix):
        cand = prefix | lax.shift_left(jnp.int32(1), 30 - i)
        cnt = total(one_hot(bits >= cand))
        return jnp.where(cnt >= cap, cand, prefix)

    thr = lax.fori_loop(0, 31, bit_step, jnp.zeros((1, 1), jnp.int32))
    gt = bits > thr
    eq = bits == thr
    need = cap - total(one_hot(gt))

    li = lax.broadcasted_iota(jnp.int32, (LANE, LANE), 0)
    lj = lax.broadcasted_iota(jnp.int32, (LANE, LANE), 1)
    lane_incl = one_hot(li <= lj).astype(BF16)
    ones = jnp.ones((LANE, LANE), BF16)
    bi = lax.broadcasted_iota(jnp.int32, (nb, nb), 0)
    bj = lax.broadcasted_iota(jnp.int32, (nb, nb), 1)
    blk_before = one_hot(bj < bi).astype(BF16)

    def ranks(mask):
        m = one_hot(mask).astype(BF16)
        within = dot(m, lane_incl)
        tot = dot(m, ones)
        before = dot(blk_before, tot.astype(BF16))
        return within, tot, before

    w_eq, _, b_eq = ranks(eq)
    sel = gt | (eq & (w_eq + b_eq <= need))
    within, tot, before = ranks(sel)
    pos_ref[0] = jnp.where(sel, within + before - 1.0, -1.0)
    off_ref[0] = before.T[0:SUBLANE, :].astype(jnp.int32)

    within_t = within.T.astype(BF16)
    sel_t = one_hot(sel).T.astype(BF16)
    x_t = x.T
    x_parts = []
    for _ in range(3):
        part = x_t.astype(BF16)
        x_parts.append(part)
        x_t = x_t - part.astype(F32)
    after = before + tot
    blk_id = lax.broadcasted_iota(jnp.int32, (nb, LANE), 0).astype(F32)
    lane_id = lax.broadcasted_iota(jnp.int32, (LANE, LANE), 0).astype(F32)
    for c in range(cap // LANE):
        j = (lax.broadcasted_iota(jnp.int32, (1, LANE), 1) + c * LANE).astype(F32)
        holds = one_hot((before <= j) & (j < after))
        local = j - jnp.sum(holds * before, axis=0, keepdims=True) + 1.0
        blk = jnp.sum(holds * blk_id, axis=0, keepdims=True)
        hb = holds.astype(BF16)
        cnt_in_blk = dot(within_t, hb)
        sel_in_blk = dot(sel_t, hb)
        hit = one_hot((sel_in_blk > 0.5) & (cnt_in_blk == local))
        lane_of = jnp.sum(hit * lane_id, axis=0, keepdims=True)
        idx_ref[0, c:c + 1, :] = (blk * LANE + lane_of).astype(jnp.int32)
        aff_in_blk = dot(x_parts[0], hb) + dot(x_parts[1], hb) + dot(x_parts[2], hb)
        gate_ref[0, c:c + 1, :] = jnp.sum(hit * aff_in_blk, axis=0, keepdims=True)


def _route(aff3, cap):
    e, nb, _ = aff3.shape
    blk = lambda rows: pl.BlockSpec((1, rows, LANE), lambda i: (i, 0, 0))
    return pl.pallas_call(
        functools.partial(_route_kernel, cap=cap),
        grid=(e,),
        in_specs=[blk(nb)],
        out_specs=[blk(nb), blk(cap // LANE), pl.BlockSpec((1, SUBLANE, nb), lambda i: (i, 0, 0)),
                   blk(cap // LANE)],
        out_shape=[jax.ShapeDtypeStruct((e, nb, LANE), F32),
                   jax.ShapeDtypeStruct((e, cap // LANE, LANE), jnp.int32),
                   jax.ShapeDtypeStruct((e, SUBLANE, nb), jnp.int32),
                   jax.ShapeDtypeStruct((e, cap // LANE, LANE), F32)],
        compiler_params=_params("parallel"),
        name="route",
    )(aff3)


def _gather_rows(idx, pieces, n):
    return (jnp.arange(pieces, dtype=idx.dtype)[:, None] * n + idx[None, :]).reshape(1, -1)


def _sc_gather(x, rows):
    pieces, n, d = x.shape
    x = x.reshape(pieces * n, d)
    m = rows.shape[1]
    mesh = plsc.VectorSubcoreMesh(core_axis_name="c", subcore_axis_name="s")

    @pl.kernel(out_type=jax.ShapeDtypeStruct((m, d), x.dtype), mesh=mesh, scratch_types=[])
    def gather_kernel(x_hbm, i_hbm, o_hbm):
        def body(i_vmem, o_vmem):
            pltpu.sync_copy(x_hbm.at[i_vmem.at[0]], o_vmem)

        pltpu.emit_pipeline(
            body,
            grid=(m // GATHER_WINDOW,),
            in_specs=[pl.BlockSpec((1, GATHER_WINDOW), lambda i: (0, i))],
            out_specs=[pl.BlockSpec((GATHER_WINDOW, d), lambda i: (i, 0))],
            core_axis_name=("c", "s"),
            dimension_semantics=(pltpu.PARALLEL,),
        )(i_hbm, o_hbm)

    return gather_kernel(x, rows).reshape(pieces, m // pieces, d)


def _ffn_kernel(x_ref, gate_ref, wg_ref, wu_ref, wd_ref, ye_ref, xe_ref, acc_ref):
    f = pl.program_id(2)
    pieces, _, piece = x_ref.shape

    @pl.when(f == 0)
    def _():
        acc_ref[...] = jnp.zeros_like(acc_ref)
        for q in range(pieces):
            xe_ref[:, q * piece:(q + 1) * piece] = x_ref[q].astype(BF16)

    xe = xe_ref[...]
    a = jnp.dot(xe, wg_ref[0].astype(BF16), preferred_element_type=F32)
    b = jnp.dot(xe, wu_ref[0].astype(BF16), preferred_element_type=F32)
    h = (a * jax.nn.sigmoid(a) * b).astype(BF16)
    acc_ref[...] += jnp.dot(h, wd_ref[0].astype(BF16), preferred_element_type=F32)

    @pl.when(f == pl.num_programs(2) - 1)
    def _():
        for c in range(gate_ref.shape[1]):
            rows = slice(c * LANE, (c + 1) * LANE)
            gate = jnp.broadcast_to(gate_ref[0, c:c + 1, :], (LANE, LANE)).T
            ye_ref[0, rows, :] = (acc_ref[rows, :] * jnp.concatenate([gate] * (acc_ref.shape[1] // LANE), axis=1)
                                  ).astype(ye_ref.dtype)


def _expert_ffn(xe, gate, w_gate, w_up, w_down, tm=2048, tf=256):
    e = w_gate.shape[0]
    pieces, rows, piece = xe.shape
    cap, d = rows // e, piece * pieces
    dff = w_gate.shape[-1]
    tiles = cap // tm
    return pl.pallas_call(
        _ffn_kernel,
        grid=(e, tiles, dff // tf),
        in_specs=[pl.BlockSpec((pieces, tm, piece), lambda i, r, f: (0, i * tiles + r, 0)),
                  pl.BlockSpec((1, tm // LANE, LANE), lambda i, r, f: (i, r, 0)),
                  pl.BlockSpec((1, d, tf), lambda i, r, f: (i, 0, f)),
                  pl.BlockSpec((1, d, tf), lambda i, r, f: (i, 0, f)),
                  pl.BlockSpec((1, tf, d), lambda i, r, f: (i, f, 0))],
        out_specs=pl.BlockSpec((1, tm, d), lambda i, r, f: (i, r, 0)),
        out_shape=jax.ShapeDtypeStruct((e, cap, d), BF16),
        scratch_shapes=[pltpu.VMEM((tm, d), BF16), pltpu.VMEM((tm, d), F32)],
        compiler_params=_params("parallel", "parallel", "arbitrary"),
        name="expert_ffn",
    )(xe, gate, w_gate, w_up, w_down)


COMBINE_TAIL = 16


def _combine_kernel(r0_ref, spill_ref, x1_ref, pos_ref, g_ref, *rest):
    ye_refs, y_ref = rest[:N_EXPERTS], rest[N_EXPERTS]
    i = pl.program_id(0)
    nb = pl.num_programs(0)
    t, d = x1_ref.shape
    pos_t = jnp.concatenate([pos_ref[:, 0, :], jnp.zeros((LANE - N_EXPERTS, LANE), F32)], axis=0).T
    col = lax.broadcasted_iota(jnp.int32, (t, LANE), 1)

    def gathered(rows, shift):
        out = jnp.zeros((t, d), F32)
        for e0 in range(0, N_EXPERTS, 2):
            picks = []
            for e in (e0, e0 + 1):
                first = r0_ref[e * nb + i] + shift
                slot = jnp.broadcast_to(pos_t[:, e:e + 1], (t, LANE))
                picks.append(jnp.where(slot == (col + first).astype(F32), 1.0, 0.0).astype(BF16))
            out = out + jnp.dot(jnp.concatenate(picks, axis=1), jnp.concatenate([rows(e0), rows(e0 + 1)], axis=0),
                                preferred_element_type=F32)
        return out

    def finish(x):
        y_ref[...] = x * lax.rsqrt(jnp.mean(x * x, axis=-1, keepdims=True) + EPS) * g_ref[...]

    acc = x1_ref[...] + gathered(lambda e: ye_refs[e][:LANE, :], 0)
    finish(acc)

    @pl.when(spill_ref[i] != 0)
    def _():
        pad = jnp.zeros((LANE - COMBINE_TAIL, d), BF16)
        finish(acc + gathered(lambda e: jnp.concatenate([ye_refs[e][LANE:, :], pad], axis=0), LANE))


def _combine(x1, pos3, off, ye, final_g):
    n, d = x1.shape
    nb = n // LANE
    cap = ye.shape[1]
    first = off[:, 0, :]
    end = jnp.concatenate([first[:, 1:], jnp.full((N_EXPERTS, 1), cap, first.dtype)], axis=1)
    r0 = jnp.minimum(first // 16 * 16, cap - LANE - COMBINE_TAIL)
    spill = jnp.any(end > r0 + LANE, axis=0).astype(jnp.int32)
    pos4 = pos3.reshape(N_EXPERTS, nb, 1, LANE)
    window = lambda e, rows, shift: pl.BlockSpec(
        (None, pl.Element(rows), pl.Element(d)),
        lambda i, r0_ref, spill_ref: (e, pl.multiple_of(r0_ref[e * nb + i] + shift, 16), 0))
    grid_spec = pltpu.PrefetchScalarGridSpec(
        num_scalar_prefetch=2,
        grid=(nb,),
        in_specs=[pl.BlockSpec((LANE, d), lambda i, r, s: (i, 0)),
                  pl.BlockSpec((N_EXPERTS, None, 1, LANE), lambda i, r, s: (0, i, 0, 0)),
                  pl.BlockSpec((1, d), lambda i, r, s: (0, 0))]
                 + [window(e, LANE + COMBINE_TAIL, 0) for e in range(N_EXPERTS)],
        out_specs=pl.BlockSpec((LANE, d), lambda i, r, s: (i, 0)),
    )
    return pl.pallas_call(
        _combine_kernel,
        grid_spec=grid_spec,
        out_shape=jax.ShapeDtypeStruct((n, d), F32),
        compiler_params=_params("parallel"),
        name="moe_combine",
    )(r0.reshape(-1).astype(jnp.int32), spill, x1, pos4, final_g, *([ye] * N_EXPERTS))


def _dn_prep_kernel(x_ref, xp_ref, xn_ref, w_ref, q_ref, k_ref, v_ref, kt_ref):
    t = pl.program_id(1)
    nt = pl.num_programs(1)
    x = x_ref[0]
    rows = x.shape[0]
    row = lax.broadcasted_iota(jnp.int32, (rows, 1), 0)
    before = jnp.where(t > 0, xp_ref[0, SUBLANE - 1:SUBLANE, :], 0.0)
    after = jnp.where(t < nt - 1, xn_ref[0, 0:1, :], 0.0)
    x_prev = jnp.where(row == 0, before, pltpu.roll(x, 1, axis=0))
    x_next = jnp.where(row == rows - 1, after, pltpu.roll(x, rows - 1, axis=0))
    y = w_ref[0:1, :] * x_prev + w_ref[1:2, :] * x + w_ref[2:3, :] * x_next
    y = y * jax.nn.sigmoid(y)
    for h in range(DN_HEADS):
        for part, ref, scale in ((0, q_ref, DN_HEAD_DIM ** -0.5), (1, k_ref, 1.0)):
            c0 = part * DN_WIDTH + h * DN_HEAD_DIM
            a = y[:, c0:c0 + DN_HEAD_DIM]
            inv = lax.rsqrt(jnp.sum(a * a, axis=-1, keepdims=True) + EPS) * scale
            a = a * inv
            ref[0, :, h * DN_HEAD_DIM:(h + 1) * DN_HEAD_DIM] = a.astype(ref.dtype)
            if part == 1:
                for j in range(rows // CHUNK):
                    kt_ref[0, j, h * DN_HEAD_DIM:(h + 1) * DN_HEAD_DIM, :] = (
                        a[j * CHUNK:(j + 1) * CHUNK, :].T.astype(kt_ref.dtype))
    v_ref[0] = y[:, 2 * DN_WIDTH:].astype(v_ref.dtype)


def _dn_prep(qkv, conv_w, tile=512):
    b, s, c = qkv.shape
    tile = min(tile, s)
    nt = s // tile
    per = tile // SUBLANE
    nsub = s // SUBLANE
    out = jax.ShapeDtypeStruct((b, s, DN_WIDTH), BF16)
    ospec = pl.BlockSpec((1, tile, DN_WIDTH), lambda bi, ti: (bi, ti, 0))
    return pl.pallas_call(
        _dn_prep_kernel,
        grid=(b, nt),
        in_specs=[pl.BlockSpec((1, tile, c), lambda bi, ti: (bi, ti, 0)),
                  pl.BlockSpec((1, SUBLANE, c), lambda bi, ti: (bi, jnp.maximum(ti * per - 1, 0), 0)),
                  pl.BlockSpec((1, SUBLANE, c), lambda bi, ti: (bi, jnp.minimum((ti + 1) * per, nsub - 1), 0)),
                  pl.BlockSpec((CONV_K, c), lambda bi, ti: (0, 0))],
        out_specs=[ospec, ospec, ospec,
                   pl.BlockSpec((1, tile // CHUNK, DN_WIDTH, CHUNK), lambda bi, ti: (bi, ti, 0, 0))],
        out_shape=[out, out, out, jax.ShapeDtypeStruct((b, s // CHUNK, DN_WIDTH, CHUNK), BF16)],
        compiler_params=_params("parallel", "parallel"),
        name="dn_prep",
    )(qkv, qkv, qkv, conv_w)


def _lane_bcast(a, col, width=LANE):
    return jnp.broadcast_to(a[:, col:col + 1], (a.shape[0], width))


DN_TILE = 1024
INV_STEPS = 5
assert 2 ** (INV_STEPS + 1) == CHUNK
DN_UNROLL = 64
DN_BUILD_UNROLL = 4


def _dn_scan_kernel(q_ref, k_ref, v_ref, kt_ref, ba_ref, alog_ref, dtb_ref, o_ref,
                    s_ref, xbuf, tbuf, abuf, rbuf, uwbuf, qgbuf, kdbuf, lbuf, nbuf, obuf, ecbuf, *, reverse):
    @pl.when(pl.program_id(1) == 0)
    def _():
        s_ref[...] = jnp.zeros_like(s_ref)

    tile = q_ref.shape[1]
    nc = tile // CHUNK
    n_prob = nc * DN_HEADS
    unroll = min(DN_UNROLL, n_prob)
    beta_col = DN_HEADS if reverse else 0
    a_col = 2 * DN_HEADS + (DN_HEADS if reverse else 0)
    ri = lax.broadcasted_iota(jnp.int32, (CHUNK, CHUNK), 0)
    ci = lax.broadcasted_iota(jnp.int32, (CHUNK, CHUNK), 1)
    tri = (ri <= ci) if reverse else (ri >= ci)
    strict = (ri < ci) if reverse else (ri > ci)
    eye = jnp.where(ri == ci, 1.0, 0.0)
    row = lax.broadcasted_iota(jnp.int32, (CHUNK, LANE), 0)
    a_scale = jnp.exp(alog_ref[...])
    dtb = dtb_ref[...]
    dot = lambda a, b: jnp.dot(a, b, preferred_element_type=F32)

    def build(c, carry):
        rows = pl.ds(pl.multiple_of(c * CHUNK, CHUNK), CHUNK)
        raw = ba_ref[0, rows, :]
        beta_all = jax.nn.sigmoid(raw)
        xs = raw + dtb
        g = -a_scale * (jnp.maximum(xs, 0.0) + jnp.log(1.0 + jnp.exp(-jnp.abs(xs))))
        gs = g
        sh = 1
        while sh < CHUNK:
            if reverse:
                gs = gs + jnp.where(row < CHUNK - sh, pltpu.roll(gs, CHUNK - sh, axis=0), 0.0)
            else:
                gs = gs + jnp.where(row >= sh, pltpu.roll(gs, sh, axis=0), 0.0)
            sh *= 2
        gs_t = gs.T
        g_last = gs[0:1, :] if reverse else gs[CHUNK - 1:CHUNK, :]
        eg_all = jnp.exp(gs)
        ek_t = jnp.exp(g_last.T - gs_t)
        ecbuf[c] = jnp.broadcast_to(jnp.exp(g_last), (SUBLANE, LANE))
        for h in range(DN_HEADS):
            p = c * DN_HEADS + h
            hs = slice(h * DN_HEAD_DIM, (h + 1) * DN_HEAD_DIM)
            q = q_ref[0, rows, hs].astype(F32)
            k = k_ref[0, rows, hs]
            v = v_ref[0, rows, hs].astype(F32)
            beta = _lane_bcast(beta_all, beta_col + h)
            eg = _lane_bcast(eg_all, a_col + h)
            diff = _lane_bcast(gs, a_col + h, CHUNK) - gs_t[a_col + h:a_col + h + 1, :]
            decay = jnp.where(tri, jnp.exp(jnp.where(tri, diff, 0.0)), 0.0)
            kb = k.astype(F32) * beta
            kq = lax.dot_general(jnp.concatenate([kb, q], axis=0).astype(BF16), k,
                                 (((1,), (1,)), ((), ())), preferred_element_type=F32)
            lower = jnp.where(strict, kq[:CHUNK] * decay, 0.0)
            xbuf[p] = (-lower).astype(BF16)
            tbuf[p] = eye - lower
            abuf[p] = (kq[CHUNK:] * decay).astype(BF16)
            rbuf[p] = jnp.concatenate([v * beta, kb * eg], axis=1).astype(BF16)
            qgbuf[p] = q * eg
            kdbuf[p] = (kt_ref[0, c, hs, :].astype(F32)
                        * ek_t[a_col + h:a_col + h + 1, :]).astype(BF16)
        return carry

    lax.fori_loop(0, nc, build, 0, unroll=min(DN_BUILD_UNROLL, nc))

    for step in range(INV_STEPS + 1):
        def double(p, carry, first=(step == 0), last=(step == INV_STEPS)):
            x = xbuf[p]
            if not first:
                t = tbuf[p]
                tbuf[p] = t + dot(t.astype(BF16), x)
            if not last:
                xbuf[p] = dot(x, x).astype(BF16)
            return carry
        lax.fori_loop(0, n_prob, double, 0, unroll=unroll)

    def solve(p, carry):
        uwbuf[p] = dot(tbuf[p].astype(BF16), rbuf[p]).astype(BF16)
        return carry

    lax.fori_loop(0, n_prob, solve, 0, unroll=unroll)

    def fold(p, carry):
        uw = uwbuf[p]
        a_uw = dot(abuf[p], uw)
        k_uw = dot(kdbuf[p], uw)
        obuf[p] = a_uw[:, :DN_HEAD_DIM]
        nbuf[p] = k_uw[:, :DN_HEAD_DIM]
        lbuf[p, :DN_HEAD_DIM, :] = k_uw[:, DN_HEAD_DIM:].astype(BF16)
        lbuf[p, DN_HEAD_DIM:, :] = (qgbuf[p] - a_uw[:, DN_HEAD_DIM:]).astype(BF16)
        return carry

    lax.fori_loop(0, n_prob, fold, 0, unroll=unroll)

    def scan(jc, carry):
        c = (nc - 1 - jc) if reverse else jc
        rows = pl.ds(pl.multiple_of(c * CHUNK, CHUNK), CHUNK)
        ec_all = ecbuf[c]
        for h in range(DN_HEADS):
            p = c * DN_HEADS + h
            state = s_ref[h]
            r = dot(lbuf[p], state.astype(BF16))
            s_ref[h] = state * _lane_bcast(ec_all[0:1], a_col + h) - r[:DN_HEAD_DIM] + nbuf[p]
            o = r[DN_HEAD_DIM:] + obuf[p]
            o_ref[0, rows, h * DN_HEAD_DIM:(h + 1) * DN_HEAD_DIM] = o.astype(o_ref.dtype)
        return carry

    lax.fori_loop(0, nc, scan, 0)


def _dn_scan(q, k, v, kt, ba, a_log_row, dt_bias_row, reverse, tile=DN_TILE):
    b, s, _ = q.shape
    tile = min(tile, s)
    nt = s // tile
    nc = tile // CHUNK
    n_prob = nc * DN_HEADS
    tmap = (lambda bi, ti: (bi, nt - 1 - ti, 0)) if reverse else (lambda bi, ti: (bi, ti, 0))
    wide = pl.BlockSpec((1, tile, DN_WIDTH), tmap)
    par = pl.BlockSpec((1, LANE), lambda bi, ti: (0, 0))
    return pl.pallas_call(
        functools.partial(_dn_scan_kernel, reverse=reverse),
        grid=(b, nt),
        in_specs=[wide, wide, wide,
                  pl.BlockSpec((1, nc, DN_WIDTH, CHUNK), lambda bi, ti: tmap(bi, ti) + (0,)),
                  pl.BlockSpec((1, tile, LANE), tmap), par, par],
        out_specs=wide,
        out_shape=jax.ShapeDtypeStruct((b, s, DN_WIDTH), BF16),
        scratch_shapes=[pltpu.VMEM((DN_HEADS, DN_HEAD_DIM, DN_HEAD_DIM), F32),
                        pltpu.VMEM((n_prob, CHUNK, CHUNK), BF16),
                        pltpu.VMEM((n_prob, CHUNK, CHUNK), F32),
                        pltpu.VMEM((n_prob, CHUNK, CHUNK), BF16),
                        pltpu.VMEM((n_prob, CHUNK, 2 * DN_HEAD_DIM), BF16),
                        pltpu.VMEM((n_prob, CHUNK, 2 * DN_HEAD_DIM), BF16),
                        pltpu.VMEM((n_prob, CHUNK, DN_HEAD_DIM), F32),
                        pltpu.VMEM((n_prob, DN_HEAD_DIM, CHUNK), BF16),
                        pltpu.VMEM((n_prob, DN_HEAD_DIM + CHUNK, DN_HEAD_DIM), BF16),
                        pltpu.VMEM((n_prob, DN_HEAD_DIM, DN_HEAD_DIM), F32),
                        pltpu.VMEM((n_prob, CHUNK, DN_HEAD_DIM), F32),
                        pltpu.VMEM((nc, SUBLANE, LANE), F32)],
        compiler_params=_params("parallel", "arbitrary"),
        name="dn_scan_bwd" if reverse else "dn_scan_fwd",
    )(q, k, v, kt, ba, a_log_row, dt_bias_row)


def _gate_rows(a_log_fwd, a_log_bwd, dt_bias_fwd, dt_bias_bwd):
    pad = lambda f, b: jnp.pad(jnp.concatenate([f, b]).astype(F32), (2 * DN_HEADS, LANE - 4 * DN_HEADS))[None]
    return pad(a_log_fwd, a_log_bwd), pad(dt_bias_fwd, dt_bias_bwd)


def _deltanet(qkv, ba, conv_w, a_log_fwd, dt_bias_fwd, a_log_bwd, dt_bias_bwd):
    q, k, v, kt = _dn_prep(qkv, conv_w)
    alog, dtb = _gate_rows(a_log_fwd, a_log_bwd, dt_bias_fwd, dt_bias_bwd)
    o_f = _dn_scan(q, k, v, kt, ba, alog, dtb, reverse=False)
    o_b = _dn_scan(q, k, v, kt, ba, alog, dtb, reverse=True)
    return o_f, o_b


def _t5_bucket(rel):
    nb = NUM_BUCKETS // 2
    ret = jnp.where(rel > 0, nb, 0)
    n = jnp.abs(rel)
    max_exact = nb // 2
    nf = jnp.maximum(n, 1).astype(F32)
    large = max_exact + (jnp.log(nf / max_exact) / math.log(MAX_DISTANCE / max_exact)
                         * (nb - max_exact)).astype(jnp.int32)
    large = jnp.minimum(large, nb - 1)
    return ret + jnp.where(n < max_exact, n, large)


ATT_TILE = 2048
ATT_HALO = BAND * max(d for _, d in DILATED_PATTERNS)
QBLK = 2 * BAND
KBLK = QBLK + 2 * BAND


def _attn_bias_table(rel_bias):
    delta = (jnp.arange(KBLK) - BAND)[None, :] - jnp.arange(QBLK)[:, None]
    tabs = []
    for window, dil in DILATED_PATTERNS:
        half = window // (2 * dil)
        onehot = (_t5_bucket(delta * dil)[..., None] == jnp.arange(NUM_BUCKETS)).astype(F32)
        bias = jnp.einsum('qkb,bh->hqk', onehot, rel_bias.astype(F32),
                          precision=lax.Precision.HIGHEST)
        tabs.append(jnp.where((jnp.abs(delta) <= half)[None], bias, NEG))
    return jnp.stack(tabs)


def _attn_kernel(q_ref, kp_ref, kc_ref, kn_ref, vp_ref, vc_ref, vn_ref, bias_ref, o_ref,
                 qd_a, qd_b, kf, vf, kd, vd, od, md, ld, o_sc, m_sc, l_sc, *, seq_len):
    t = pl.program_id(1)
    lane = lax.broadcasted_iota(jnp.int32, (QBLK, LANE), 1)
    low = lane < ATT_HEAD_DIM
    scale = ATT_HEAD_DIM ** -0.5
    kcol = lax.broadcasted_iota(jnp.int32, (1, KBLK), 1)
    ones = jnp.ones((KBLK, LANE), BF16)

    for p, (window, dil) in enumerate(DILATED_PATTERNS):
        lt = ATT_TILE // dil
        lh = ATT_HALO // dil
        span = lt + 2 * lh
        nblk = lt // QBLK
        n_pos = seq_len // dil
        qd, q_prev = (qd_a, qd_b) if p % 2 == 0 else (qd_b, qd_a)
        prev_dil = DILATED_PATTERNS[p - 1][1] if p > 0 else 1
        nested = p > 0 and prev_dil > 1 and dil % prev_dil == 0
        step = dil // prev_dil
        keep_f32 = p + 1 < len(DILATED_PATTERNS) and dil > 1 and DILATED_PATTERNS[p + 1][1] % dil == 0
        for r in range(dil):
            if nested:
                rp, a = r % prev_dil, r // prev_dil
                lt_p, span_p = ATT_TILE // prev_dil, (ATT_TILE + 2 * ATT_HALO) // prev_dil
                qd[pl.ds(r * lt, lt), :] = q_prev[pl.ds(rp * lt_p + a, lt, stride=step), :]
                src = pl.ds(rp * span_p + a, span, stride=step)
                kd[pl.ds(r * span, span), :] = kf[src, :].astype(BF16)
                vd[pl.ds(r * span, span), :] = vf[src, :].astype(BF16)
                continue
            sl = lambda n: pl.ds(r, n, stride=dil) if dil > 1 else pl.ds(0, n)
            qd[pl.ds(r * lt, lt), :] = q_ref[0, sl(lt), :] * scale
            off = r * span
            for (kr, vr, n) in ((kp_ref, vp_ref, lh), (kc_ref, vc_ref, lt), (kn_ref, vn_ref, lh)):
                kx, vx = kr[0, sl(n), :], vr[0, sl(n), :]
                if keep_f32:
                    kf[pl.ds(off, n), :] = kx
                    vf[pl.ds(off, n), :] = vx
                kd[pl.ds(off, n), :] = kx.astype(BF16)
                vd[pl.ds(off, n), :] = vx.astype(BF16)
                off += n


        for i in range(dil * nblk):
            r, j = divmod(i, nblk)
            q0 = r * lt + j * QBLK
            k0 = r * span + lh + j * QBLK - BAND
            q = qd[pl.ds(q0, QBLK), :]
            k = kd[pl.ds(k0, KBLK), :]
            v1 = jnp.concatenate([vd[pl.ds(k0, KBLK), :], ones], axis=1)
            edge = j == 0 or j == nblk - 1
            if edge:
                pos = t * lt + j * QBLK - BAND + kcol
                valid = (pos >= 0) & (pos < n_pos)
            res = []
            for h in range(2):
                qh = jnp.where(low, q, 0.0) if h == 0 else jnp.where(low, 0.0, q)
                s = lax.dot_general(qh.astype(BF16), k, (((1,), (1,)), ((), ())),
                                    preferred_element_type=F32) + bias_ref[p, h]
                if edge:
                    s = jnp.where(valid, s, NEG)
                m = jnp.max(s, axis=-1, keepdims=True)
                e = jnp.exp(s - m).astype(BF16)
                res.append((jnp.dot(e, v1, preferred_element_type=F32), m))
            (pv0, m0), (pv1, m1) = res
            od[pl.ds(q0, QBLK), :] = jnp.where(low, pv0[:, :LANE], pv1[:, :LANE])
            ld[pl.ds(q0, QBLK), :] = jnp.where(low, pv0[:, LANE:], pv1[:, LANE:])
            md[pl.ds(q0, QBLK), :] = jnp.where(low, m0, m1)

        for r in range(dil):
            sl = pl.ds(r, lt, stride=dil) if dil > 1 else pl.ds(0, lt)
            src = pl.ds(r * lt, lt)
            o_sc[p, sl, :] = od[src, :]
            m_sc[p, sl, :] = md[src, :]
            l_sc[p, sl, :] = ld[src, :]

    n_p = len(DILATED_PATTERNS)
    mx = m_sc[0]
    for p in range(1, n_p):
        mx = jnp.maximum(mx, m_sc[p])
    num = jnp.zeros((ATT_TILE, LANE), F32)
    den = jnp.zeros((ATT_TILE, LANE), F32)
    for p in range(n_p):
        w = jnp.exp(m_sc[p] - mx)
        num += o_sc[p] * w
        den += l_sc[p] * w
    o_ref[0] = (num / den).astype(o_ref.dtype)


def _dilated_attention(att, bias):
    b, s, _ = att.shape
    assert s % ATT_TILE == 0
    nt = s // ATT_TILE
    nh = s // ATT_HALO
    per = ATT_TILE // ATT_HALO
    hp = ATT_WIDTH // LANE
    cur = lambda col0: pl.BlockSpec((1, ATT_TILE, LANE), lambda bi, ti, hi: (bi, ti, col0 + hi))
    prev = lambda col0: pl.BlockSpec(
        (1, ATT_HALO, LANE), lambda bi, ti, hi: (bi, jnp.maximum(ti * per - 1, 0), col0 + hi))
    nxt = lambda col0: pl.BlockSpec(
        (1, ATT_HALO, LANE), lambda bi, ti, hi: (bi, jnp.minimum((ti + 1) * per, nh - 1), col0 + hi))
    n_p = len(DILATED_PATTERNS)
    tile_f32 = pltpu.VMEM((ATT_TILE, LANE), F32)
    span_bf16 = pltpu.VMEM((ATT_TILE + 2 * ATT_HALO, LANE), BF16)
    span_f32 = pltpu.VMEM((ATT_TILE + 2 * ATT_HALO, LANE), F32)
    per_pattern = pltpu.VMEM((n_p, ATT_TILE, LANE), F32)
    return pl.pallas_call(
        functools.partial(_attn_kernel, seq_len=s),
        grid=(b, nt, hp),
        in_specs=[cur(0), prev(hp), cur(hp), nxt(hp), prev(2 * hp), cur(2 * hp), nxt(2 * hp),
                  pl.BlockSpec((n_p, 2, QBLK, KBLK), lambda bi, ti, hi: (0, hi, 0, 0))],
        out_specs=pl.BlockSpec((1, ATT_TILE, LANE), lambda bi, ti, hi: (bi, ti, hi)),
        out_shape=jax.ShapeDtypeStruct((b, s, ATT_WIDTH), BF16),
        scratch_shapes=[tile_f32, tile_f32, span_f32, span_f32, span_bf16, span_bf16, tile_f32, tile_f32, tile_f32,
                        per_pattern, per_pattern, per_pattern],
        compiler_params=_params("parallel", "parallel", "parallel"),
        name="dilated_attention",
    )(att, att, att, att, att, att, att, bias)


def _mix(x, att_bias, norm1_g, w_in, conv_w, a_log_fwd, dt_bias_fwd, a_log_bwd, dt_bias_bwd,
         dn_norm_g, w_out, norm2_g, w_router):
    bsz, s, d = x.shape
    xf = x.reshape(bsz * s, d)
    qkv, z, ba, att = _in_proj(xf, norm1_g[None], w_in)
    shp = lambda a: a.reshape(bsz, s, a.shape[-1])
    o_f, o_b = _deltanet(shp(qkv), shp(ba), conv_w, a_log_fwd, dt_bias_fwd, a_log_bwd, dt_bias_bwd)
    att_out = _dilated_attention(shp(att), att_bias)
    flat = lambda a: a.reshape(bsz * s, a.shape[-1])
    return _out_proj(flat(o_f), flat(o_b), z, dn_norm_g[None], flat(att_out), xf, w_out, norm2_g[None],
                     w_router)


def kernel(x_prompt, x_sample, rel_bias, norm1_g, w_in, conv_w, a_log_fwd, dt_bias_fwd, a_log_bwd,
           dt_bias_bwd, dn_norm_g, w_out, norm2_g, w_router, w_gate, w_up, w_down, final_norm_g):
    groups = [x_prompt, x_sample]
    att_bias = _attn_bias_table(rel_bias)
    parts, xes = [], []
    for k in range(len(groups)):
        x = groups[k]
        x1, h2, afft = _mix(x, att_bias, norm1_g[0], w_in[0], conv_w[0], a_log_fwd[0], dt_bias_fwd[0],
                            a_log_bwd[0], dt_bias_bwd[0], dn_norm_g[0], w_out[0], norm2_g[0], w_router[0])
        n_tok = x1.shape[0]
        cap = CAPACITY_FACTOR * n_tok // N_EXPERTS
        pos, idx, off, gate = _route(afft, cap)
        rows = _gather_rows(idx.reshape(-1), GATHER_SPLIT, n_tok)
        if k + 1 < len(groups):
            rows, groups[k + 1] = lax.optimization_barrier((rows, groups[k + 1]))
        else:
            rows, xes[0] = lax.optimization_barrier((rows, xes[0]))
        xes.append(_sc_gather(h2, rows))
        parts.append((x, x1, pos, off, gate))
    outs = []
    for k, (x, x1, pos, off, gate) in enumerate(parts):
        ye = _expert_ffn(xes[k], gate, w_gate[0], w_up[0], w_down[0])
        if k + 1 < len(parts):
            xes[k + 1], ye = lax.optimization_barrier((xes[k + 1], ye))
        outs.append(_combine(x1, pos, off, ye, final_norm_g[None]).reshape(groups[k].shape))
    return tuple(outs)
```

```python
import functools
import math

import jax
import jax.numpy as jnp
from jax import lax
from jax.experimental import pallas as pl
from jax.experimental.pallas import tpu as pltpu
from jax.experimental.pallas import tpu_sc as plsc

D_MODEL = 1024
DN_HEADS = 4
DN_HEAD_DIM = 128
DN_WIDTH = DN_HEADS * DN_HEAD_DIM
ATT_HEADS = 8
ATT_HEAD_DIM = 64
ATT_WIDTH = ATT_HEADS * ATT_HEAD_DIM
CONV_K = 3
CHUNK = 64
DILATED_PATTERNS = ((128, 1), (512, 4), (2048, 16))
BAND = 64
NUM_BUCKETS = 32
MAX_DISTANCE = 1024
N_EXPERTS = 16
CAPACITY_FACTOR = 2
EXPERT_D_FF = 2816
EPS = 1e-6
NEG = -1e30
GATE_COLS = 4 * DN_HEADS
LANE = 128
SUBLANE = 8
GATHER_WINDOW = 128
GATHER_SPLIT = 4

F32 = jnp.float32
BF16 = jnp.bfloat16

VMEM_LIMIT = 56 * 1024 * 1024


def _params(*sem):
    return pltpu.CompilerParams(dimension_semantics=sem, vmem_limit_bytes=VMEM_LIMIT)


def _in_proj_kernel(x_ref, xp_ref, xn_ref, g_ref, wqkv_ref, wz_ref, wba_ref, watt_ref, cw_ref,
                    q_ref, k_ref, v_ref, kt_ref, z_ref, ba_ref, att_ref, *, tiles_per_seq):
    t = pl.program_id(0) % tiles_per_seq
    tm = x_ref.shape[0]
    x = jnp.concatenate([xp_ref[...], x_ref[...], xn_ref[...]], axis=0)
    h = x * lax.rsqrt(jnp.mean(x * x, axis=-1, keepdims=True) + EPS) * g_ref[...]
    hb = h[SUBLANE:SUBLANE + tm].astype(BF16)
    z_ref[...] = jnp.dot(hb, wz_ref[...], preferred_element_type=F32)
    ba_ref[...] = jnp.dot(hb, wba_ref[...], preferred_element_type=F32)
    att_ref[...] = jnp.dot(hb, watt_ref[...], preferred_element_type=F32)

    p = jnp.dot(h.astype(BF16), wqkv_ref[...], preferred_element_type=F32)
    rows = p.shape[0]
    row = lax.broadcasted_iota(jnp.int32, (tm, 1), 0)
    mid = lambda a: a[SUBLANE:SUBLANE + tm]
    p_prev = jnp.where((row == 0) & (t == 0), 0.0, mid(pltpu.roll(p, 1, axis=0)))
    p_next = jnp.where((row == tm - 1) & (t == tiles_per_seq - 1), 0.0, mid(pltpu.roll(p, rows - 1, axis=0)))
    y = cw_ref[0:1, :] * p_prev + cw_ref[1:2, :] * mid(p) + cw_ref[2:3, :] * p_next
    y = y * jax.nn.sigmoid(y)
    for hd in range(DN_HEADS):
        for part, ref, scale in ((0, q_ref, DN_HEAD_DIM ** -0.5), (1, k_ref, 1.0)):
            c0 = part * DN_WIDTH + hd * DN_HEAD_DIM
            a = y[:, c0:c0 + DN_HEAD_DIM]
            a = a * (lax.rsqrt(jnp.sum(a * a, axis=-1, keepdims=True) + EPS) * scale)
            ref[:, hd * DN_HEAD_DIM:(hd + 1) * DN_HEAD_DIM] = a.astype(ref.dtype)
            if part == 1:
                for j in range(tm // CHUNK):
                    kt_ref[j, hd * DN_HEAD_DIM:(hd + 1) * DN_HEAD_DIM, :] = (
                        a[j * CHUNK:(j + 1) * CHUNK, :].T.astype(kt_ref.dtype))
    v_ref[...] = y[:, 2 * DN_WIDTH:].astype(v_ref.dtype)


def _in_proj(x, g, w_in, conv_w, seq_len, tm=512):
    n = x.shape[0]
    o_qkv = 3 * DN_WIDTH
    o_z = o_qkv + DN_WIDTH
    o_ba = o_z + GATE_COLS
    wb = w_in.astype(BF16)
    wqkv = wb[:, :o_qkv]
    wz = wb[:, o_qkv:o_z]
    wba = jnp.pad(wb[:, o_z:o_ba], ((0, 0), (0, LANE - GATE_COLS)))
    watt = wb[:, o_ba:]
    per = tm // SUBLANE
    nsub = n // SUBLANE
    full = lambda a: pl.BlockSpec(a.shape, lambda i: (0, 0))
    row = lambda w: pl.BlockSpec((tm, w), lambda i: (i, 0))
    dn = jax.ShapeDtypeStruct((n, DN_WIDTH), BF16)
    return pl.pallas_call(
        functools.partial(_in_proj_kernel, tiles_per_seq=seq_len // tm),
        grid=(n // tm,),
        in_specs=[row(D_MODEL),
                  pl.BlockSpec((SUBLANE, D_MODEL), lambda i: (jnp.maximum(i * per - 1, 0), 0)),
                  pl.BlockSpec((SUBLANE, D_MODEL), lambda i: (jnp.minimum((i + 1) * per, nsub - 1), 0)),
                  full(g), full(wqkv), full(wz), full(wba), full(watt), full(conv_w)],
        out_specs=[row(DN_WIDTH), row(DN_WIDTH), row(DN_WIDTH),
                   pl.BlockSpec((tm // CHUNK, DN_WIDTH, CHUNK), lambda i: (i, 0, 0)),
                   row(DN_WIDTH), row(LANE), row(3 * ATT_WIDTH)],
        out_shape=[dn, dn, dn, jax.ShapeDtypeStruct((n // CHUNK, DN_WIDTH, CHUNK), BF16),
                   jax.ShapeDtypeStruct((n, DN_WIDTH), F32),
                   jax.ShapeDtypeStruct((n, LANE), F32),
                   jax.ShapeDtypeStruct((n, 3 * ATT_WIDTH), F32)],
        compiler_params=_params("parallel"),
        name="in_proj",
    )(x, x, x, g, wqkv, wz, wba, watt, conv_w)


def _split(a):
    hi = a.astype(BF16)
    return hi, (a - hi.astype(F32)).astype(BF16)


def _out_proj_kernel(of_ref, ob_ref, z_ref, gdn_ref, att_ref, x_ref, wdn_ref, watt_ref, g_ref, wr_ref,
                     x1_ref, h2_ref, afft_ref):
    dot = lambda a, b: jnp.dot(a, b, preferred_element_type=F32)
    o = of_ref[...].astype(F32) + ob_ref[...].astype(F32)
    z = z_ref[...]
    gated = []
    for h in range(DN_HEADS):
        hs = slice(h * DN_HEAD_DIM, (h + 1) * DN_HEAD_DIM)
        oh = o[:, hs]
        oh = oh * lax.rsqrt(jnp.mean(oh * oh, axis=-1, keepdims=True) + EPS) * gdn_ref[...]
        zh = z[:, hs]
        gated.append((oh * (zh * jax.nn.sigmoid(zh))).astype(BF16))
    dn = jnp.concatenate(gated, axis=1)
    x1 = x_ref[...] + dot(dn, wdn_ref[...]) + dot(att_ref[...], watt_ref[...])
    x1_ref[...] = x1
    h2 = x1 * lax.rsqrt(jnp.mean(x1 * x1, axis=-1, keepdims=True) + EPS) * g_ref[...]
    rows = h2.shape[0]
    piece = D_MODEL // GATHER_SPLIT
    for q in range(GATHER_SPLIT):
        h2_ref[q] = h2[:, q * piece:(q + 1) * piece]
    hh, hl = _split(h2)
    wh, wl = _split(wr_ref[...])
    logits = dot(hh, wh) + (dot(hh, wl) + dot(hl, wh))
    lane = lax.broadcasted_iota(jnp.int32, logits.shape, 1)
    logits = jnp.where(lane < N_EXPERTS, logits, NEG)
    p = jnp.exp(logits - jnp.max(logits, axis=-1, keepdims=True))
    aff = p / jnp.sum(p, axis=-1, keepdims=True)
    for j in range(rows // LANE):
        afft_ref[:, j, :] = aff[j * LANE:(j + 1) * LANE, :].T[:N_EXPERTS, :]


def _out_proj(o_f, o_b, z, dn_norm_g, att, x, w_out, g, w_router, tm=1024):
    n = x.shape[0]
    wb = w_out.astype(BF16)
    wdn, watt = wb[:DN_WIDTH], wb[DN_WIDTH:]
    wr = jnp.pad(w_router.astype(F32), ((0, 0), (0, LANE - N_EXPERTS)))
    full = lambda a: pl.BlockSpec(a.shape, lambda i: (0, 0))
    row = lambda w: pl.BlockSpec((tm, w), lambda i: (i, 0))
    return pl.pallas_call(
        _out_proj_kernel,
        grid=(n // tm,),
        in_specs=[row(DN_WIDTH), row(DN_WIDTH), row(DN_WIDTH), full(dn_norm_g), row(ATT_WIDTH),
                  row(D_MODEL), full(wdn), full(watt), full(g), full(wr)],
        out_specs=[row(D_MODEL),
                   pl.BlockSpec((GATHER_SPLIT, tm, D_MODEL // GATHER_SPLIT), lambda i: (0, i, 0)),
                   pl.BlockSpec((N_EXPERTS, tm // LANE, LANE), lambda i: (0, i, 0))],
        out_shape=[jax.ShapeDtypeStruct((n, D_MODEL), F32),
                   jax.ShapeDtypeStruct((GATHER_SPLIT, n, D_MODEL // GATHER_SPLIT), F32),
                   jax.ShapeDtypeStruct((N_EXPERTS, n // LANE, LANE), F32)],
        compiler_params=_params("parallel"),
        name="out_proj",
    )(o_f, o_b, z, dn_norm_g, att, x, wdn, watt, g, wr)


def _route_kernel(aff_ref, pos_ref, idx_ref, off_ref, gate_ref, *, cap):
    x = aff_ref[0]
    nb = x.shape[0]
    bits = pltpu.bitcast(x, jnp.int32)
    total = lambda m: jnp.sum(jnp.sum(m, axis=0, keepdims=True), axis=1, keepdims=True)
    dot = lambda a, b: jnp.dot(a, b, preferred_element_type=F32)
    one_hot = lambda m: jnp.where(m, 1.0, 0.0)

    def bit_step(i, prefix):
        cand = prefix | lax.shift_left(jnp.int32(1), 30 - i)
        cnt = total(one_hot(bits >= cand))
        return jnp.where(cnt >= cap, cand, prefix)

    thr = lax.fori_loop(0, 31, bit_step, jnp.zeros((1, 1), jnp.int32))
    gt = bits > thr
    eq = bits == thr
    need = cap - total(one_hot(gt))

    li = lax.broadcasted_iota(jnp.int32, (LANE, LANE), 0)
    lj = lax.broadcasted_iota(jnp.int32, (LANE, LANE), 1)
    lane_incl = one_hot(li <= lj).astype(BF16)
    ones = jnp.ones((LANE, LANE), BF16)
    bi = lax.broadcasted_iota(jnp.int32, (nb, nb), 0)
    bj = lax.broadcasted_iota(jnp.int32, (nb, nb), 1)
    blk_before = one_hot(bj < bi).astype(BF16)

    def ranks(mask):
        m = one_hot(mask).astype(BF16)
        within = dot(m, lane_incl)
        tot = dot(m, ones)
        before = dot(blk_before, tot.astype(BF16))
        return within, tot, before

    w_eq, _, b_eq = ranks(eq)
    sel = gt | (eq & (w_eq + b_eq <= need))
    within, tot, before = ranks(sel)
    pos_ref[0] = jnp.where(sel, within + before - 1.0, -1.0)
    off_ref[0] = before.T[0:SUBLANE, :].astype(jnp.int32)

    within_t = within.T.astype(BF16)
    sel_t = one_hot(sel).T.astype(BF16)
    x_t = x.T
    x_parts = []
    for _ in range(3):
        part = x_t.astype(BF16)
        x_parts.append(part)
        x_t = x_t - part.astype(F32)
    after = before + tot
    blk_id = lax.broadcasted_iota(jnp.int32, (nb, LANE), 0).astype(F32)
    lane_id = lax.broadcasted_iota(jnp.int32, (LANE, LANE), 0).astype(F32)
    for c in range(cap // LANE):
        j = (lax.broadcasted_iota(jnp.int32, (1, LANE), 1) + c * LANE).astype(F32)
        holds = one_hot((before <= j) & (j < after))
        local = j - jnp.sum(holds * before, axis=0, keepdims=True) + 1.0
        blk = jnp.sum(holds * blk_id, axis=0, keepdims=True)
        hb = holds.astype(BF16)
        cnt_in_blk = dot(within_t, hb)
        sel_in_blk = dot(sel_t, hb)
        hit = one_hot((sel_in_blk > 0.5) & (cnt_in_blk == local))
        lane_of = jnp.sum(hit * lane_id, axis=0, keepdims=True)
        idx_ref[0, c:c + 1, :] = (blk * LANE + lane_of).astype(jnp.int32)
        aff_in_blk = dot(x_parts[0], hb) + dot(x_parts[1], hb) + dot(x_parts[2], hb)
        gate_ref[0, c:c + 1, :] = jnp.sum(hit * aff_in_blk, axis=0, keepdims=True)


def _route(aff3, cap):
    e, nb, _ = aff3.shape
    blk = lambda rows: pl.BlockSpec((1, rows, LANE), lambda i: (i, 0, 0))
    return pl.pallas_call(
        functools.partial(_route_kernel, cap=cap),
        grid=(e,),
        in_specs=[blk(nb)],
        out_specs=[blk(nb), blk(cap // LANE), pl.BlockSpec((1, SUBLANE, nb), lambda i: (i, 0, 0)),
                   blk(cap // LANE)],
        out_shape=[jax.ShapeDtypeStruct((e, nb, LANE), F32),
                   jax.ShapeDtypeStruct((e, cap // LANE, LANE), jnp.int32),
                   jax.ShapeDtypeStruct((e, SUBLANE, nb), jnp.int32),
                   jax.ShapeDtypeStruct((e, cap // LANE, LANE), F32)],
        compiler_params=_params("parallel"),
        name="route",
    )(aff3)


def _gather_rows(idx, pieces, n):
    return (jnp.arange(pieces, dtype=idx.dtype)[:, None] * n + idx[None, :]).reshape(1, -1)


def _sc_gather(x, rows):
    pieces, n, d = x.shape
    x = x.reshape(pieces * n, d)
    m = rows.shape[1]
    mesh = plsc.VectorSubcoreMesh(core_axis_name="c", subcore_axis_name="s")

    @pl.kernel(out_type=jax.ShapeDtypeStruct((m, d), x.dtype), mesh=mesh, scratch_types=[])
    def gather_kernel(x_hbm, i_hbm, o_hbm):
        def body(i_vmem, o_vmem):
            pltpu.sync_copy(x_hbm.at[i_vmem.at[0]], o_vmem)

        pltpu.emit_pipeline(
            body,
            grid=(m // GATHER_WINDOW,),
            in_specs=[pl.BlockSpec((1, GATHER_WINDOW), lambda i: (0, i))],
            out_specs=[pl.BlockSpec((GATHER_WINDOW, d), lambda i: (i, 0))],
            core_axis_name=("c", "s"),
            dimension_semantics=(pltpu.PARALLEL,),
        )(i_hbm, o_hbm)

    return gather_kernel(x, rows).reshape(pieces, m // pieces, d)


def _ffn_kernel(x_ref, gate_ref, wg_ref, wu_ref, wd_ref, ye_ref, xe_ref, acc_ref):
    f = pl.program_id(2)
    pieces, _, piece = x_ref.shape

    @pl.when(f == 0)
    def _():
        acc_ref[...] = jnp.zeros_like(acc_ref)
        for q in range(pieces):
            xe_ref[:, q * piece:(q + 1) * piece] = x_ref[q].astype(BF16)

    xe = xe_ref[...]
    a = jnp.dot(xe, wg_ref[0].astype(BF16), preferred_element_type=F32)
    b = jnp.dot(xe, wu_ref[0].astype(BF16), preferred_element_type=F32)
    h = (a * jax.nn.sigmoid(a) * b).astype(BF16)
    acc_ref[...] += jnp.dot(h, wd_ref[0].astype(BF16), preferred_element_type=F32)

    @pl.when(f == pl.num_programs(2) - 1)
    def _():
        for c in range(gate_ref.shape[1]):
            rows = slice(c * LANE, (c + 1) * LANE)
            gate = jnp.broadcast_to(gate_ref[0, c:c + 1, :], (LANE, LANE)).T
            ye_ref[0, rows, :] = (acc_ref[rows, :] * jnp.concatenate([gate] * (acc_ref.shape[1] // LANE), axis=1)
                                  ).astype(ye_ref.dtype)


def _expert_ffn(xe, gate, w_gate, w_up, w_down, tm=2048, tf=256):
    e = w_gate.shape[0]
    pieces, rows, piece = xe.shape
    cap, d = rows // e, piece * pieces
    dff = w_gate.shape[-1]
    tiles = cap // tm
    return pl.pallas_call(
        _ffn_kernel,
        grid=(e, tiles, dff // tf),
        in_specs=[pl.BlockSpec((pieces, tm, piece), lambda i, r, f: (0, i * tiles + r, 0)),
                  pl.BlockSpec((1, tm // LANE, LANE), lambda i, r, f: (i, r, 0)),
                  pl.BlockSpec((1, d, tf), lambda i, r, f: (i, 0, f)),
                  pl.BlockSpec((1, d, tf), lambda i, r, f: (i, 0, f)),
                  pl.BlockSpec((1, tf, d), lambda i, r, f: (i, f, 0))],
        out_specs=pl.BlockSpec((1, tm, d), lambda i, r, f: (i, r, 0)),
        out_shape=jax.ShapeDtypeStruct((e, cap, d), BF16),
        scratch_shapes=[pltpu.VMEM((tm, d), BF16), pltpu.VMEM((tm, d), F32)],
        compiler_params=_params("parallel", "parallel", "arbitrary"),
        name="expert_ffn",
    )(xe, gate, w_gate, w_up, w_down)


COMBINE_TAIL = 16


def _combine_kernel(r0_ref, spill_ref, x1_ref, pos_ref, g_ref, *rest):
    ye_refs, y_ref = rest[:N_EXPERTS], rest[N_EXPERTS]
    i = pl.program_id(0)
    nb = pl.num_programs(0)
    t, d = x1_ref.shape
    pos_t = jnp.concatenate([pos_ref[:, 0, :], jnp.zeros((LANE - N_EXPERTS, LANE), F32)], axis=0).T
    col = lax.broadcasted_iota(jnp.int32, (t, LANE), 1)

    def gathered(rows, shift):
        out = jnp.zeros((t, d), F32)
        for e0 in range(0, N_EXPERTS, 2):
            picks = []
            for e in (e0, e0 + 1):
                first = r0_ref[e * nb + i] + shift
                slot = jnp.broadcast_to(pos_t[:, e:e + 1], (t, LANE))
                picks.append(jnp.where(slot == (col + first).astype(F32), 1.0, 0.0).astype(BF16))
            out = out + jnp.dot(jnp.concatenate(picks, axis=1), jnp.concatenate([rows(e0), rows(e0 + 1)], axis=0),
                                preferred_element_type=F32)
        return out

    def finish(x):
        y_ref[...] = x * lax.rsqrt(jnp.mean(x * x, axis=-1, keepdims=True) + EPS) * g_ref[...]

    acc = x1_ref[...] + gathered(lambda e: ye_refs[e][:LANE, :], 0)
    finish(acc)

    @pl.when(spill_ref[i] != 0)
    def _():
        pad = jnp.zeros((LANE - COMBINE_TAIL, d), BF16)
        finish(acc + gathered(lambda e: jnp.concatenate([ye_refs[e][LANE:, :], pad], axis=0), LANE))


def _combine(x1, pos3, off, ye, final_g):
    n, d = x1.shape
    nb = n // LANE
    cap = ye.shape[1]
    first = off[:, 0, :]
    end = jnp.concatenate([first[:, 1:], jnp.full((N_EXPERTS, 1), cap, first.dtype)], axis=1)
    r0 = jnp.minimum(first // 16 * 16, cap - LANE - COMBINE_TAIL)
    spill = jnp.any(end > r0 + LANE, axis=0).astype(jnp.int32)
    pos4 = pos3.reshape(N_EXPERTS, nb, 1, LANE)
    window = lambda e, rows, shift: pl.BlockSpec(
        (None, pl.Element(rows), pl.Element(d)),
        lambda i, r0_ref, spill_ref: (e, pl.multiple_of(r0_ref[e * nb + i] + shift, 16), 0))
    grid_spec = pltpu.PrefetchScalarGridSpec(
        num_scalar_prefetch=2,
        grid=(nb,),
        in_specs=[pl.BlockSpec((LANE, d), lambda i, r, s: (i, 0)),
                  pl.BlockSpec((N_EXPERTS, None, 1, LANE), lambda i, r, s: (0, i, 0, 0)),
                  pl.BlockSpec((1, d), lambda i, r, s: (0, 0))]
                 + [window(e, LANE + COMBINE_TAIL, 0) for e in range(N_EXPERTS)],
        out_specs=pl.BlockSpec((LANE, d), lambda i, r, s: (i, 0)),
    )
    return pl.pallas_call(
        _combine_kernel,
        grid_spec=grid_spec,
        out_shape=jax.ShapeDtypeStruct((n, d), F32),
        compiler_params=_params("parallel"),
        name="moe_combine",
    )(r0.reshape(-1).astype(jnp.int32), spill, x1, pos4, final_g, *([ye] * N_EXPERTS))


def _lane_bcast(a, col, width=LANE):
    return jnp.broadcast_to(a[:, col:col + 1], (a.shape[0], width))


DN_TILE = 1024
INV_STEPS = 5
assert 2 ** (INV_STEPS + 1) == CHUNK
DN_UNROLL = 64
DN_BUILD_UNROLL = 4


def _dn_scan_kernel(q_ref, k_ref, v_ref, kt_ref, ba_ref, alog_ref, dtb_ref, o_ref,
                    s_ref, xbuf, tbuf, abuf, rbuf, uwbuf, qgbuf, kdbuf, lbuf, nbuf, obuf, ecbuf, *, reverse):
    @pl.when(pl.program_id(1) == 0)
    def _():
        s_ref[...] = jnp.zeros_like(s_ref)

    tile = q_ref.shape[1]
    nc = tile // CHUNK
    n_prob = nc * DN_HEADS
    unroll = min(DN_UNROLL, n_prob)
    beta_col = DN_HEADS if reverse else 0
    a_col = 2 * DN_HEADS + (DN_HEADS if reverse else 0)
    ri = lax.broadcasted_iota(jnp.int32, (CHUNK, CHUNK), 0)
    ci = lax.broadcasted_iota(jnp.int32, (CHUNK, CHUNK), 1)
    tri = (ri <= ci) if reverse else (ri >= ci)
    strict = (ri < ci) if reverse else (ri > ci)
    eye = jnp.where(ri == ci, 1.0, 0.0)
    row = lax.broadcasted_iota(jnp.int32, (CHUNK, LANE), 0)
    a_scale = jnp.exp(alog_ref[...])
    dtb = dtb_ref[...]
    dot = lambda a, b: jnp.dot(a, b, preferred_element_type=F32)

    def build(c, carry):
        rows = pl.ds(pl.multiple_of(c * CHUNK, CHUNK), CHUNK)
        raw = ba_ref[0, rows, :]
        beta_all = jax.nn.sigmoid(raw)
        xs = raw + dtb
        g = -a_scale * (jnp.maximum(xs, 0.0) + jnp.log(1.0 + jnp.exp(-jnp.abs(xs))))
        gs = g
        sh = 1
        while sh < CHUNK:
            if reverse:
                gs = gs + jnp.where(row < CHUNK - sh, pltpu.roll(gs, CHUNK - sh, axis=0), 0.0)
            else:
                gs = gs + jnp.where(row >= sh, pltpu.roll(gs, sh, axis=0), 0.0)
            sh *= 2
        gs_t = gs.T
        g_last = gs[0:1, :] if reverse else gs[CHUNK - 1:CHUNK, :]
        eg_all = jnp.exp(gs)
        ek_t = jnp.exp(g_last.T - gs_t)
        ecbuf[c] = jnp.broadcast_to(jnp.exp(g_last), (SUBLANE, LANE))
        for h in range(DN_HEADS):
            p = c * DN_HEADS + h
            hs = slice(h * DN_HEAD_DIM, (h + 1) * DN_HEAD_DIM)
            q = q_ref[0, rows, hs].astype(F32)
            k = k_ref[0, rows, hs]
            v = v_ref[0, rows, hs].astype(F32)
            beta = _lane_bcast(beta_all, beta_col + h)
            eg = _lane_bcast(eg_all, a_col + h)
            diff = _lane_bcast(gs, a_col + h, CHUNK) - gs_t[a_col + h:a_col + h + 1, :]
            decay = jnp.where(tri, jnp.exp(jnp.where(tri, diff, 0.0)), 0.0)
            kb = k.astype(F32) * beta
            kq = lax.dot_general(jnp.concatenate([kb, q], axis=0).astype(BF16), k,
                                 (((1,), (1,)), ((), ())), preferred_element_type=F32)
            lower = jnp.where(strict, kq[:CHUNK] * decay, 0.0)
            xbuf[p] = (-lower).astype(BF16)
            tbuf[p] = eye - lower
            abuf[p] = (kq[CHUNK:] * decay).astype(BF16)
            rbuf[p] = jnp.concatenate([v * beta, kb * eg], axis=1).astype(BF16)
            qgbuf[p] = q * eg
            kdbuf[p] = (kt_ref[0, c, hs, :].astype(F32)
                        * ek_t[a_col + h:a_col + h + 1, :]).astype(BF16)
        return carry

    lax.fori_loop(0, nc, build, 0, unroll=min(DN_BUILD_UNROLL, nc))

    for step in range(INV_STEPS + 1):
        def double(p, carry, first=(step == 0), last=(step == INV_STEPS)):
            x = xbuf[p]
            if not first:
                t = tbuf[p]
                tbuf[p] = t + dot(t.astype(BF16), x)
            if not last:
                xbuf[p] = dot(x, x).astype(BF16)
            return carry
        lax.fori_loop(0, n_prob, double, 0, unroll=unroll)

    def solve(p, carry):
        uwbuf[p] = dot(tbuf[p].astype(BF16), rbuf[p]).astype(BF16)
        return carry

    lax.fori_loop(0, n_prob, solve, 0, unroll=unroll)

    def fold(p, carry):
        uw = uwbuf[p]
        a_uw = dot(abuf[p], uw)
        k_uw = dot(kdbuf[p], uw)
        obuf[p] = a_uw[:, :DN_HEAD_DIM]
        nbuf[p] = k_uw[:, :DN_HEAD_DIM]
        lbuf[p, :DN_HEAD_DIM, :] = k_uw[:, DN_HEAD_DIM:].astype(BF16)
        lbuf[p, DN_HEAD_DIM:, :] = (qgbuf[p] - a_uw[:, DN_HEAD_DIM:]).astype(BF16)
        return carry

    lax.fori_loop(0, n_prob, fold, 0, unroll=unroll)

    def scan(jc, carry):
        c = (nc - 1 - jc) if reverse else jc
        rows = pl.ds(pl.multiple_of(c * CHUNK, CHUNK), CHUNK)
        ec_all = ecbuf[c]
        for h in range(DN_HEADS):
            p = c * DN_HEADS + h
            state = s_ref[h]
            r = dot(lbuf[p], state.astype(BF16))
            s_ref[h] = state * _lane_bcast(ec_all[0:1], a_col + h) - r[:DN_HEAD_DIM] + nbuf[p]
            o = r[DN_HEAD_DIM:] + obuf[p]
            o_ref[0, rows, h * DN_HEAD_DIM:(h + 1) * DN_HEAD_DIM] = o.astype(o_ref.dtype)
        return carry

    lax.fori_loop(0, nc, scan, 0)


def _dn_scan(q, k, v, kt, ba, a_log_row, dt_bias_row, reverse, tile=DN_TILE):
    b, s, _ = q.shape
    tile = min(tile, s)
    nt = s // tile
    nc = tile // CHUNK
    n_prob = nc * DN_HEADS
    tmap = (lambda bi, ti: (bi, nt - 1 - ti, 0)) if reverse else (lambda bi, ti: (bi, ti, 0))
    wide = pl.BlockSpec((1, tile, DN_WIDTH), tmap)
    par = pl.BlockSpec((1, LANE), lambda bi, ti: (0, 0))
    return pl.pallas_call(
        functools.partial(_dn_scan_kernel, reverse=reverse),
        grid=(b, nt),
        in_specs=[wide, wide, wide,
                  pl.BlockSpec((1, nc, DN_WIDTH, CHUNK), lambda bi, ti: tmap(bi, ti) + (0,)),
                  pl.BlockSpec((1, tile, LANE), tmap), par, par],
        out_specs=wide,
        out_shape=jax.ShapeDtypeStruct((b, s, DN_WIDTH), BF16),
        scratch_shapes=[pltpu.VMEM((DN_HEADS, DN_HEAD_DIM, DN_HEAD_DIM), F32),
                        pltpu.VMEM((n_prob, CHUNK, CHUNK), BF16),
                        pltpu.VMEM((n_prob, CHUNK, CHUNK), F32),
                        pltpu.VMEM((n_prob, CHUNK, CHUNK), BF16),
                        pltpu.VMEM((n_prob, CHUNK, 2 * DN_HEAD_DIM), BF16),
                        pltpu.VMEM((n_prob, CHUNK, 2 * DN_HEAD_DIM), BF16),
                        pltpu.VMEM((n_prob, CHUNK, DN_HEAD_DIM), F32),
                        pltpu.VMEM((n_prob, DN_HEAD_DIM, CHUNK), BF16),
                        pltpu.VMEM((n_prob, DN_HEAD_DIM + CHUNK, DN_HEAD_DIM), BF16),
                        pltpu.VMEM((n_prob, DN_HEAD_DIM, DN_HEAD_DIM), F32),
                        pltpu.VMEM((n_prob, CHUNK, DN_HEAD_DIM), F32),
                        pltpu.VMEM((nc, SUBLANE, LANE), F32)],
        compiler_params=_params("parallel", "arbitrary"),
        name="dn_scan_bwd" if reverse else "dn_scan_fwd",
    )(q, k, v, kt, ba, a_log_row, dt_bias_row)


def _gate_rows(a_log_fwd, a_log_bwd, dt_bias_fwd, dt_bias_bwd):
    pad = lambda f, b: jnp.pad(jnp.concatenate([f, b]).astype(F32), (2 * DN_HEADS, LANE - 4 * DN_HEADS))[None]
    return pad(a_log_fwd, a_log_bwd), pad(dt_bias_fwd, dt_bias_bwd)


def _deltanet(q, k, v, kt, ba, a_log_fwd, dt_bias_fwd, a_log_bwd, dt_bias_bwd):
    alog, dtb = _gate_rows(a_log_fwd, a_log_bwd, dt_bias_fwd, dt_bias_bwd)
    o_f = _dn_scan(q, k, v, kt, ba, alog, dtb, reverse=False)
    o_b = _dn_scan(q, k, v, kt, ba, alog, dtb, reverse=True)
    return o_f, o_b


def _t5_bucket(rel):
    nb = NUM_BUCKETS // 2
    ret = jnp.where(rel > 0, nb, 0)
    n = jnp.abs(rel)
    max_exact = nb // 2
    nf = jnp.maximum(n, 1).astype(F32)
    large = max_exact + (jnp.log(nf / max_exact) / math.log(MAX_DISTANCE / max_exact)
                         * (nb - max_exact)).astype(jnp.int32)
    large = jnp.minimum(large, nb - 1)
    return ret + jnp.where(n < max_exact, n, large)


ATT_TILE = 2048
ATT_HALO = BAND * max(d for _, d in DILATED_PATTERNS)
QBLK = 2 * BAND
KBLK = QBLK + 2 * BAND


def _attn_bias_table(rel_bias):
    delta = (jnp.arange(KBLK) - BAND)[None, :] - jnp.arange(QBLK)[:, None]
    tabs = []
    for window, dil in DILATED_PATTERNS:
        half = window // (2 * dil)
        onehot = (_t5_bucket(delta * dil)[..., None] == jnp.arange(NUM_BUCKETS)).astype(F32)
        bias = jnp.einsum('qkb,bh->hqk', onehot, rel_bias.astype(F32),
                          precision=lax.Precision.HIGHEST)
        tabs.append(jnp.where((jnp.abs(delta) <= half)[None], bias, NEG))
    return jnp.stack(tabs)


def _attn_kernel(q_ref, kp_ref, kc_ref, kn_ref, vp_ref, vc_ref, vn_ref, bias_ref, o_ref,
                 qd_a, qd_b, kf, vf, kd, vd, od, md, ld, o_sc, m_sc, l_sc, *, seq_len):
    t = pl.program_id(1)
    lane = lax.broadcasted_iota(jnp.int32, (QBLK, LANE), 1)
    low = lane < ATT_HEAD_DIM
    scale = ATT_HEAD_DIM ** -0.5
    kcol = lax.broadcasted_iota(jnp.int32, (1, KBLK), 1)
    ones = jnp.ones((KBLK, LANE), BF16)

    for p, (window, dil) in enumerate(DILATED_PATTERNS):
        lt = ATT_TILE // dil
        lh = ATT_HALO // dil
        span = lt + 2 * lh
        nblk = lt // QBLK
        n_pos = seq_len // dil
        qd, q_prev = (qd_a, qd_b) if p % 2 == 0 else (qd_b, qd_a)
        prev_dil = DILATED_PATTERNS[p - 1][1] if p > 0 else 1
        nested = p > 0 and prev_dil > 1 and dil % prev_dil == 0
        step = dil // prev_dil
        keep_f32 = p + 1 < len(DILATED_PATTERNS) and dil > 1 and DILATED_PATTERNS[p + 1][1] % dil == 0
        for r in range(dil):
            if nested:
                rp, a = r % prev_dil, r // prev_dil
                lt_p, span_p = ATT_TILE // prev_dil, (ATT_TILE + 2 * ATT_HALO) // prev_dil
                qd[pl.ds(r * lt, lt), :] = q_prev[pl.ds(rp * lt_p + a, lt, stride=step), :]
                src = pl.ds(rp * span_p + a, span, stride=step)
                kd[pl.ds(r * span, span), :] = kf[src, :].astype(BF16)
                vd[pl.ds(r * span, span), :] = vf[src, :].astype(BF16)
                continue
            sl = lambda n: pl.ds(r, n, stride=dil) if dil > 1 else pl.ds(0, n)
            qd[pl.ds(r * lt, lt), :] = q_ref[0, sl(lt), :] * scale
            off = r * span
            for (kr, vr, n) in ((kp_ref, vp_ref, lh), (kc_ref, vc_ref, lt), (kn_ref, vn_ref, lh)):
                kx, vx = kr[0, sl(n), :], vr[0, sl(n), :]
                if keep_f32:
                    kf[pl.ds(off, n), :] = kx
                    vf[pl.ds(off, n), :] = vx
                kd[pl.ds(off, n), :] = kx.astype(BF16)
                vd[pl.ds(off, n), :] = vx.astype(BF16)
                off += n


        for i in range(dil * nblk):
            r, j = divmod(i, nblk)
            q0 = r * lt + j * QBLK
            k0 = r * span + lh + j * QBLK - BAND
            q = qd[pl.ds(q0, QBLK), :]
            k = kd[pl.ds(k0, KBLK), :]
            v1 = jnp.concatenate([vd[pl.ds(k0, KBLK), :], ones], axis=1)
            edge = j == 0 or j == nblk - 1
            if edge:
                pos = t * lt + j * QBLK - BAND + kcol
                valid = (pos >= 0) & (pos < n_pos)
            res = []
            for h in range(2):
                qh = jnp.where(low, q, 0.0) if h == 0 else jnp.where(low, 0.0, q)
                s = lax.dot_general(qh.astype(BF16), k, (((1,), (1,)), ((), ())),
                                    preferred_element_type=F32) + bias_ref[p, h]
                if edge:
                    s = jnp.where(valid, s, NEG)
                m = jnp.max(s, axis=-1, keepdims=True)
                e = jnp.exp(s - m).astype(BF16)
                res.append((jnp.dot(e, v1, preferred_element_type=F32), m))
            (pv0, m0), (pv1, m1) = res
            od[pl.ds(q0, QBLK), :] = jnp.where(low, pv0[:, :LANE], pv1[:, :LANE])
            ld[pl.ds(q0, QBLK), :] = jnp.where(low, pv0[:, LANE:], pv1[:, LANE:])
            md[pl.ds(q0, QBLK), :] = jnp.where(low, m0, m1)

        for r in range(dil):
            sl = pl.ds(r, lt, stride=dil) if dil > 1 else pl.ds(0, lt)
            src = pl.ds(r * lt, lt)
            o_sc[p, sl, :] = od[src, :]
            m_sc[p, sl, :] = md[src, :]
            l_sc[p, sl, :] = ld[src, :]

    n_p = len(DILATED_PATTERNS)
    mx = m_sc[0]
    for p in range(1, n_p):
        mx = jnp.maximum(mx, m_sc[p])
    num = jnp.zeros((ATT_TILE, LANE), F32)
    den = jnp.zeros((ATT_TILE, LANE), F32)
    for p in range(n_p):
        w = jnp.exp(m_sc[p] - mx)
        num += o_sc[p] * w
        den += l_sc[p] * w
    o_ref[0] = (num / den).astype(o_ref.dtype)


def _dilated_attention(att, bias):
    b, s, _ = att.shape
    assert s % ATT_TILE == 0
    nt = s // ATT_TILE
    nh = s // ATT_HALO
    per = ATT_TILE // ATT_HALO
    hp = ATT_WIDTH // LANE
    cur = lambda col0: pl.BlockSpec((1, ATT_TILE, LANE), lambda bi, ti, hi: (bi, ti, col0 + hi))
    prev = lambda col0: pl.BlockSpec(
        (1, ATT_HALO, LANE), lambda bi, ti, hi: (bi, jnp.maximum(ti * per - 1, 0), col0 + hi))
    nxt = lambda col0: pl.BlockSpec(
        (1, ATT_HALO, LANE), lambda bi, ti, hi: (bi, jnp.minimum((ti + 1) * per, nh - 1), col0 + hi))
    n_p = len(DILATED_PATTERNS)
    tile_f32 = pltpu.VMEM((ATT_TILE, LANE), F32)
    span_bf16 = pltpu.VMEM((ATT_TILE + 2 * ATT_HALO, LANE), BF16)
    span_f32 = pltpu.VMEM((ATT_TILE + 2 * ATT_HALO, LANE), F32)
    per_pattern = pltpu.VMEM((n_p, ATT_TILE, LANE), F32)
    return pl.pallas_call(
        functools.partial(_attn_kernel, seq_len=s),
        grid=(b, nt, hp),
        in_specs=[cur(0), prev(hp), cur(hp), nxt(hp), prev(2 * hp), cur(2 * hp), nxt(2 * hp),
                  pl.BlockSpec((n_p, 2, QBLK, KBLK), lambda bi, ti, hi: (0, hi, 0, 0))],
        out_specs=pl.BlockSpec((1, ATT_TILE, LANE), lambda bi, ti, hi: (bi, ti, hi)),
        out_shape=jax.ShapeDtypeStruct((b, s, ATT_WIDTH), BF16),
        scratch_shapes=[tile_f32, tile_f32, span_f32, span_f32, span_bf16, span_bf16, tile_f32, tile_f32, tile_f32,
                        per_pattern, per_pattern, per_pattern],
        compiler_params=_params("parallel", "parallel", "parallel"),
        name="dilated_attention",
    )(att, att, att, att, att, att, att, bias)


def _mix(x, att_bias, norm1_g, w_in, conv_w, a_log_fwd, dt_bias_fwd, a_log_bwd, dt_bias_bwd,
         dn_norm_g, w_out, norm2_g, w_router):
    bsz, s, d = x.shape
    xf = x.reshape(bsz * s, d)
    q, k, v, kt, z, ba, att = _in_proj(xf, norm1_g[None], w_in, conv_w, s)
    shp = lambda a: a.reshape(bsz, s, a.shape[-1])
    o_f, o_b = _deltanet(shp(q), shp(k), shp(v), kt.reshape(bsz, s // CHUNK, DN_WIDTH, CHUNK), shp(ba),
                         a_log_fwd, dt_bias_fwd, a_log_bwd, dt_bias_bwd)
    att_out = _dilated_attention(shp(att), att_bias)
    flat = lambda a: a.reshape(bsz * s, a.shape[-1])
    return _out_proj(flat(o_f), flat(o_b), z, dn_norm_g[None], flat(att_out), xf, w_out, norm2_g[None],
                     w_router)


def kernel(x_prompt, x_sample, rel_bias, norm1_g, w_in, conv_w, a_log_fwd, dt_bias_fwd, a_log_bwd,
           dt_bias_bwd, dn_norm_g, w_out, norm2_g, w_router, w_gate, w_up, w_down, final_norm_g):
    groups = [x_prompt, x_sample]
    att_bias = _attn_bias_table(rel_bias)
    parts, xes = [], []
    for k in range(len(groups)):
        x = groups[k]
        x1, h2, afft = _mix(x, att_bias, norm1_g[0], w_in[0], conv_w[0], a_log_fwd[0], dt_bias_fwd[0],
                            a_log_bwd[0], dt_bias_bwd[0], dn_norm_g[0], w_out[0], norm2_g[0], w_router[0])
        n_tok = x1.shape[0]
        cap = CAPACITY_FACTOR * n_tok // N_EXPERTS
        pos, idx, off, gate = _route(afft, cap)
        rows = _gather_rows(idx.reshape(-1), GATHER_SPLIT, n_tok)
        if k + 1 < len(groups):
            rows, groups[k + 1] = lax.optimization_barrier((rows, groups[k + 1]))
        else:
            rows, xes[0] = lax.optimization_barrier((rows, xes[0]))
        xes.append(_sc_gather(h2, rows))
        parts.append((x, x1, pos, off, gate))
    outs = []
    for k, (x, x1, pos, off, gate) in enumerate(parts):
        ye = _expert_ffn(xes[k], gate, w_gate[0], w_up[0], w_down[0])
        if k + 1 < len(parts):
            xes[k + 1], ye = lax.optimization_barrier((xes[k + 1], ye))
        outs.append(_combine(x1, pos, off, ye, final_norm_g[None]).reshape(groups[k].shape))
    return tuple(outs)
```

```python
import functools
import math

import jax
import jax.numpy as jnp
from jax import lax
from jax.experimental import pallas as pl
from jax.experimental.pallas import tpu as pltpu
from jax.experimental.pallas import tpu_sc as plsc

D_MODEL = 1024
DN_HEADS = 4
DN_HEAD_DIM = 128
DN_WIDTH = DN_HEADS * DN_HEAD_DIM
ATT_HEADS = 8
ATT_HEAD_DIM = 64
ATT_WIDTH = ATT_HEADS * ATT_HEAD_DIM
CONV_K = 3
CHUNK = 64
DILATED_PATTERNS = ((128, 1), (512, 4), (2048, 16))
BAND = 64
NUM_BUCKETS = 32
MAX_DISTANCE = 1024
N_EXPERTS = 16
CAPACITY_FACTOR = 2
EXPERT_D_FF = 2816
EPS = 1e-6
NEG = -1e30
GATE_COLS = 4 * DN_HEADS
LANE = 128
SUBLANE = 8
GATHER_WINDOW = 128
GATHER_SPLIT = 4

F32 = jnp.float32
BF16 = jnp.bfloat16

VMEM_LIMIT = 56 * 1024 * 1024


def _params(*sem):
    return pltpu.CompilerParams(dimension_semantics=sem, vmem_limit_bytes=VMEM_LIMIT)


def _in_proj_kernel(x_ref, xp_ref, xn_ref, g_ref, wqkv_ref, wz_ref, wba_ref, watt_ref, cw_ref,
                    q_ref, k_ref, v_ref, kt_ref, z_ref, ba_ref, att_ref, *, tiles_per_seq):
    t = pl.program_id(0) % tiles_per_seq
    tm = x_ref.shape[0]
    x = jnp.concatenate([xp_ref[...], x_ref[...], xn_ref[...]], axis=0)
    h = x * lax.rsqrt(jnp.mean(x * x, axis=-1, keepdims=True) + EPS) * g_ref[...]
    hb = h[SUBLANE:SUBLANE + tm].astype(BF16)
    z_ref[...] = jnp.dot(hb, wz_ref[...], preferred_element_type=F32)
    ba_ref[...] = jnp.dot(hb, wba_ref[...], preferred_element_type=F32)
    att_ref[...] = jnp.dot(hb, watt_ref[...], preferred_element_type=F32)

    p = jnp.dot(h.astype(BF16), wqkv_ref[...], preferred_element_type=F32)
    rows = p.shape[0]
    row = lax.broadcasted_iota(jnp.int32, (tm, 1), 0)
    mid = lambda a: a[SUBLANE:SUBLANE + tm]
    p_prev = jnp.where((row == 0) & (t == 0), 0.0, mid(pltpu.roll(p, 1, axis=0)))
    p_next = jnp.where((row == tm - 1) & (t == tiles_per_seq - 1), 0.0, mid(pltpu.roll(p, rows - 1, axis=0)))
    y = cw_ref[0:1, :] * p_prev + cw_ref[1:2, :] * mid(p) + cw_ref[2:3, :] * p_next
    y = y * jax.nn.sigmoid(y)
    for hd in range(DN_HEADS):
        for part, ref, scale in ((0, q_ref, DN_HEAD_DIM ** -0.5), (1, k_ref, 1.0)):
            c0 = part * DN_WIDTH + hd * DN_HEAD_DIM
            a = y[:, c0:c0 + DN_HEAD_DIM]
            a = a * (lax.rsqrt(jnp.sum(a * a, axis=-1, keepdims=True) + EPS) * scale)
            ref[:, hd * DN_HEAD_DIM:(hd + 1) * DN_HEAD_DIM] = a.astype(ref.dtype)
            if part == 1:
                for j in range(tm // CHUNK):
                    kt_ref[j, hd * DN_HEAD_DIM:(hd + 1) * DN_HEAD_DIM, :] = (
                        a[j * CHUNK:(j + 1) * CHUNK, :].T.astype(kt_ref.dtype))
    v_ref[...] = y[:, 2 * DN_WIDTH:].astype(v_ref.dtype)


def _in_proj(x, g, w_in, conv_w, seq_len, tm=512):
    n = x.shape[0]
    o_qkv = 3 * DN_WIDTH
    o_z = o_qkv + DN_WIDTH
    o_ba = o_z + GATE_COLS
    wb = w_in.astype(BF16)
    wqkv = wb[:, :o_qkv]
    wz = wb[:, o_qkv:o_z]
    wba = jnp.pad(wb[:, o_z:o_ba], ((0, 0), (0, LANE - GATE_COLS)))
    watt = wb[:, o_ba:]
    per = tm // SUBLANE
    nsub = n // SUBLANE
    full = lambda a: pl.BlockSpec(a.shape, lambda i: (0, 0))
    row = lambda w: pl.BlockSpec((tm, w), lambda i: (i, 0))
    dn = jax.ShapeDtypeStruct((n, DN_WIDTH), BF16)
    return pl.pallas_call(
        functools.partial(_in_proj_kernel, tiles_per_seq=seq_len // tm),
        grid=(n // tm,),
        in_specs=[row(D_MODEL),
                  pl.BlockSpec((SUBLANE, D_MODEL), lambda i: (jnp.maximum(i * per - 1, 0), 0)),
                  pl.BlockSpec((SUBLANE, D_MODEL), lambda i: (jnp.minimum((i + 1) * per, nsub - 1), 0)),
                  full(g), full(wqkv), full(wz), full(wba), full(watt), full(conv_w)],
        out_specs=[row(DN_WIDTH), row(DN_WIDTH), row(DN_WIDTH),
                   pl.BlockSpec((tm // CHUNK, DN_WIDTH, CHUNK), lambda i: (i, 0, 0)),
                   row(DN_WIDTH), row(LANE), row(3 * ATT_WIDTH)],
        out_shape=[dn, dn, dn, jax.ShapeDtypeStruct((n // CHUNK, DN_WIDTH, CHUNK), BF16),
                   jax.ShapeDtypeStruct((n, DN_WIDTH), F32),
                   jax.ShapeDtypeStruct((n, LANE), F32),
                   jax.ShapeDtypeStruct((n, 3 * ATT_WIDTH), F32)],
        compiler_params=_params("parallel"),
        name="in_proj",
    )(x, x, x, g, wqkv, wz, wba, watt, conv_w)


def _split(a):
    hi = a.astype(BF16)
    return hi, (a - hi.astype(F32)).astype(BF16)


def _out_proj_kernel(of_ref, ob_ref, z_ref, gdn_ref, att_ref, x_ref, wdn_ref, watt_ref, g_ref, wr_ref,
                     x1_ref, h2_ref, afft_ref):
    dot = lambda a, b: jnp.dot(a, b, preferred_element_type=F32)
    o = of_ref[...].astype(F32) + ob_ref[...].astype(F32)
    z = z_ref[...]
    gated = []
    for h in range(DN_HEADS):
        hs = slice(h * DN_HEAD_DIM, (h + 1) * DN_HEAD_DIM)
        oh = o[:, hs]
        oh = oh * lax.rsqrt(jnp.mean(oh * oh, axis=-1, keepdims=True) + EPS) * gdn_ref[...]
        zh = z[:, hs]
        gated.append((oh * (zh * jax.nn.sigmoid(zh))).astype(BF16))
    dn = jnp.concatenate(gated, axis=1)
    x1 = x_ref[...] + dot(dn, wdn_ref[...]) + dot(att_ref[...], watt_ref[...])
    x1_ref[...] = x1
    h2 = x1 * lax.rsqrt(jnp.mean(x1 * x1, axis=-1, keepdims=True) + EPS) * g_ref[...]
    rows = h2.shape[0]
    piece = D_MODEL // GATHER_SPLIT
    for q in range(GATHER_SPLIT):
        h2_ref[q] = h2[:, q * piece:(q + 1) * piece]
    hh, hl = _split(h2)
    wh, wl = _split(wr_ref[...])
    logits = dot(hh, wh) + (dot(hh, wl) + dot(hl, wh))
    lane = lax.broadcasted_iota(jnp.int32, logits.shape, 1)
    logits = jnp.where(lane < N_EXPERTS, logits, NEG)
    p = jnp.exp(logits - jnp.max(logits, axis=-1, keepdims=True))
    aff = p / jnp.sum(p, axis=-1, keepdims=True)
    for j in range(rows // LANE):
        afft_ref[:, j, :] = aff[j * LANE:(j + 1) * LANE, :].T[:N_EXPERTS, :]


def _out_proj(o_f, o_b, z, dn_norm_g, att, x, w_out, g, w_router, tm=1024):
    n = x.shape[0]
    wb = w_out.astype(BF16)
    wdn, watt = wb[:DN_WIDTH], wb[DN_WIDTH:]
    wr = jnp.pad(w_router.astype(F32), ((0, 0), (0, LANE - N_EXPERTS)))
    full = lambda a: pl.BlockSpec(a.shape, lambda i: (0, 0))
    row = lambda w: pl.BlockSpec((tm, w), lambda i: (i, 0))
    return pl.pallas_call(
        _out_proj_kernel,
        grid=(n // tm,),
        in_specs=[row(DN_WIDTH), row(DN_WIDTH), row(DN_WIDTH), full(dn_norm_g), row(ATT_WIDTH),
                  row(D_MODEL), full(wdn), full(watt), full(g), full(wr)],
        out_specs=[row(D_MODEL),
                   pl.BlockSpec((GATHER_SPLIT, tm, D_MODEL // GATHER_SPLIT), lambda i: (0, i, 0)),
                   pl.BlockSpec((N_EXPERTS, tm // LANE, LANE), lambda i: (0, i, 0))],
        out_shape=[jax.ShapeDtypeStruct((n, D_MODEL), F32),
                   jax.ShapeDtypeStruct((GATHER_SPLIT, n, D_MODEL // GATHER_SPLIT), F32),
                   jax.ShapeDtypeStruct((N_EXPERTS, n // LANE, LANE), F32)],
        compiler_params=_params("parallel"),
        name="out_proj",
    )(o_f, o_b, z, dn_norm_g, att, x, wdn, watt, g, wr)


def _route_kernel(aff_ref, pos_ref, idx_ref, off_ref, gate_ref, *, cap):
    x = aff_ref[0]
    nb = x.shape[0]
    bits = pltpu.bitcast(x, jnp.int32)
    total = lambda m: jnp.sum(jnp.sum(m, axis=0, keepdims=True), axis=1, keepdims=True)
    dot = lambda a, b: jnp.dot(a, b, preferred_element_type=F32)
    one_hot = lambda m: jnp.where(m, 1.0, 0.0)

    def bit_step(i, prefix):
        cand = prefix | lax.shift_left(jnp.int32(1), 30 - i)
        cnt = total(one_hot(bits >= cand))
        return jnp.where(cnt >= cap, cand, prefix)

    thr = lax.fori_loop(0, 31, bit_step, jnp.zeros((1, 1), jnp.int32))
    gt = bits > thr
    eq = bits == thr
    need = cap - total(one_hot(gt))

    li = lax.broadcasted_iota(jnp.int32, (LANE, LANE), 0)
    lj = lax.broadcasted_iota(jnp.int32, (LANE, LANE), 1)
    lane_incl = one_hot(li <= lj).astype(BF16)
    ones = jnp.ones((LANE, LANE), BF16)
    bi = lax.broadcasted_iota(jnp.int32, (nb, nb), 0)
    bj = lax.broadcasted_iota(jnp.int32, (nb, nb), 1)
    blk_before = one_hot(bj < bi).astype(BF16)

    def ranks(mask):
        m = one_hot(mask).astype(BF16)
        within = dot(m, lane_incl)
        tot = dot(m, ones)
        before = dot(blk_before, tot.astype(BF16))
        return within, tot, before

    w_eq, _, b_eq = ranks(eq)
    sel = gt | (eq & (w_eq + b_eq <= need))
    within, tot, before = ranks(sel)
    pos_ref[0] = jnp.where(sel, within + before - 1.0, -1.0)
    off_ref[0] = before.T[0:SUBLANE, :].astype(jnp.int32)

    within_t = within.T.astype(BF16)
    sel_t = one_hot(sel).T.astype(BF16)
    x_t = x.T
    x_parts = []
    for _ in range(3):
        part = x_t.astype(BF16)
        x_parts.append(part)
        x_t = x_t - part.astype(F32)
    after = before + tot
    blk_id = lax.broadcasted_iota(jnp.int32, (nb, LANE), 0).astype(F32)
    lane_id = lax.broadcasted_iota(jnp.int32, (LANE, LANE), 0).astype(F32)
    for c in range(cap // LANE):
        j = (lax.broadcasted_iota(jnp.int32, (1, LANE), 1) + c * LANE).astype(F32)
        holds = one_hot((before <= j) & (j < after))
        local = j - jnp.sum(holds * before, axis=0, keepdims=True) + 1.0
        blk = jnp.sum(holds * blk_id, axis=0, keepdims=True)
        hb = holds.astype(BF16)
        cnt_in_blk = dot(within_t, hb)
        sel_in_blk = dot(sel_t, hb)
        hit = one_hot((sel_in_blk > 0.5) & (cnt_in_blk == local))
        lane_of = jnp.sum(hit * lane_id, axis=0, keepdims=True)
        idx_ref[0, c:c + 1, :] = (blk * LANE + lane_of).astype(jnp.int32)
        aff_in_blk = dot(x_parts[0], hb) + dot(x_parts[1], hb) + dot(x_parts[2], hb)
        gate_ref[0, c:c + 1, :] = jnp.sum(hit * aff_in_blk, axis=0, keepdims=True)


def _route(aff3, cap):
    e, nb, _ = aff3.shape
    blk = lambda rows: pl.BlockSpec((1, rows, LANE), lambda i: (i, 0, 0))
    return pl.pallas_call(
        functools.partial(_route_kernel, cap=cap),
        grid=(e,),
        in_specs=[blk(nb)],
        out_specs=[blk(nb), blk(cap // LANE), pl.BlockSpec((1, SUBLANE, nb), lambda i: (i, 0, 0)),
                   blk(cap // LANE)],
        out_shape=[jax.ShapeDtypeStruct((e, nb, LANE), F32),
                   jax.ShapeDtypeStruct((e, cap // LANE, LANE), jnp.int32),
                   jax.ShapeDtypeStruct((e, SUBLANE, nb), jnp.int32),
                   jax.ShapeDtypeStruct((e, cap // LANE, LANE), F32)],
        compiler_params=_params("parallel"),
        name="route",
    )(aff3)


def _gather_rows(idx, pieces, n):
    return (jnp.arange(pieces, dtype=idx.dtype)[:, None] * n + idx[None, :]).reshape(1, -1)


def _sc_gather(x, rows):
    pieces, n, d = x.shape
    x = x.reshape(pieces * n, d)
    m = rows.shape[1]
    mesh = plsc.VectorSubcoreMesh(core_axis_name="c", subcore_axis_name="s")

    @pl.kernel(out_type=jax.ShapeDtypeStruct((m, d), x.dtype), mesh=mesh, scratch_types=[])
    def gather_kernel(x_hbm, i_hbm, o_hbm):
        def body(i_vmem, o_vmem):
            pltpu.sync_copy(x_hbm.at[i_vmem.at[0]], o_vmem)

        pltpu.emit_pipeline(
            body,
            grid=(m // GATHER_WINDOW,),
            in_specs=[pl.BlockSpec((1, GATHER_WINDOW), lambda i: (0, i))],
            out_specs=[pl.BlockSpec((GATHER_WINDOW, d), lambda i: (i, 0))],
            core_axis_name=("c", "s"),
            dimension_semantics=(pltpu.PARALLEL,),
        )(i_hbm, o_hbm)

    return gather_kernel(x, rows).reshape(pieces, m // pieces, d)


def _ffn_kernel(x_ref, gate_ref, wg_ref, wu_ref, wd_ref, ye_ref, xe_ref, acc_ref):
    f = pl.program_id(2)
    pieces, _, piece = x_ref.shape

    @pl.when(f == 0)
    def _():
        acc_ref[...] = jnp.zeros_like(acc_ref)
        for q in range(pieces):
            xe_ref[:, q * piece:(q + 1) * piece] = x_ref[q].astype(BF16)

    xe = xe_ref[...]
    a = jnp.dot(xe, wg_ref[0].astype(BF16), preferred_element_type=F32)
    b = jnp.dot(xe, wu_ref[0].astype(BF16), preferred_element_type=F32)
    h = (a * jax.nn.sigmoid(a) * b).astype(BF16)
    acc_ref[...] += jnp.dot(h, wd_ref[0].astype(BF16), preferred_element_type=F32)

    @pl.when(f == pl.num_programs(2) - 1)
    def _():
        for c in range(gate_ref.shape[1]):
            rows = slice(c * LANE, (c + 1) * LANE)
            gate = jnp.broadcast_to(gate_ref[0, c:c + 1, :], (LANE, LANE)).T
            ye_ref[0, rows, :] = (acc_ref[rows, :] * jnp.concatenate([gate] * (acc_ref.shape[1] // LANE), axis=1)
                                  ).astype(ye_ref.dtype)


def _expert_ffn(xe, gate, w_gate, w_up, w_down, tm=2048, tf=256):
    e = w_gate.shape[0]
    pieces, rows, piece = xe.shape
    cap, d = rows // e, piece * pieces
    dff = w_gate.shape[-1]
    tiles = cap // tm
    return pl.pallas_call(
        _ffn_kernel,
        grid=(e, tiles, dff // tf),
        in_specs=[pl.BlockSpec((pieces, tm, piece), lambda i, r, f: (0, i * tiles + r, 0)),
                  pl.BlockSpec((1, tm // LANE, LANE), lambda i, r, f: (i, r, 0)),
                  pl.BlockSpec((1, d, tf), lambda i, r, f: (i, 0, f)),
                  pl.BlockSpec((1, d, tf), lambda i, r, f: (i, 0, f)),
                  pl.BlockSpec((1, tf, d), lambda i, r, f: (i, f, 0))],
        out_specs=pl.BlockSpec((1, tm, d), lambda i, r, f: (i, r, 0)),
        out_shape=jax.ShapeDtypeStruct((e, cap, d), BF16),
        scratch_shapes=[pltpu.VMEM((tm, d), BF16), pltpu.VMEM((tm, d), F32)],
        compiler_params=_params("parallel", "parallel", "arbitrary"),
        name="expert_ffn",
    )(xe, gate, w_gate, w_up, w_down)


COMBINE_TAIL = 16


def _combine_kernel(r0_ref, spill_ref, x1_ref, pos_ref, g_ref, *rest):
    ye_refs, y_ref = rest[:N_EXPERTS], rest[N_EXPERTS]
    i = pl.program_id(0)
    nb = pl.num_programs(0)
    t, d = x1_ref.shape
    pos_t = jnp.concatenate([pos_ref[:, 0, :], jnp.zeros((LANE - N_EXPERTS, LANE), F32)], axis=0).T
    col = lax.broadcasted_iota(jnp.int32, (t, LANE), 1)

    def gathered(rows, shift):
        out = jnp.zeros((t, d), F32)
        for e0 in range(0, N_EXPERTS, 2):
            picks = []
            for e in (e0, e0 + 1):
                first = r0_ref[e * nb + i] + shift
                slot = jnp.broadcast_to(pos_t[:, e:e + 1], (t, LANE))
                picks.append(jnp.where(slot == (col + first).astype(F32), 1.0, 0.0).astype(BF16))
            out = out + jnp.dot(jnp.concatenate(picks, axis=1), jnp.concatenate([rows(e0), rows(e0 + 1)], axis=0),
                                preferred_element_type=F32)
        return out

    def finish(x):
        y_ref[...] = x * lax.rsqrt(jnp.mean(x * x, axis=-1, keepdims=True) + EPS) * g_ref[...]

    acc = x1_ref[...] + gathered(lambda e: ye_refs[e][:LANE, :], 0)
    finish(acc)

    @pl.when(spill_ref[i] != 0)
    def _():
        pad = jnp.zeros((LANE - COMBINE_TAIL, d), BF16)
        finish(acc + gathered(lambda e: jnp.concatenate([ye_refs[e][LANE:, :], pad], axis=0), LANE))


def _combine(x1, pos3, off, ye, final_g):
    n, d = x1.shape
    nb = n // LANE
    cap = ye.shape[1]
    first = off[:, 0, :]
    end = jnp.concatenate([first[:, 1:], jnp.full((N_EXPERTS, 1), cap, first.dtype)], axis=1)
    r0 = jnp.minimum(first // 16 * 16, cap - LANE - COMBINE_TAIL)
    spill = jnp.any(end > r0 + LANE, axis=0).astype(jnp.int32)
    pos4 = pos3.reshape(N_EXPERTS, nb, 1, LANE)
    window = lambda e, rows, shift: pl.BlockSpec(
        (None, pl.Element(rows), pl.Element(d)),
        lambda i, r0_ref, spill_ref: (e, pl.multiple_of(r0_ref[e * nb + i] + shift, 16), 0))
    grid_spec = pltpu.PrefetchScalarGridSpec(
        num_scalar_prefetch=2,
        grid=(nb,),
        in_specs=[pl.BlockSpec((LANE, d), lambda i, r, s: (i, 0)),
                  pl.BlockSpec((N_EXPERTS, None, 1, LANE), lambda i, r, s: (0, i, 0, 0)),
                  pl.BlockSpec((1, d), lambda i, r, s: (0, 0))]
                 + [window(e, LANE + COMBINE_TAIL, 0) for e in range(N_EXPERTS)],
        out_specs=pl.BlockSpec((LANE, d), lambda i, r, s: (i, 0)),
    )
    return pl.pallas_call(
        _combine_kernel,
        grid_spec=grid_spec,
        out_shape=jax.ShapeDtypeStruct((n, d), F32),
        compiler_params=_params("parallel"),
        name="moe_combine",
    )(r0.reshape(-1).astype(jnp.int32), spill, x1, pos4, final_g, *([ye] * N_EXPERTS))


def _lane_bcast(a, col, width=LANE):
    return jnp.broadcast_to(a[:, col:col + 1], (a.shape[0], width))


DN_TILE = 512
DN_ROWS = 2
INV_STEPS = 5
assert 2 ** (INV_STEPS + 1) == CHUNK
DN_UNROLL = 64
DN_BUILD_UNROLL = 4


def _dn_scan_kernel(q_ref, k_ref, v_ref, kt_ref, ba_ref, alog_ref, dtb_ref, o_ref,
                    s_ref, xbuf, tbuf, abuf, rbuf, uwbuf, qgbuf, kdbuf, lbuf, nbuf, obuf, ecbuf, *, reverse):
    @pl.when(pl.program_id(1) == 0)
    def _():
        s_ref[...] = jnp.zeros_like(s_ref)

    nrow, tile = q_ref.shape[0], q_ref.shape[1]
    nc = tile // CHUNK
    n_prob = nrow * nc * DN_HEADS
    unroll = min(DN_UNROLL, n_prob)
    beta_col = DN_HEADS if reverse else 0
    a_col = 2 * DN_HEADS + (DN_HEADS if reverse else 0)
    ri = lax.broadcasted_iota(jnp.int32, (CHUNK, CHUNK), 0)
    ci = lax.broadcasted_iota(jnp.int32, (CHUNK, CHUNK), 1)
    tri = (ri <= ci) if reverse else (ri >= ci)
    strict = (ri < ci) if reverse else (ri > ci)
    eye = jnp.where(ri == ci, 1.0, 0.0)
    row = lax.broadcasted_iota(jnp.int32, (CHUNK, LANE), 0)
    a_scale = jnp.exp(alog_ref[...])
    dtb = dtb_ref[...]
    dot = lambda a, b: jnp.dot(a, b, preferred_element_type=F32)

    def build(bc, carry):
        b, c = bc // nc, bc % nc
        rows = pl.ds(pl.multiple_of(c * CHUNK, CHUNK), CHUNK)
        raw = ba_ref[b, rows, :]
        beta_all = jax.nn.sigmoid(raw)
        xs = raw + dtb
        g = -a_scale * (jnp.maximum(xs, 0.0) + jnp.log(1.0 + jnp.exp(-jnp.abs(xs))))
        gs = g
        sh = 1
        while sh < CHUNK:
            if reverse:
                gs = gs + jnp.where(row < CHUNK - sh, pltpu.roll(gs, CHUNK - sh, axis=0), 0.0)
            else:
                gs = gs + jnp.where(row >= sh, pltpu.roll(gs, sh, axis=0), 0.0)
            sh *= 2
        gs_t = gs.T
        g_last = gs[0:1, :] if reverse else gs[CHUNK - 1:CHUNK, :]
        eg_all = jnp.exp(gs)
        ek_t = jnp.exp(g_last.T - gs_t)
        ecbuf[bc] = jnp.broadcast_to(jnp.exp(g_last), (SUBLANE, LANE))
        for h in range(DN_HEADS):
            p = bc * DN_HEADS + h
            hs = slice(h * DN_HEAD_DIM, (h + 1) * DN_HEAD_DIM)
            q = q_ref[b, rows, hs].astype(F32)
            k = k_ref[b, rows, hs]
            v = v_ref[b, rows, hs].astype(F32)
            beta = _lane_bcast(beta_all, beta_col + h)
            eg = _lane_bcast(eg_all, a_col + h)
            diff = _lane_bcast(gs, a_col + h, CHUNK) - gs_t[a_col + h:a_col + h + 1, :]
            decay = jnp.where(tri, jnp.exp(diff), 0.0)
            kb = k.astype(F32) * beta
            kq = lax.dot_general(jnp.concatenate([kb, q], axis=0).astype(BF16), k,
                                 (((1,), (1,)), ((), ())), preferred_element_type=F32)
            lower = jnp.where(strict, kq[:CHUNK] * decay, 0.0)
            xbuf[p] = (-lower).astype(BF16)
            tbuf[p] = eye - lower
            abuf[p] = (kq[CHUNK:] * decay).astype(BF16)
            rbuf[p] = jnp.concatenate([v * beta, kb * eg], axis=1).astype(BF16)
            qgbuf[p] = q * eg
            kdbuf[p] = (kt_ref[b, c, hs, :].astype(F32)
                        * ek_t[a_col + h:a_col + h + 1, :]).astype(BF16)
        return carry

    lax.fori_loop(0, nrow * nc, build, 0, unroll=DN_BUILD_UNROLL)

    for step in range(INV_STEPS + 1):
        def double(p, carry, first=(step == 0), last=(step == INV_STEPS)):
            x = xbuf[p]
            if not first:
                t = tbuf[p]
                tbuf[p] = t + dot(t.astype(BF16), x)
            if not last:
                xbuf[p] = dot(x, x).astype(BF16)
            return carry
        lax.fori_loop(0, n_prob, double, 0, unroll=unroll)

    def solve(p, carry):
        uwbuf[p] = dot(tbuf[p].astype(BF16), rbuf[p]).astype(BF16)
        return carry

    lax.fori_loop(0, n_prob, solve, 0, unroll=unroll)

    def fold(p, carry):
        uw = uwbuf[p]
        a_uw = dot(abuf[p], uw)
        k_uw = dot(kdbuf[p], uw)
        obuf[p] = a_uw[:, :DN_HEAD_DIM]
        nbuf[p] = k_uw[:, :DN_HEAD_DIM]
        lbuf[p, :DN_HEAD_DIM, :] = k_uw[:, DN_HEAD_DIM:].astype(BF16)
        lbuf[p, DN_HEAD_DIM:, :] = (qgbuf[p] - a_uw[:, DN_HEAD_DIM:]).astype(BF16)
        return carry

    lax.fori_loop(0, n_prob, fold, 0, unroll=unroll)

    def scan(jc, carry):
        c = (nc - 1 - jc) if reverse else jc
        rows = pl.ds(pl.multiple_of(c * CHUNK, CHUNK), CHUNK)
        for b in range(nrow):
            ec_all = ecbuf[b * nc + c]
            for h in range(DN_HEADS):
                p = (b * nc + c) * DN_HEADS + h
                state = s_ref[b * DN_HEADS + h]
                r = dot(lbuf[p], state.astype(BF16))
                s_ref[b * DN_HEADS + h] = (state * _lane_bcast(ec_all[0:1], a_col + h)
                                           - r[:DN_HEAD_DIM] + nbuf[p])
                o = r[DN_HEAD_DIM:] + obuf[p]
                o_ref[b, rows, h * DN_HEAD_DIM:(h + 1) * DN_HEAD_DIM] = o.astype(o_ref.dtype)
        return carry

    lax.fori_loop(0, nc, scan, 0)


def _dn_scan(q, k, v, kt, ba, a_log_row, dt_bias_row, reverse, tile=DN_TILE, nrow=DN_ROWS):
    b, s, _ = q.shape
    assert b % nrow == 0
    tile = min(tile, s)
    nt = s // tile
    nc = tile // CHUNK
    n_prob = nrow * nc * DN_HEADS
    tmap = (lambda bi, ti: (bi, nt - 1 - ti, 0)) if reverse else (lambda bi, ti: (bi, ti, 0))
    wide = pl.BlockSpec((nrow, tile, DN_WIDTH), tmap)
    par = pl.BlockSpec((1, LANE), lambda bi, ti: (0, 0))
    return pl.pallas_call(
        functools.partial(_dn_scan_kernel, reverse=reverse),
        grid=(b // nrow, nt),
        in_specs=[wide, wide, wide,
                  pl.BlockSpec((nrow, nc, DN_WIDTH, CHUNK), lambda bi, ti: tmap(bi, ti) + (0,)),
                  pl.BlockSpec((nrow, tile, LANE), tmap), par, par],
        out_specs=wide,
        out_shape=jax.ShapeDtypeStruct((b, s, DN_WIDTH), BF16),
        scratch_shapes=[pltpu.VMEM((nrow * DN_HEADS, DN_HEAD_DIM, DN_HEAD_DIM), F32),
                        pltpu.VMEM((n_prob, CHUNK, CHUNK), BF16),
                        pltpu.VMEM((n_prob, CHUNK, CHUNK), F32),
                        pltpu.VMEM((n_prob, CHUNK, CHUNK), BF16),
                        pltpu.VMEM((n_prob, CHUNK, 2 * DN_HEAD_DIM), BF16),
                        pltpu.VMEM((n_prob, CHUNK, 2 * DN_HEAD_DIM), BF16),
                        pltpu.VMEM((n_prob, CHUNK, DN_HEAD_DIM), F32),
                        pltpu.VMEM((n_prob, DN_HEAD_DIM, CHUNK), BF16),
                        pltpu.VMEM((n_prob, DN_HEAD_DIM + CHUNK, DN_HEAD_DIM), BF16),
                        pltpu.VMEM((n_prob, DN_HEAD_DIM, DN_HEAD_DIM), F32),
                        pltpu.VMEM((n_prob, CHUNK, DN_HEAD_DIM), F32),
                        pltpu.VMEM((nrow * nc, SUBLANE, LANE), F32)],
        compiler_params=_params("parallel", "arbitrary"),
        name="dn_scan_bwd" if reverse else "dn_scan_fwd",
    )(q, k, v, kt, ba, a_log_row, dt_bias_row)


def _gate_rows(a_log_fwd, a_log_bwd, dt_bias_fwd, dt_bias_bwd):
    pad = lambda f, b: jnp.pad(jnp.concatenate([f, b]).astype(F32), (2 * DN_HEADS, LANE - 4 * DN_HEADS))[None]
    return pad(a_log_fwd, a_log_bwd), pad(dt_bias_fwd, dt_bias_bwd)


def _deltanet(q, k, v, kt, ba, a_log_fwd, dt_bias_fwd, a_log_bwd, dt_bias_bwd):
    alog, dtb = _gate_rows(a_log_fwd, a_log_bwd, dt_bias_fwd, dt_bias_bwd)
    o_f = _dn_scan(q, k, v, kt, ba, alog, dtb, reverse=False)
    o_b = _dn_scan(q, k, v, kt, ba, alog, dtb, reverse=True)
    return o_f, o_b


def _t5_bucket(rel):
    nb = NUM_BUCKETS // 2
    ret = jnp.where(rel > 0, nb, 0)
    n = jnp.abs(rel)
    max_exact = nb // 2
    nf = jnp.maximum(n, 1).astype(F32)
    large = max_exact + (jnp.log(nf / max_exact) / math.log(MAX_DISTANCE / max_exact)
                         * (nb - max_exact)).astype(jnp.int32)
    large = jnp.minimum(large, nb - 1)
    return ret + jnp.where(n < max_exact, n, large)


ATT_TILE = 2048
ATT_HALO = BAND * max(d for _, d in DILATED_PATTERNS)
QBLK = 2 * BAND
KBLK = QBLK + 2 * BAND


def _attn_bias_table(rel_bias):
    delta = (jnp.arange(KBLK) - BAND)[None, :] - jnp.arange(QBLK)[:, None]
    tabs = []
    for window, dil in DILATED_PATTERNS:
        half = window // (2 * dil)
        onehot = (_t5_bucket(delta * dil)[..., None] == jnp.arange(NUM_BUCKETS)).astype(F32)
        bias = jnp.einsum('qkb,bh->hqk', onehot, rel_bias.astype(F32),
                          precision=lax.Precision.HIGHEST)
        tabs.append(jnp.where((jnp.abs(delta) <= half)[None], bias, NEG))
    return jnp.stack(tabs)


def _attn_kernel(q_ref, kp_ref, kc_ref, kn_ref, vp_ref, vc_ref, vn_ref, bias_ref, o_ref,
                 qd_a, qd_b, kf, vf, kd, vd, od, md, ld, o_sc, m_sc, l_sc, *, seq_len):
    t = pl.program_id(1)
    lane = lax.broadcasted_iota(jnp.int32, (QBLK, LANE), 1)
    low = lane < ATT_HEAD_DIM
    scale = ATT_HEAD_DIM ** -0.5
    kcol = lax.broadcasted_iota(jnp.int32, (1, KBLK), 1)
    ones = jnp.ones((KBLK, LANE), BF16)

    for p, (window, dil) in enumerate(DILATED_PATTERNS):
        lt = ATT_TILE // dil
        lh = ATT_HALO // dil
        span = lt + 2 * lh
        nblk = lt // QBLK
        n_pos = seq_len // dil
        qd, q_prev = (qd_a, qd_b) if p % 2 == 0 else (qd_b, qd_a)
        prev_dil = DILATED_PATTERNS[p - 1][1] if p > 0 else 1
        nested = p > 0 and prev_dil > 1 and dil % prev_dil == 0
        step = dil // prev_dil
        keep_f32 = p + 1 < len(DILATED_PATTERNS) and dil > 1 and DILATED_PATTERNS[p + 1][1] % dil == 0
        for r in range(dil):
            if nested:
                rp, a = r % prev_dil, r // prev_dil
                lt_p, span_p = ATT_TILE // prev_dil, (ATT_TILE + 2 * ATT_HALO) // prev_dil
                qd[pl.ds(r * lt, lt), :] = q_prev[pl.ds(rp * lt_p + a, lt, stride=step), :]
                src = pl.ds(rp * span_p + a, span, stride=step)
                kd[pl.ds(r * span, span), :] = kf[src, :].astype(BF16)
                vd[pl.ds(r * span, span), :] = vf[src, :].astype(BF16)
                continue
            sl = lambda n: pl.ds(r, n, stride=dil) if dil > 1 else pl.ds(0, n)
            qd[pl.ds(r * lt, lt), :] = q_ref[0, sl(lt), :] * scale
            off = r * span
            for (kr, vr, n) in ((kp_ref, vp_ref, lh), (kc_ref, vc_ref, lt), (kn_ref, vn_ref, lh)):
                kx, vx = kr[0, sl(n), :], vr[0, sl(n), :]
                if keep_f32:
                    kf[pl.ds(off, n), :] = kx
                    vf[pl.ds(off, n), :] = vx
                kd[pl.ds(off, n), :] = kx.astype(BF16)
                vd[pl.ds(off, n), :] = vx.astype(BF16)
                off += n


        for i in range(dil * nblk):
            r, j = divmod(i, nblk)
            q0 = r * lt + j * QBLK
            k0 = r * span + lh + j * QBLK - BAND
            q = qd[pl.ds(q0, QBLK), :]
            k = kd[pl.ds(k0, KBLK), :]
            v1 = jnp.concatenate([vd[pl.ds(k0, KBLK), :], ones], axis=1)
            edge = j == 0 or j == nblk - 1
            if edge:
                pos = t * lt + j * QBLK - BAND + kcol
                valid = (pos >= 0) & (pos < n_pos)
            res = []
            for h in range(2):
                qh = jnp.where(low, q, 0.0) if h == 0 else jnp.where(low, 0.0, q)
                s = lax.dot_general(qh.astype(BF16), k, (((1,), (1,)), ((), ())),
                                    preferred_element_type=F32) + bias_ref[p, h]
                if edge:
                    s = jnp.where(valid, s, NEG)
                m = jnp.max(s, axis=-1, keepdims=True)
                e = jnp.exp(s - m).astype(BF16)
                res.append((jnp.dot(e, v1, preferred_element_type=F32), m))
            (pv0, m0), (pv1, m1) = res
            od[pl.ds(q0, QBLK), :] = jnp.where(low, pv0[:, :LANE], pv1[:, :LANE])
            ld[pl.ds(q0, QBLK), :] = jnp.where(low, pv0[:, LANE:], pv1[:, LANE:])
            md[pl.ds(q0, QBLK), :] = jnp.where(low, m0, m1)

        for r in range(dil):
            sl = pl.ds(r, lt, stride=dil) if dil > 1 else pl.ds(0, lt)
            src = pl.ds(r * lt, lt)
            o_sc[p, sl, :] = od[src, :]
            m_sc[p, sl, :] = md[src, :]
            l_sc[p, sl, :] = ld[src, :]

    n_p = len(DILATED_PATTERNS)
    mx = m_sc[0]
    for p in range(1, n_p):
        mx = jnp.maximum(mx, m_sc[p])
    num = jnp.zeros((ATT_TILE, LANE), F32)
    den = jnp.zeros((ATT_TILE, LANE), F32)
    for p in range(n_p):
        w = jnp.exp(m_sc[p] - mx)
        num += o_sc[p] * w
        den += l_sc[p] * w
    o_ref[0] = (num / den).astype(o_ref.dtype)


def _dilated_attention(att, bias):
    b, s, _ = att.shape
    assert s % ATT_TILE == 0
    nt = s // ATT_TILE
    nh = s // ATT_HALO
    per = ATT_TILE // ATT_HALO
    hp = ATT_WIDTH // LANE
    cur = lambda col0: pl.BlockSpec((1, ATT_TILE, LANE), lambda bi, ti, hi: (bi, ti, col0 + hi))
    prev = lambda col0: pl.BlockSpec(
        (1, ATT_HALO, LANE), lambda bi, ti, hi: (bi, jnp.maximum(ti * per - 1, 0), col0 + hi))
    nxt = lambda col0: pl.BlockSpec(
        (1, ATT_HALO, LANE), lambda bi, ti, hi: (bi, jnp.minimum((ti + 1) * per, nh - 1), col0 + hi))
    n_p = len(DILATED_PATTERNS)
    tile_f32 = pltpu.VMEM((ATT_TILE, LANE), F32)
    span_bf16 = pltpu.VMEM((ATT_TILE + 2 * ATT_HALO, LANE), BF16)
    span_f32 = pltpu.VMEM((ATT_TILE + 2 * ATT_HALO, LANE), F32)
    per_pattern = pltpu.VMEM((n_p, ATT_TILE, LANE), F32)
    return pl.pallas_call(
        functools.partial(_attn_kernel, seq_len=s),
        grid=(b, nt, hp),
        in_specs=[cur(0), prev(hp), cur(hp), nxt(hp), prev(2 * hp), cur(2 * hp), nxt(2 * hp),
                  pl.BlockSpec((n_p, 2, QBLK, KBLK), lambda bi, ti, hi: (0, hi, 0, 0))],
        out_specs=pl.BlockSpec((1, ATT_TILE, LANE), lambda bi, ti, hi: (bi, ti, hi)),
        out_shape=jax.ShapeDtypeStruct((b, s, ATT_WIDTH), BF16),
        scratch_shapes=[tile_f32, tile_f32, span_f32, span_f32, span_bf16, span_bf16, tile_f32, tile_f32, tile_f32,
                        per_pattern, per_pattern, per_pattern],
        compiler_params=_params("parallel", "parallel", "parallel"),
        name="dilated_attention",
    )(att, att, att, att, att, att, att, bias)


def _mix(x, att_bias, norm1_g, w_in, conv_w, a_log_fwd, dt_bias_fwd, a_log_bwd, dt_bias_bwd,
         dn_norm_g, w_out, norm2_g, w_router):
    bsz, s, d = x.shape
    xf = x.reshape(bsz * s, d)
    q, k, v, kt, z, ba, att = _in_proj(xf, norm1_g[None], w_in, conv_w, s)
    shp = lambda a: a.reshape(bsz, s, a.shape[-1])
    o_f, o_b = _deltanet(shp(q), shp(k), shp(v), kt.reshape(bsz, s // CHUNK, DN_WIDTH, CHUNK), shp(ba),
                         a_log_fwd, dt_bias_fwd, a_log_bwd, dt_bias_bwd)
    att_out = _dilated_attention(shp(att), att_bias)
    flat = lambda a: a.reshape(bsz * s, a.shape[-1])
    return _out_proj(flat(o_f), flat(o_b), z, dn_norm_g[None], flat(att_out), xf, w_out, norm2_g[None],
                     w_router)


def kernel(x_prompt, x_sample, rel_bias, norm1_g, w_in, conv_w, a_log_fwd, dt_bias_fwd, a_log_bwd,
           dt_bias_bwd, dn_norm_g, w_out, norm2_g, w_router, w_gate, w_up, w_down, final_norm_g):
    groups = [x_prompt, x_sample]
    att_bias = _attn_bias_table(rel_bias)
    parts, xes = [], []
    for k in range(len(groups)):
        x = groups[k]
        x1, h2, afft = _mix(x, att_bias, norm1_g[0], w_in[0], conv_w[0], a_log_fwd[0], dt_bias_fwd[0],
                            a_log_bwd[0], dt_bias_bwd[0], dn_norm_g[0], w_out[0], norm2_g[0], w_router[0])
        n_tok = x1.shape[0]
        cap = CAPACITY_FACTOR * n_tok // N_EXPERTS
        pos, idx, off, gate = _route(afft, cap)
        rows = _gather_rows(idx.reshape(-1), GATHER_SPLIT, n_tok)
        if k + 1 < len(groups):
            rows, groups[k + 1] = lax.optimization_barrier((rows, groups[k + 1]))
        else:
            rows, xes[0] = lax.optimization_barrier((rows, xes[0]))
        xes.append(_sc_gather(h2, rows))
        parts.append((x, x1, pos, off, gate))
    outs = []
    for k, (x, x1, pos, off, gate) in enumerate(parts):
        ye = _expert_ffn(xes[k], gate, w_gate[0], w_up[0], w_down[0])
        if k + 1 < len(parts):
            xes[k + 1], ye = lax.optimization_barrier((xes[k + 1], ye))
        outs.append(_combine(x1, pos, off, ye, final_norm_g[None]).reshape(groups[k].shape))
    return tuple(outs)
```

```python
import functools
import math

import jax
import jax.numpy as jnp
from jax import lax
from jax.experimental import pallas as pl
from jax.experimental.pallas import tpu as pltpu
from jax.experimental.pallas import tpu_sc as plsc

D_MODEL = 1024
DN_HEADS = 4
DN_HEAD_DIM = 128
DN_WIDTH = DN_HEADS * DN_HEAD_DIM
ATT_HEADS = 8
ATT_HEAD_DIM = 64
ATT_WIDTH = ATT_HEADS * ATT_HEAD_DIM
CONV_K = 3
CHUNK = 64
DILATED_PATTERNS = ((128, 1), (512, 4), (2048, 16))
BAND = 64
NUM_BUCKETS = 32
MAX_DISTANCE = 1024
N_EXPERTS = 16
CAPACITY_FACTOR = 2
EXPERT_D_FF = 2816
EPS = 1e-6
NEG = -1e30
GATE_COLS = 4 * DN_HEADS
LANE = 128
SUBLANE = 8
GATHER_WINDOW = 128
GATHER_SPLIT = 4

F32 = jnp.float32
BF16 = jnp.bfloat16

VMEM_LIMIT = 56 * 1024 * 1024


def _params(*sem):
    return pltpu.CompilerParams(dimension_semantics=sem, vmem_limit_bytes=VMEM_LIMIT)


def _in_proj_kernel(x_ref, xp_ref, xn_ref, g_ref, wqkv_ref, wz_ref, wba_ref, watt_ref, cw_ref,
                    q_ref, k_ref, v_ref, kt_ref, z_ref, ba_ref, att_ref, *, tiles_per_seq):
    t = pl.program_id(0) % tiles_per_seq
    tm = x_ref.shape[0]
    x = jnp.concatenate([xp_ref[...], x_ref[...], xn_ref[...]], axis=0)
    h = x * lax.rsqrt(jnp.mean(x * x, axis=-1, keepdims=True) + EPS) * g_ref[...]
    hb = h[SUBLANE:SUBLANE + tm].astype(BF16)
    z_ref[...] = jnp.dot(hb, wz_ref[...], preferred_element_type=F32)
    ba_ref[...] = jnp.dot(hb, wba_ref[...], preferred_element_type=F32)
    att_ref[...] = jnp.dot(hb, watt_ref[...], preferred_element_type=F32)

    p = jnp.dot(h.astype(BF16), wqkv_ref[...], preferred_element_type=F32)
    rows = p.shape[0]
    row = lax.broadcasted_iota(jnp.int32, (tm, 1), 0)
    mid = lambda a: a[SUBLANE:SUBLANE + tm]
    p_prev = jnp.where((row == 0) & (t == 0), 0.0, mid(pltpu.roll(p, 1, axis=0)))
    p_next = jnp.where((row == tm - 1) & (t == tiles_per_seq - 1), 0.0, mid(pltpu.roll(p, rows - 1, axis=0)))
    y = cw_ref[0:1, :] * p_prev + cw_ref[1:2, :] * mid(p) + cw_ref[2:3, :] * p_next
    y = y * jax.nn.sigmoid(y)
    for hd in range(DN_HEADS):
        for part, ref, scale in ((0, q_ref, DN_HEAD_DIM ** -0.5), (1, k_ref, 1.0)):
            c0 = part * DN_WIDTH + hd * DN_HEAD_DIM
            a = y[:, c0:c0 + DN_HEAD_DIM]
            a = a * (lax.rsqrt(jnp.sum(a * a, axis=-1, keepdims=True) + EPS) * scale)
            ref[:, hd * DN_HEAD_DIM:(hd + 1) * DN_HEAD_DIM] = a.astype(ref.dtype)
            if part == 1:
                for j in range(tm // CHUNK):
                    kt_ref[j, hd * DN_HEAD_DIM:(hd + 1) * DN_HEAD_DIM, :] = (
                        a[j * CHUNK:(j + 1) * CHUNK, :].T.astype(kt_ref.dtype))
    v_ref[...] = y[:, 2 * DN_WIDTH:].astype(v_ref.dtype)


def _in_proj(x, g, w_in, conv_w, seq_len, tm=512):
    n = x.shape[0]
    o_qkv = 3 * DN_WIDTH
    o_z = o_qkv + DN_WIDTH
    o_ba = o_z + GATE_COLS
    wb = w_in.astype(BF16)
    wqkv = wb[:, :o_qkv]
    wz = wb[:, o_qkv:o_z]
    wba = jnp.pad(wb[:, o_z:o_ba], ((0, 0), (0, LANE - GATE_COLS)))
    watt = wb[:, o_ba:]
    per = tm // SUBLANE
    nsub = n // SUBLANE
    full = lambda a: pl.BlockSpec(a.shape, lambda i: (0, 0))
    row = lambda w: pl.BlockSpec((tm, w), lambda i: (i, 0))
    dn = jax.ShapeDtypeStruct((n, DN_WIDTH), BF16)
    return pl.pallas_call(
        functools.partial(_in_proj_kernel, tiles_per_seq=seq_len // tm),
        grid=(n // tm,),
        in_specs=[row(D_MODEL),
                  pl.BlockSpec((SUBLANE, D_MODEL), lambda i: (jnp.maximum(i * per - 1, 0), 0)),
                  pl.BlockSpec((SUBLANE, D_MODEL), lambda i: (jnp.minimum((i + 1) * per, nsub - 1), 0)),
                  full(g), full(wqkv), full(wz), full(wba), full(watt), full(conv_w)],
        out_specs=[row(DN_WIDTH), row(DN_WIDTH), row(DN_WIDTH),
                   pl.BlockSpec((tm // CHUNK, DN_WIDTH, CHUNK), lambda i: (i, 0, 0)),
                   row(DN_WIDTH), row(LANE), row(3 * ATT_WIDTH)],
        out_shape=[dn, dn, dn, jax.ShapeDtypeStruct((n // CHUNK, DN_WIDTH, CHUNK), BF16),
                   jax.ShapeDtypeStruct((n, DN_WIDTH), F32),
                   jax.ShapeDtypeStruct((n, LANE), F32),
                   jax.ShapeDtypeStruct((n, 3 * ATT_WIDTH), F32)],
        compiler_params=_params("parallel"),
        name="in_proj",
    )(x, x, x, g, wqkv, wz, wba, watt, conv_w)


def _split(a):
    hi = a.astype(BF16)
    return hi, (a - hi.astype(F32)).astype(BF16)


def _out_proj_kernel(of_ref, ob_ref, z_ref, gdn_ref, att_ref, x_ref, wdn_ref, watt_ref, g_ref, wr_ref,
                     x1_ref, h2_ref, afft_ref):
    dot = lambda a, b: jnp.dot(a, b, preferred_element_type=F32)
    o = of_ref[...].astype(F32) + ob_ref[...].astype(F32)
    z = z_ref[...]
    gated = []
    for h in range(DN_HEADS):
        hs = slice(h * DN_HEAD_DIM, (h + 1) * DN_HEAD_DIM)
        oh = o[:, hs]
        oh = oh * lax.rsqrt(jnp.mean(oh * oh, axis=-1, keepdims=True) + EPS) * gdn_ref[...]
        zh = z[:, hs]
        gated.append((oh * (zh * jax.nn.sigmoid(zh))).astype(BF16))
    dn = jnp.concatenate(gated, axis=1)
    x1 = x_ref[...] + dot(dn, wdn_ref[...]) + dot(att_ref[...], watt_ref[...])
    x1_ref[...] = x1
    h2 = x1 * lax.rsqrt(jnp.mean(x1 * x1, axis=-1, keepdims=True) + EPS) * g_ref[...]
    rows = h2.shape[0]
    piece = D_MODEL // GATHER_SPLIT
    for q in range(GATHER_SPLIT):
        h2_ref[q] = h2[:, q * piece:(q + 1) * piece]
    hh, hl = _split(h2)
    wh, wl = _split(wr_ref[...])
    both = dot(hh, jnp.concatenate([wh, wl], axis=1))
    logits = both[:, :LANE] + (both[:, LANE:] + dot(hl, wh))
    lane = lax.broadcasted_iota(jnp.int32, logits.shape, 1)
    logits = jnp.where(lane < N_EXPERTS, logits, NEG)
    p = jnp.exp(logits - jnp.max(logits, axis=-1, keepdims=True))
    aff = p / jnp.sum(p, axis=-1, keepdims=True)
    for j in range(rows // LANE):
        afft_ref[:, j, :] = aff[j * LANE:(j + 1) * LANE, :].T[:N_EXPERTS, :]


def _out_proj(o_f, o_b, z, dn_norm_g, att, x, w_out, g, w_router, tm=1024):
    n = x.shape[0]
    wb = w_out.astype(BF16)
    wdn, watt = wb[:DN_WIDTH], wb[DN_WIDTH:]
    wr = jnp.pad(w_router.astype(F32), ((0, 0), (0, LANE - N_EXPERTS)))
    full = lambda a: pl.BlockSpec(a.shape, lambda i: (0, 0))
    row = lambda w: pl.BlockSpec((tm, w), lambda i: (i, 0))
    return pl.pallas_call(
        _out_proj_kernel,
        grid=(n // tm,),
        in_specs=[row(DN_WIDTH), row(DN_WIDTH), row(DN_WIDTH), full(dn_norm_g), row(ATT_WIDTH),
                  row(D_MODEL), full(wdn), full(watt), full(g), full(wr)],
        out_specs=[row(D_MODEL),
                   pl.BlockSpec((GATHER_SPLIT, tm, D_MODEL // GATHER_SPLIT), lambda i: (0, i, 0)),
                   pl.BlockSpec((N_EXPERTS, tm // LANE, LANE), lambda i: (0, i, 0))],
        out_shape=[jax.ShapeDtypeStruct((n, D_MODEL), F32),
                   jax.ShapeDtypeStruct((GATHER_SPLIT, n, D_MODEL // GATHER_SPLIT), F32),
                   jax.ShapeDtypeStruct((N_EXPERTS, n // LANE, LANE), F32)],
        compiler_params=_params("parallel"),
        name="out_proj",
    )(o_f, o_b, z, dn_norm_g, att, x, wdn, watt, g, wr)


def _route_kernel(aff_ref, pos_ref, idx_ref, off_ref, gate_ref, *, cap):
    x = aff_ref[0]
    nb = x.shape[0]
    bits = pltpu.bitcast(x, jnp.int32)
    total = lambda m: jnp.sum(jnp.sum(m, axis=0, keepdims=True), axis=1, keepdims=True)
    dot = lambda a, b: jnp.dot(a, b, preferred_element_type=F32)
    one_hot = lambda m: jnp.where(m, 1.0, 0.0)

    def bit_step(i, prefix):
        cand = prefix | lax.shift_left(jnp.int32(1), 30 - i)
        cnt = total(one_hot(bits >= cand))
        return jnp.where(cnt >= cap, cand, prefix)

    thr = lax.fori_loop(0, 31, bit_step, jnp.zeros((1, 1), jnp.int32))
    gt = bits > thr
    eq = bits == thr
    need = cap - total(one_hot(gt))

    li = lax.broadcasted_iota(jnp.int32, (LANE, LANE), 0)
    lj = lax.broadcasted_iota(jnp.int32, (LANE, LANE), 1)
    lane_incl = one_hot(li <= lj).astype(BF16)
    ones = jnp.ones((LANE, LANE), BF16)
    bi = lax.broadcasted_iota(jnp.int32, (nb, nb), 0)
    bj = lax.broadcasted_iota(jnp.int32, (nb, nb), 1)
    blk_before = one_hot(bj < bi).astype(BF16)

    def ranks(mask):
        m = one_hot(mask).astype(BF16)
        within = dot(m, lane_incl)
        tot = dot(m, ones)
        before = dot(blk_before, tot.astype(BF16))
        return within, tot, before

    w_eq, _, b_eq = ranks(eq)
    sel = gt | (eq & (w_eq + b_eq <= need))
    within, tot, before = ranks(sel)
    pos_ref[0] = jnp.where(sel, within + before - 1.0, -1.0)
    off_ref[0] = before.T[0:SUBLANE, :].astype(jnp.int32)

    within_t = within.T.astype(BF16)
    sel_t = one_hot(sel).T.astype(BF16)
    x_t = x.T
    x_parts = []
    for _ in range(3):
        part = x_t.astype(BF16)
        x_parts.append(part)
        x_t = x_t - part.astype(F32)
    after = before + tot
    blk_id = lax.broadcasted_iota(jnp.int32, (nb, LANE), 0).astype(F32)
    lane_id = lax.broadcasted_iota(jnp.int32, (LANE, LANE), 0).astype(F32)
    for c in range(cap // LANE):
        j = (lax.broadcasted_iota(jnp.int32, (1, LANE), 1) + c * LANE).astype(F32)
        holds = one_hot((before <= j) & (j < after))
        local = j - jnp.sum(holds * before, axis=0, keepdims=True) + 1.0
        blk = jnp.sum(holds * blk_id, axis=0, keepdims=True)
        hb = holds.astype(BF16)
        cnt_in_blk = dot(within_t, hb)
        sel_in_blk = dot(sel_t, hb)
        hit = one_hot((sel_in_blk > 0.5) & (cnt_in_blk == local))
        lane_of = jnp.sum(hit * lane_id, axis=0, keepdims=True)
        idx_ref[0, c:c + 1, :] = (blk * LANE + lane_of).astype(jnp.int32)
        aff_in_blk = dot(x_parts[0], hb) + dot(x_parts[1], hb) + dot(x_parts[2], hb)
        gate_ref[0, c:c + 1, :] = jnp.sum(hit * aff_in_blk, axis=0, keepdims=True)


def _route(aff3, cap):
    e, nb, _ = aff3.shape
    blk = lambda rows: pl.BlockSpec((1, rows, LANE), lambda i: (i, 0, 0))
    return pl.pallas_call(
        functools.partial(_route_kernel, cap=cap),
        grid=(e,),
        in_specs=[blk(nb)],
        out_specs=[blk(nb), blk(cap // LANE), pl.BlockSpec((1, SUBLANE, nb), lambda i: (i, 0, 0)),
                   blk(cap // LANE)],
        out_shape=[jax.ShapeDtypeStruct((e, nb, LANE), F32),
                   jax.ShapeDtypeStruct((e, cap // LANE, LANE), jnp.int32),
                   jax.ShapeDtypeStruct((e, SUBLANE, nb), jnp.int32),
                   jax.ShapeDtypeStruct((e, cap // LANE, LANE), F32)],
        compiler_params=_params("parallel"),
        name="route",
    )(aff3)


def _gather_rows(idx, pieces, n):
    return (jnp.arange(pieces, dtype=idx.dtype)[:, None] * n + idx[None, :]).reshape(1, -1)


def _sc_gather(x, rows):
    pieces, n, d = x.shape
    x = x.reshape(pieces * n, d)
    m = rows.shape[1]
    mesh = plsc.VectorSubcoreMesh(core_axis_name="c", subcore_axis_name="s")

    @pl.kernel(out_type=jax.ShapeDtypeStruct((m, d), x.dtype), mesh=mesh, scratch_types=[])
    def gather_kernel(x_hbm, i_hbm, o_hbm):
        def body(i_vmem, o_vmem):
            pltpu.sync_copy(x_hbm.at[i_vmem.at[0]], o_vmem)

        pltpu.emit_pipeline(
            body,
            grid=(m // GATHER_WINDOW,),
            in_specs=[pl.BlockSpec((1, GATHER_WINDOW), lambda i: (0, i))],
            out_specs=[pl.BlockSpec((GATHER_WINDOW, d), lambda i: (i, 0))],
            core_axis_name=("c", "s"),
            dimension_semantics=(pltpu.PARALLEL,),
        )(i_hbm, o_hbm)

    return gather_kernel(x, rows).reshape(pieces, m // pieces, d)


def _ffn_kernel(x_ref, gate_ref, wg_ref, wu_ref, wd_ref, ye_ref, xe_ref, acc_ref):
    f = pl.program_id(2)
    pieces, _, piece = x_ref.shape

    @pl.when(f == 0)
    def _():
        acc_ref[...] = jnp.zeros_like(acc_ref)
        for q in range(pieces):
            xe_ref[:, q * piece:(q + 1) * piece] = x_ref[q].astype(BF16)

    xe = xe_ref[...]
    a = jnp.dot(xe, wg_ref[0].astype(BF16), preferred_element_type=F32)
    b = jnp.dot(xe, wu_ref[0].astype(BF16), preferred_element_type=F32)
    h = (a * jax.nn.sigmoid(a) * b).astype(BF16)
    acc_ref[...] += jnp.dot(h, wd_ref[0].astype(BF16), preferred_element_type=F32)

    @pl.when(f == pl.num_programs(2) - 1)
    def _():
        for c in range(gate_ref.shape[1]):
            rows = slice(c * LANE, (c + 1) * LANE)
            gate = jnp.broadcast_to(gate_ref[0, c:c + 1, :], (LANE, LANE)).T
            ye_ref[0, rows, :] = (acc_ref[rows, :] * jnp.concatenate([gate] * (acc_ref.shape[1] // LANE), axis=1)
                                  ).astype(ye_ref.dtype)


def _expert_ffn(xe, gate, w_gate, w_up, w_down, tm=2048, tf=256):
    e = w_gate.shape[0]
    pieces, rows, piece = xe.shape
    cap, d = rows // e, piece * pieces
    dff = w_gate.shape[-1]
    tiles = cap // tm
    return pl.pallas_call(
        _ffn_kernel,
        grid=(e, tiles, dff // tf),
        in_specs=[pl.BlockSpec((pieces, tm, piece), lambda i, r, f: (0, i * tiles + r, 0)),
                  pl.BlockSpec((1, tm // LANE, LANE), lambda i, r, f: (i, r, 0)),
                  pl.BlockSpec((1, d, tf), lambda i, r, f: (i, 0, f)),
                  pl.BlockSpec((1, d, tf), lambda i, r, f: (i, 0, f)),
                  pl.BlockSpec((1, tf, d), lambda i, r, f: (i, f, 0))],
        out_specs=pl.BlockSpec((1, tm, d), lambda i, r, f: (i, r, 0)),
        out_shape=jax.ShapeDtypeStruct((e, cap, d), BF16),
        scratch_shapes=[pltpu.VMEM((tm, d), BF16), pltpu.VMEM((tm, d), F32)],
        compiler_params=_params("parallel", "parallel", "arbitrary"),
        name="expert_ffn",
    )(xe, gate, w_gate, w_up, w_down)


COMBINE_TAIL = 16


def _combine_kernel(r0_ref, spill_ref, x1_ref, pos_ref, g_ref, *rest):
    ye_refs, y_ref = rest[:N_EXPERTS], rest[N_EXPERTS]
    i = pl.program_id(0)
    nb = pl.num_programs(0)
    t, d = x1_ref.shape
    pos_t = jnp.concatenate([pos_ref[:, 0, :], jnp.zeros((LANE - N_EXPERTS, LANE), F32)], axis=0).T
    col = lax.broadcasted_iota(jnp.int32, (t, LANE), 1)

    def gathered(rows, shift):
        out = jnp.zeros((t, d), F32)
        for e0 in range(0, N_EXPERTS, 2):
            picks = []
            for e in (e0, e0 + 1):
                first = r0_ref[e * nb + i] + shift
                slot = jnp.broadcast_to(pos_t[:, e:e + 1], (t, LANE))
                picks.append(jnp.where(slot == (col + first).astype(F32), 1.0, 0.0).astype(BF16))
            out = out + jnp.dot(jnp.concatenate(picks, axis=1), jnp.concatenate([rows(e0), rows(e0 + 1)], axis=0),
                                preferred_element_type=F32)
        return out

    def finish(x):
        y_ref[...] = x * lax.rsqrt(jnp.mean(x * x, axis=-1, keepdims=True) + EPS) * g_ref[...]

    acc = x1_ref[...] + gathered(lambda e: ye_refs[e][:LANE, :], 0)
    finish(acc)

    @pl.when(spill_ref[i] != 0)
    def _():
        pad = jnp.zeros((LANE - COMBINE_TAIL, d), BF16)
        finish(acc + gathered(lambda e: jnp.concatenate([ye_refs[e][LANE:, :], pad], axis=0), LANE))


def _combine(x1, pos3, off, ye, final_g):
    n, d = x1.shape
    nb = n // LANE
    cap = ye.shape[1]
    first = off[:, 0, :]
    end = jnp.concatenate([first[:, 1:], jnp.full((N_EXPERTS, 1), cap, first.dtype)], axis=1)
    r0 = jnp.minimum(first // 16 * 16, cap - LANE - COMBINE_TAIL)
    spill = jnp.any(end > r0 + LANE, axis=0).astype(jnp.int32)
    pos4 = pos3.reshape(N_EXPERTS, nb, 1, LANE)
    window = lambda e, rows, shift: pl.BlockSpec(
        (None, pl.Element(rows), pl.Element(d)),
        lambda i, r0_ref, spill_ref: (e, pl.multiple_of(r0_ref[e * nb + i] + shift, 16), 0))
    grid_spec = pltpu.PrefetchScalarGridSpec(
        num_scalar_prefetch=2,
        grid=(nb,),
        in_specs=[pl.BlockSpec((LANE, d), lambda i, r, s: (i, 0)),
                  pl.BlockSpec((N_EXPERTS, None, 1, LANE), lambda i, r, s: (0, i, 0, 0)),
                  pl.BlockSpec((1, d), lambda i, r, s: (0, 0))]
                 + [window(e, LANE + COMBINE_TAIL, 0) for e in range(N_EXPERTS)],
        out_specs=pl.BlockSpec((LANE, d), lambda i, r, s: (i, 0)),
    )
    return pl.pallas_call(
        _combine_kernel,
        grid_spec=grid_spec,
        out_shape=jax.ShapeDtypeStruct((n, d), F32),
        compiler_params=_params("parallel"),
        name="moe_combine",
    )(r0.reshape(-1).astype(jnp.int32), spill, x1, pos4, final_g, *([ye] * N_EXPERTS))


def _lane_bcast(a, col, width=LANE):
    return jnp.broadcast_to(a[:, col:col + 1], (a.shape[0], width))


DN_TILE = 512
DN_ROWS = 2
INV_STEPS = 5
assert 2 ** (INV_STEPS + 1) == CHUNK
DN_UNROLL = 64
DN_BUILD_UNROLL = 4


def _dn_scan_kernel(q_ref, k_ref, v_ref, kt_ref, ba_ref, alog_ref, dtb_ref, o_ref,
                    s_ref, xbuf, tbuf, abuf, rbuf, uwbuf, qgbuf, kdbuf, lbuf, nbuf, obuf, ecbuf, *, reverse):
    @pl.when(pl.program_id(1) == 0)
    def _():
        s_ref[...] = jnp.zeros_like(s_ref)

    nrow, tile = q_ref.shape[0], q_ref.shape[1]
    nc = tile // CHUNK
    n_prob = nrow * nc * DN_HEADS
    unroll = min(DN_UNROLL, n_prob)
    beta_col = DN_HEADS if reverse else 0
    a_col = 2 * DN_HEADS + (DN_HEADS if reverse else 0)
    ri = lax.broadcasted_iota(jnp.int32, (CHUNK, CHUNK), 0)
    ci = lax.broadcasted_iota(jnp.int32, (CHUNK, CHUNK), 1)
    tri = (ri <= ci) if reverse else (ri >= ci)
    strict = (ri < ci) if reverse else (ri > ci)
    eye = jnp.where(ri == ci, 1.0, 0.0)
    row = lax.broadcasted_iota(jnp.int32, (CHUNK, LANE), 0)
    a_scale = jnp.exp(alog_ref[...])
    dtb = dtb_ref[...]
    dot = lambda a, b: jnp.dot(a, b, preferred_element_type=F32)

    def build(bc, carry):
        b, c = bc // nc, bc % nc
        rows = pl.ds(pl.multiple_of(c * CHUNK, CHUNK), CHUNK)
        raw = ba_ref[b, rows, :]
        beta_all = jax.nn.sigmoid(raw)
        xs = raw + dtb
        g = -a_scale * (jnp.maximum(xs, 0.0) + jnp.log(1.0 + jnp.exp(-jnp.abs(xs))))
        gs = g
        sh = 1
        while sh < CHUNK:
            if reverse:
                gs = gs + jnp.where(row < CHUNK - sh, pltpu.roll(gs, CHUNK - sh, axis=0), 0.0)
            else:
                gs = gs + jnp.where(row >= sh, pltpu.roll(gs, sh, axis=0), 0.0)
            sh *= 2
        gs_t = gs.T
        g_last = gs[0:1, :] if reverse else gs[CHUNK - 1:CHUNK, :]
        eg_all = jnp.exp(gs)
        ek_t = jnp.exp(g_last.T - gs_t)
        ecbuf[bc] = jnp.broadcast_to(jnp.exp(g_last), (SUBLANE, LANE))
        for h in range(DN_HEADS):
            p = bc * DN_HEADS + h
            hs = slice(h * DN_HEAD_DIM, (h + 1) * DN_HEAD_DIM)
            q = q_ref[b, rows, hs].astype(F32)
            k = k_ref[b, rows, hs]
            v = v_ref[b, rows, hs].astype(F32)
            beta = _lane_bcast(beta_all, beta_col + h)
            eg = _lane_bcast(eg_all, a_col + h)
            diff = _lane_bcast(gs, a_col + h, CHUNK) - gs_t[a_col + h:a_col + h + 1, :]
            decay = jnp.where(tri, jnp.exp(diff), 0.0)
            kb = k.astype(F32) * beta
            kq = lax.dot_general(jnp.concatenate([kb, q], axis=0).astype(BF16), k,
                                 (((1,), (1,)), ((), ())), preferred_element_type=F32)
            lower = jnp.where(strict, kq[:CHUNK] * decay, 0.0)
            xbuf[p] = (-lower).astype(BF16)
            tbuf[p] = eye - lower
            abuf[p] = (kq[CHUNK:] * decay).astype(BF16)
            rbuf[p] = jnp.concatenate([v * beta, kb * eg], axis=1).astype(BF16)
            qgbuf[p] = q * eg
            kdbuf[p] = (kt_ref[b, c, hs, :].astype(F32)
                        * ek_t[a_col + h:a_col + h + 1, :]).astype(BF16)
        return carry

    lax.fori_loop(0, nrow * nc, build, 0, unroll=DN_BUILD_UNROLL)

    for step in range(INV_STEPS + 1):
        def double(p, carry, first=(step == 0), last=(step == INV_STEPS)):
            x = xbuf[p]
            if not first:
                t = tbuf[p]
                tbuf[p] = t + dot(t.astype(BF16), x)
            if not last:
                xbuf[p] = dot(x, x).astype(BF16)
            return carry
        lax.fori_loop(0, n_prob, double, 0, unroll=unroll)

    def solve(p, carry):
        uwbuf[p] = dot(tbuf[p].astype(BF16), rbuf[p]).astype(BF16)
        return carry

    lax.fori_loop(0, n_prob, solve, 0, unroll=unroll)

    def fold(p, carry):
        uw = uwbuf[p]
        a_uw = dot(abuf[p], uw)
        k_uw = dot(kdbuf[p], uw)
        obuf[p] = a_uw[:, :DN_HEAD_DIM]
        nbuf[p] = k_uw[:, :DN_HEAD_DIM]
        lbuf[p, :DN_HEAD_DIM, :] = k_uw[:, DN_HEAD_DIM:].astype(BF16)
        lbuf[p, DN_HEAD_DIM:, :] = (qgbuf[p] - a_uw[:, DN_HEAD_DIM:]).astype(BF16)
        return carry

    lax.fori_loop(0, n_prob, fold, 0, unroll=unroll)

    def scan(jc, carry):
        c = (nc - 1 - jc) if reverse else jc
        rows = pl.ds(pl.multiple_of(c * CHUNK, CHUNK), CHUNK)
        for b in range(nrow):
            ec_all = ecbuf[b * nc + c]
            for h in range(DN_HEADS):
                p = (b * nc + c) * DN_HEADS + h
                state = s_ref[b * DN_HEADS + h]
                r = dot(lbuf[p], state.astype(BF16))
                s_ref[b * DN_HEADS + h] = (state * _lane_bcast(ec_all[0:1], a_col + h)
                                           - r[:DN_HEAD_DIM] + nbuf[p])
                o = r[DN_HEAD_DIM:] + obuf[p]
                o_ref[b, rows, h * DN_HEAD_DIM:(h + 1) * DN_HEAD_DIM] = o.astype(o_ref.dtype)
        return carry

    lax.fori_loop(0, nc, scan, 0)


def _dn_scan(q, k, v, kt, ba, a_log_row, dt_bias_row, reverse, tile=DN_TILE, nrow=DN_ROWS):
    b, s, _ = q.shape
    assert b % nrow == 0
    tile = min(tile, s)
    nt = s // tile
    nc = tile // CHUNK
    n_prob = nrow * nc * DN_HEADS
    tmap = (lambda bi, ti: (bi, nt - 1 - ti, 0)) if reverse else (lambda bi, ti: (bi, ti, 0))
    wide = pl.BlockSpec((nrow, tile, DN_WIDTH), tmap)
    par = pl.BlockSpec((1, LANE), lambda bi, ti: (0, 0))
    return pl.pallas_call(
        functools.partial(_dn_scan_kernel, reverse=reverse),
        grid=(b // nrow, nt),
        in_specs=[wide, wide, wide,
                  pl.BlockSpec((nrow, nc, DN_WIDTH, CHUNK), lambda bi, ti: tmap(bi, ti) + (0,)),
                  pl.BlockSpec((nrow, tile, LANE), tmap), par, par],
        out_specs=wide,
        out_shape=jax.ShapeDtypeStruct((b, s, DN_WIDTH), BF16),
        scratch_shapes=[pltpu.VMEM((nrow * DN_HEADS, DN_HEAD_DIM, DN_HEAD_DIM), F32),
                        pltpu.VMEM((n_prob, CHUNK, CHUNK), BF16),
                        pltpu.VMEM((n_prob, CHUNK, CHUNK), F32),
                        pltpu.VMEM((n_prob, CHUNK, CHUNK), BF16),
                        pltpu.VMEM((n_prob, CHUNK, 2 * DN_HEAD_DIM), BF16),
                        pltpu.VMEM((n_prob, CHUNK, 2 * DN_HEAD_DIM), BF16),
                        pltpu.VMEM((n_prob, CHUNK, DN_HEAD_DIM), F32),
                        pltpu.VMEM((n_prob, DN_HEAD_DIM, CHUNK), BF16),
                        pltpu.VMEM((n_prob, DN_HEAD_DIM + CHUNK, DN_HEAD_DIM), BF16),
                        pltpu.VMEM((n_prob, DN_HEAD_DIM, DN_HEAD_DIM), F32),
                        pltpu.VMEM((n_prob, CHUNK, DN_HEAD_DIM), F32),
                        pltpu.VMEM((nrow * nc, SUBLANE, LANE), F32)],
        compiler_params=_params("parallel", "arbitrary"),
        name="dn_scan_bwd" if reverse else "dn_scan_fwd",
    )(q, k, v, kt, ba, a_log_row, dt_bias_row)


def _gate_rows(a_log_fwd, a_log_bwd, dt_bias_fwd, dt_bias_bwd):
    pad = lambda f, b: jnp.pad(jnp.concatenate([f, b]).astype(F32), (2 * DN_HEADS, LANE - 4 * DN_HEADS))[None]
    return pad(a_log_fwd, a_log_bwd), pad(dt_bias_fwd, dt_bias_bwd)


def _deltanet(q, k, v, kt, ba, a_log_fwd, dt_bias_fwd, a_log_bwd, dt_bias_bwd):
    alog, dtb = _gate_rows(a_log_fwd, a_log_bwd, dt_bias_fwd, dt_bias_bwd)
    o_f = _dn_scan(q, k, v, kt, ba, alog, dtb, reverse=False)
    o_b = _dn_scan(q, k, v, kt, ba, alog, dtb, reverse=True)
    return o_f, o_b


def _t5_bucket(rel):
    nb = NUM_BUCKETS // 2
    ret = jnp.where(rel > 0, nb, 0)
    n = jnp.abs(rel)
    max_exact = nb // 2
    nf = jnp.maximum(n, 1).astype(F32)
    large = max_exact + (jnp.log(nf / max_exact) / math.log(MAX_DISTANCE / max_exact)
                         * (nb - max_exact)).astype(jnp.int32)
    large = jnp.minimum(large, nb - 1)
    return ret + jnp.where(n < max_exact, n, large)


ATT_TILE = 2048
ATT_HALO = BAND * max(d for _, d in DILATED_PATTERNS)
QBLK = 2 * BAND
KBLK = QBLK + 2 * BAND


def _attn_bias_table(rel_bias):
    delta = (jnp.arange(KBLK) - BAND)[None, :] - jnp.arange(QBLK)[:, None]
    tabs = []
    for window, dil in DILATED_PATTERNS:
        half = window // (2 * dil)
        onehot = (_t5_bucket(delta * dil)[..., None] == jnp.arange(NUM_BUCKETS)).astype(F32)
        bias = jnp.einsum('qkb,bh->hqk', onehot, rel_bias.astype(F32),
                          precision=lax.Precision.HIGHEST)
        tabs.append(jnp.where((jnp.abs(delta) <= half)[None], bias, NEG))
    return jnp.stack(tabs)


def _attn_kernel(q_ref, kp_ref, kc_ref, kn_ref, vp_ref, vc_ref, vn_ref, bias_ref, o_ref,
                 qd_a, qd_b, kf, vf, kd, vd, od, md, ld, o_sc, m_sc, l_sc, *, seq_len):
    t = pl.program_id(1)
    lane = lax.broadcasted_iota(jnp.int32, (QBLK, LANE), 1)
    low = lane < ATT_HEAD_DIM
    scale = ATT_HEAD_DIM ** -0.5
    kcol = lax.broadcasted_iota(jnp.int32, (1, KBLK), 1)
    ones = jnp.ones((KBLK, LANE), BF16)

    for p, (window, dil) in enumerate(DILATED_PATTERNS):
        lt = ATT_TILE // dil
        lh = ATT_HALO // dil
        span = lt + 2 * lh
        nblk = lt // QBLK
        n_pos = seq_len // dil
        qd, q_prev = (qd_a, qd_b) if p % 2 == 0 else (qd_b, qd_a)
        prev_dil = DILATED_PATTERNS[p - 1][1] if p > 0 else 1
        nested = p > 0 and prev_dil > 1 and dil % prev_dil == 0
        step = dil // prev_dil
        keep_f32 = p + 1 < len(DILATED_PATTERNS) and dil > 1 and DILATED_PATTERNS[p + 1][1] % dil == 0
        for r in range(dil):
            if nested:
                rp, a = r % prev_dil, r // prev_dil
                lt_p, span_p = ATT_TILE // prev_dil, (ATT_TILE + 2 * ATT_HALO) // prev_dil
                qd[pl.ds(r * lt, lt), :] = q_prev[pl.ds(rp * lt_p + a, lt, stride=step), :]
                src = pl.ds(rp * span_p + a, span, stride=step)
                kd[pl.ds(r * span, span), :] = kf[src, :].astype(BF16)
                vd[pl.ds(r * span, span), :] = vf[src, :].astype(BF16)
                continue
            sl = lambda n: pl.ds(r, n, stride=dil) if dil > 1 else pl.ds(0, n)
            qd[pl.ds(r * lt, lt), :] = q_ref[0, sl(lt), :] * scale
            off = r * span
            for (kr, vr, n) in ((kp_ref, vp_ref, lh), (kc_ref, vc_ref, lt), (kn_ref, vn_ref, lh)):
                kx, vx = kr[0, sl(n), :], vr[0, sl(n), :]
                if keep_f32:
                    kf[pl.ds(off, n), :] = kx
                    vf[pl.ds(off, n), :] = vx
                kd[pl.ds(off, n), :] = kx.astype(BF16)
                vd[pl.ds(off, n), :] = vx.astype(BF16)
                off += n


        for i in range(dil * nblk):
            r, j = divmod(i, nblk)
            q0 = r * lt + j * QBLK
            k0 = r * span + lh + j * QBLK - BAND
            q = qd[pl.ds(q0, QBLK), :]
            k = kd[pl.ds(k0, KBLK), :]
            v1 = jnp.concatenate([vd[pl.ds(k0, KBLK), :], ones], axis=1)
            edge = j == 0 or j == nblk - 1
            if edge:
                pos = t * lt + j * QBLK - BAND + kcol
                valid = (pos >= 0) & (pos < n_pos)
            res = []
            for h in range(2):
                qh = jnp.where(low, q, 0.0) if h == 0 else jnp.where(low, 0.0, q)
                s = lax.dot_general(qh.astype(BF16), k, (((1,), (1,)), ((), ())),
                                    preferred_element_type=F32) + bias_ref[p, h]
                if edge:
                    s = jnp.where(valid, s, NEG)
                m = jnp.max(s, axis=-1, keepdims=True)
                e = jnp.exp(s - m).astype(BF16)
                res.append((jnp.dot(e, v1, preferred_element_type=F32), m))
            (pv0, m0), (pv1, m1) = res
            o_blk = jnp.where(low, pv0[:, :LANE], pv1[:, :LANE])
            l_blk = jnp.where(low, pv0[:, LANE:], pv1[:, LANE:])
            m_blk = jnp.where(low, m0, m1)
            if dil == 1:
                o_sc[p, pl.ds(q0, QBLK), :] = o_blk
                l_sc[p, pl.ds(q0, QBLK), :] = l_blk
                m_sc[p, pl.ds(q0, QBLK), :] = m_blk
            else:
                od[pl.ds(q0, QBLK), :] = o_blk
                ld[pl.ds(q0, QBLK), :] = l_blk
                md[pl.ds(q0, QBLK), :] = m_blk

        for r in range(dil if dil > 1 else 0):
            sl = pl.ds(r, lt, stride=dil)
            src = pl.ds(r * lt, lt)
            o_sc[p, sl, :] = od[src, :]
            m_sc[p, sl, :] = md[src, :]
            l_sc[p, sl, :] = ld[src, :]

    n_p = len(DILATED_PATTERNS)
    mx = m_sc[0]
    for p in range(1, n_p):
        mx = jnp.maximum(mx, m_sc[p])
    num = jnp.zeros((ATT_TILE, LANE), F32)
    den = jnp.zeros((ATT_TILE, LANE), F32)
    for p in range(n_p):
        w = jnp.exp(m_sc[p] - mx)
        num += o_sc[p] * w
        den += l_sc[p] * w
    o_ref[0] = (num / den).astype(o_ref.dtype)


def _dilated_attention(att, bias):
    b, s, _ = att.shape
    assert s % ATT_TILE == 0
    nt = s // ATT_TILE
    nh = s // ATT_HALO
    per = ATT_TILE // ATT_HALO
    hp = ATT_WIDTH // LANE
    cur = lambda col0: pl.BlockSpec((1, ATT_TILE, LANE), lambda bi, ti, hi: (bi, ti, col0 + hi))
    prev = lambda col0: pl.BlockSpec(
        (1, ATT_HALO, LANE), lambda bi, ti, hi: (bi, jnp.maximum(ti * per - 1, 0), col0 + hi))
    nxt = lambda col0: pl.BlockSpec(
        (1, ATT_HALO, LANE), lambda bi, ti, hi: (bi, jnp.minimum((ti + 1) * per, nh - 1), col0 + hi))
    n_p = len(DILATED_PATTERNS)
    tile_f32 = pltpu.VMEM((ATT_TILE, LANE), F32)
    span_bf16 = pltpu.VMEM((ATT_TILE + 2 * ATT_HALO, LANE), BF16)
    span_f32 = pltpu.VMEM((ATT_TILE + 2 * ATT_HALO, LANE), F32)
    per_pattern = pltpu.VMEM((n_p, ATT_TILE, LANE), F32)
    return pl.pallas_call(
        functools.partial(_attn_kernel, seq_len=s),
        grid=(b, nt, hp),
        in_specs=[cur(0), prev(hp), cur(hp), nxt(hp), prev(2 * hp), cur(2 * hp), nxt(2 * hp),
                  pl.BlockSpec((n_p, 2, QBLK, KBLK), lambda bi, ti, hi: (0, hi, 0, 0))],
        out_specs=pl.BlockSpec((1, ATT_TILE, LANE), lambda bi, ti, hi: (bi, ti, hi)),
        out_shape=jax.ShapeDtypeStruct((b, s, ATT_WIDTH), BF16),
        scratch_shapes=[tile_f32, tile_f32, span_f32, span_f32, span_bf16, span_bf16, tile_f32, tile_f32, tile_f32,
                        per_pattern, per_pattern, per_pattern],
        compiler_params=_params("parallel", "parallel", "parallel"),
        name="dilated_attention",
    )(att, att, att, att, att, att, att, bias)


def _mix(x, att_bias, norm1_g, w_in, conv_w, a_log_fwd, dt_bias_fwd, a_log_bwd, dt_bias_bwd,
         dn_norm_g, w_out, norm2_g, w_router, pending=None):
    bsz, s, d = x.shape
    xf = x.reshape(bsz * s, d)
    q, k, v, kt, z, ba, att = _in_proj(xf, norm1_g[None], w_in, conv_w, s)
    if pending is not None:
        pending, q = lax.optimization_barrier((pending, q))
    shp = lambda a: a.reshape(bsz, s, a.shape[-1])
    o_f, o_b = _deltanet(shp(q), shp(k), shp(v), kt.reshape(bsz, s // CHUNK, DN_WIDTH, CHUNK), shp(ba),
                         a_log_fwd, dt_bias_fwd, a_log_bwd, dt_bias_bwd)
    att_out = _dilated_attention(shp(att), att_bias)
    flat = lambda a: a.reshape(bsz * s, a.shape[-1])
    x1, h2, afft = _out_proj(flat(o_f), flat(o_b), z, dn_norm_g[None], flat(att_out), xf, w_out,
                             norm2_g[None], w_router)
    return x1, h2, afft, pending


def kernel(x_prompt, x_sample, rel_bias, norm1_g, w_in, conv_w, a_log_fwd, dt_bias_fwd, a_log_bwd,
           dt_bias_bwd, dn_norm_g, w_out, norm2_g, w_router, w_gate, w_up, w_down, final_norm_g):
    groups = [x_prompt, x_sample]
    att_bias = _attn_bias_table(rel_bias)
    parts, xes = [], []
    rows, h2_prev = None, None
    for k in range(len(groups)):
        if rows is not None:
            rows, groups[k] = lax.optimization_barrier((rows, groups[k]))
        x = groups[k]
        x1, h2, afft, rows = _mix(x, att_bias, norm1_g[0], w_in[0], conv_w[0], a_log_fwd[0], dt_bias_fwd[0],
                                  a_log_bwd[0], dt_bias_bwd[0], dn_norm_g[0], w_out[0], norm2_g[0], w_router[0],
                                  pending=rows)
        if rows is not None:
            xes.append(_sc_gather(h2_prev, rows))
        n_tok = x1.shape[0]
        cap = CAPACITY_FACTOR * n_tok // N_EXPERTS
        pos, idx, off, gate = _route(afft, cap)
        rows, h2_prev = _gather_rows(idx.reshape(-1), GATHER_SPLIT, n_tok), h2
        parts.append((x, x1, pos, off, gate))
    rows, xes[0] = lax.optimization_barrier((rows, xes[0]))
    xes.append(_sc_gather(h2_prev, rows))
    outs = []
    for k, (x, x1, pos, off, gate) in enumerate(parts):
        ye = _expert_ffn(xes[k], gate, w_gate[0], w_up[0], w_down[0])
        if k + 1 < len(parts):
            xes[k + 1], ye = lax.optimization_barrier((xes[k + 1], ye))
        outs.append(_combine(x1, pos, off, ye, final_norm_g[None]).reshape(groups[k].shape))
    return tuple(outs)
```

```python
import functools
import math

import jax
import jax.numpy as jnp
from jax import lax
from jax.experimental import pallas as pl
from jax.experimental.pallas import tpu as pltpu
from jax.experimental.pallas import tpu_sc as plsc

D_MODEL = 1024
DN_HEADS = 4
DN_HEAD_DIM = 128
DN_WIDTH = DN_HEADS * DN_HEAD_DIM
ATT_HEADS = 8
ATT_HEAD_DIM = 64
ATT_WIDTH = ATT_HEADS * ATT_HEAD_DIM
CONV_K = 3
CHUNK = 64
DILATED_PATTERNS = ((128, 1), (512, 4), (2048, 16))
BAND = 64
NUM_BUCKETS = 32
MAX_DISTANCE = 1024
N_EXPERTS = 16
CAPACITY_FACTOR = 2
EXPERT_D_FF = 2816
EPS = 1e-6
NEG = -1e30
GATE_COLS = 4 * DN_HEADS
LANE = 128
SUBLANE = 8
GATHER_WINDOW = 128
GATHER_SPLIT = 4

F32 = jnp.float32
BF16 = jnp.bfloat16

VMEM_LIMIT = 56 * 1024 * 1024


def _params(*sem):
    return pltpu.CompilerParams(dimension_semantics=sem, vmem_limit_bytes=VMEM_LIMIT)


def _in_proj_kernel(x_ref, xp_ref, xn_ref, g_ref, wqkv_ref, wz_ref, wba_ref, watt_ref, cw_ref,
                    q_ref, k_ref, v_ref, kt_ref, z_ref, ba_ref, att_ref, *, tiles_per_seq):
    t = pl.program_id(0) % tiles_per_seq
    tm = x_ref.shape[0]
    x = jnp.concatenate([xp_ref[...], x_ref[...], xn_ref[...]], axis=0)
    h = x * lax.rsqrt(jnp.mean(x * x, axis=-1, keepdims=True) + EPS) * g_ref[...]
    hb = h[SUBLANE:SUBLANE + tm].astype(BF16)
    z_ref[...] = jnp.dot(hb, wz_ref[...], preferred_element_type=F32)
    ba_ref[...] = jnp.dot(hb, wba_ref[...], preferred_element_type=F32)
    att_ref[...] = jnp.dot(hb, watt_ref[...], preferred_element_type=F32)

    p = jnp.dot(h.astype(BF16), wqkv_ref[...], preferred_element_type=F32)
    rows = p.shape[0]
    row = lax.broadcasted_iota(jnp.int32, (tm, 1), 0)
    mid = lambda a: a[SUBLANE:SUBLANE + tm]
    p_prev = jnp.where((row == 0) & (t == 0), 0.0, mid(pltpu.roll(p, 1, axis=0)))
    p_next = jnp.where((row == tm - 1) & (t == tiles_per_seq - 1), 0.0, mid(pltpu.roll(p, rows - 1, axis=0)))
    y = cw_ref[0:1, :] * p_prev + cw_ref[1:2, :] * mid(p) + cw_ref[2:3, :] * p_next
    y = y * jax.nn.sigmoid(y)
    for hd in range(DN_HEADS):
        for part, ref, scale in ((0, q_ref, DN_HEAD_DIM ** -0.5), (1, k_ref, 1.0)):
            c0 = part * DN_WIDTH + hd * DN_HEAD_DIM
            a = y[:, c0:c0 + DN_HEAD_DIM]
            a = a * (lax.rsqrt(jnp.sum(a * a, axis=-1, keepdims=True) + EPS) * scale)
            ref[:, hd * DN_HEAD_DIM:(hd + 1) * DN_HEAD_DIM] = a.astype(ref.dtype)
            if part == 1:
                for j in range(tm // CHUNK):
                    kt_ref[j, hd * DN_HEAD_DIM:(hd + 1) * DN_HEAD_DIM, :] = (
                        a[j * CHUNK:(j + 1) * CHUNK, :].T.astype(kt_ref.dtype))
    v_ref[...] = y[:, 2 * DN_WIDTH:].astype(v_ref.dtype)


def _in_proj(x, g, w_in, conv_w, seq_len, tm=512):
    n = x.shape[0]
    o_qkv = 3 * DN_WIDTH
    o_z = o_qkv + DN_WIDTH
    o_ba = o_z + GATE_COLS
    wb = w_in.astype(BF16)
    wqkv = wb[:, :o_qkv]
    wz = wb[:, o_qkv:o_z]
    wba = jnp.pad(wb[:, o_z:o_ba], ((0, 0), (0, LANE - GATE_COLS)))
    watt = wb[:, o_ba:]
    per = tm // SUBLANE
    nsub = n // SUBLANE
    full = lambda a: pl.BlockSpec(a.shape, lambda i: (0, 0))
    row = lambda w: pl.BlockSpec((tm, w), lambda i: (i, 0))
    dn = jax.ShapeDtypeStruct((n, DN_WIDTH), BF16)
    return pl.pallas_call(
        functools.partial(_in_proj_kernel, tiles_per_seq=seq_len // tm),
        grid=(n // tm,),
        in_specs=[row(D_MODEL),
                  pl.BlockSpec((SUBLANE, D_MODEL), lambda i: (jnp.maximum(i * per - 1, 0), 0)),
                  pl.BlockSpec((SUBLANE, D_MODEL), lambda i: (jnp.minimum((i + 1) * per, nsub - 1), 0)),
                  full(g), full(wqkv), full(wz), full(wba), full(watt), full(conv_w)],
        out_specs=[row(DN_WIDTH), row(DN_WIDTH), row(DN_WIDTH),
                   pl.BlockSpec((tm // CHUNK, DN_WIDTH, CHUNK), lambda i: (i, 0, 0)),
                   row(DN_WIDTH), row(LANE), row(3 * ATT_WIDTH)],
        out_shape=[dn, dn, dn, jax.ShapeDtypeStruct((n // CHUNK, DN_WIDTH, CHUNK), BF16),
                   jax.ShapeDtypeStruct((n, DN_WIDTH), F32),
                   jax.ShapeDtypeStruct((n, LANE), F32),
                   jax.ShapeDtypeStruct((n, 3 * ATT_WIDTH), F32)],
        compiler_params=_params("parallel"),
        name="in_proj",
    )(x, x, x, g, wqkv, wz, wba, watt, conv_w)


def _split(a):
    hi = a.astype(BF16)
    return hi, (a - hi.astype(F32)).astype(BF16)


def _out_proj_kernel(of_ref, ob_ref, z_ref, gdn_ref, att_ref, x_ref, wdn_ref, watt_ref, g_ref, wr_ref,
                     x1_ref, h2_ref, afft_ref):
    dot = lambda a, b: jnp.dot(a, b, preferred_element_type=F32)
    o = of_ref[...].astype(F32) + ob_ref[...].astype(F32)
    z = z_ref[...]
    gated = []
    for h in range(DN_HEADS):
        hs = slice(h * DN_HEAD_DIM, (h + 1) * DN_HEAD_DIM)
        oh = o[:, hs]
        oh = oh * lax.rsqrt(jnp.mean(oh * oh, axis=-1, keepdims=True) + EPS) * gdn_ref[...]
        zh = z[:, hs]
        gated.append((oh * (zh * jax.nn.sigmoid(zh))).astype(BF16))
    dn = jnp.concatenate(gated, axis=1)
    x1 = x_ref[...] + dot(dn, wdn_ref[...]) + dot(att_ref[...], watt_ref[...])
    x1_ref[...] = x1
    h2 = x1 * lax.rsqrt(jnp.mean(x1 * x1, axis=-1, keepdims=True) + EPS) * g_ref[...]
    rows = h2.shape[0]
    piece = D_MODEL // GATHER_SPLIT
    for q in range(GATHER_SPLIT):
        h2_ref[q] = h2[:, q * piece:(q + 1) * piece]
    hh, hl = _split(h2)
    wh, wl = _split(wr_ref[...])
    both = dot(hh, jnp.concatenate([wh, wl], axis=1))
    logits = both[:, :LANE] + (both[:, LANE:] + dot(hl, wh))
    lane = lax.broadcasted_iota(jnp.int32, logits.shape, 1)
    logits = jnp.where(lane < N_EXPERTS, logits, NEG)
    p = jnp.exp(logits - jnp.max(logits, axis=-1, keepdims=True))
    aff = p / jnp.sum(p, axis=-1, keepdims=True)
    for j in range(rows // LANE):
        afft_ref[:, j, :] = aff[j * LANE:(j + 1) * LANE, :].T[:N_EXPERTS, :]


def _out_proj(o_f, o_b, z, dn_norm_g, att, x, w_out, g, w_router, tm=1024):
    n = x.shape[0]
    wb = w_out.astype(BF16)
    wdn, watt = wb[:DN_WIDTH], wb[DN_WIDTH:]
    wr = jnp.pad(w_router.astype(F32), ((0, 0), (0, LANE - N_EXPERTS)))
    full = lambda a: pl.BlockSpec(a.shape, lambda i: (0, 0))
    row = lambda w: pl.BlockSpec((tm, w), lambda i: (i, 0))
    return pl.pallas_call(
        _out_proj_kernel,
        grid=(n // tm,),
        in_specs=[row(DN_WIDTH), row(DN_WIDTH), row(DN_WIDTH), full(dn_norm_g), row(ATT_WIDTH),
                  row(D_MODEL), full(wdn), full(watt), full(g), full(wr)],
        out_specs=[row(D_MODEL),
                   pl.BlockSpec((GATHER_SPLIT, tm, D_MODEL // GATHER_SPLIT), lambda i: (0, i, 0)),
                   pl.BlockSpec((N_EXPERTS, tm // LANE, LANE), lambda i: (0, i, 0))],
        out_shape=[jax.ShapeDtypeStruct((n, D_MODEL), F32),
                   jax.ShapeDtypeStruct((GATHER_SPLIT, n, D_MODEL // GATHER_SPLIT), F32),
                   jax.ShapeDtypeStruct((N_EXPERTS, n // LANE, LANE), F32)],
        compiler_params=_params("parallel"),
        name="out_proj",
    )(o_f, o_b, z, dn_norm_g, att, x, wdn, watt, g, wr)


def _route_kernel(aff_ref, pos_ref, idx_ref, off_ref, gate_ref, *, cap):
    x = aff_ref[0]
    nb = x.shape[0]
    bits = pltpu.bitcast(x, jnp.int32)
    total = lambda m: jnp.sum(jnp.sum(m, axis=0, keepdims=True), axis=1, keepdims=True)
    dot = lambda a, b: jnp.dot(a, b, preferred_element_type=F32)
    one_hot = lambda m: jnp.where(m, 1.0, 0.0)

    def bit_step(i, prefix):
        cand = prefix | lax.shift_left(jnp.int32(1), 30 - i)
        cnt = total(one_hot(bits >= cand))
        return jnp.where(cnt >= cap, cand, prefix)

    thr = lax.fori_loop(0, 31, bit_step, jnp.zeros((1, 1), jnp.int32))
    gt = bits > thr
    eq = bits == thr
    need = cap - total(one_hot(gt))

    li = lax.broadcasted_iota(jnp.int32, (LANE, LANE), 0)
    lj = lax.broadcasted_iota(jnp.int32, (LANE, LANE), 1)
    lane_incl = one_hot(li <= lj).astype(BF16)
    ones = jnp.ones((LANE, LANE), BF16)
    bi = lax.broadcasted_iota(jnp.int32, (nb, nb), 0)
    bj = lax.broadcasted_iota(jnp.int32, (nb, nb), 1)
    blk_before = one_hot(bj < bi).astype(BF16)

    def ranks(mask):
        m = one_hot(mask).astype(BF16)
        within = dot(m, lane_incl)
        tot = dot(m, ones)
        before = dot(blk_before, tot.astype(BF16))
        return within, tot, before

    w_eq, _, b_eq = ranks(eq)
    sel = gt | (eq & (w_eq + b_eq <= need))
    within, tot, before = ranks(sel)
    pos_ref[0] = jnp.where(sel, within + before - 1.0, -1.0)
    off_ref[0] = before.T[0:SUBLANE, :].astype(jnp.int32)

    within_t = within.T.astype(BF16)
    sel_t = one_hot(sel).T.astype(BF16)
    x_t = x.T
    x_parts = []
    for _ in range(3):
        part = x_t.astype(BF16)
        x_parts.append(part)
        x_t = x_t - part.astype(F32)
    after = before + tot
    blk_id = lax.broadcasted_iota(jnp.int32, (nb, LANE), 0).astype(F32)
    lane_id = lax.broadcasted_iota(jnp.int32, (LANE, LANE), 0).astype(F32)
    for c in range(cap // LANE):
        j = (lax.broadcasted_iota(jnp.int32, (1, LANE), 1) + c * LANE).astype(F32)
        holds = one_hot((before <= j) & (j < after))
        local = j - jnp.sum(holds * before, axis=0, keepdims=True) + 1.0
        blk = jnp.sum(holds * blk_id, axis=0, keepdims=True)
        hb = holds.astype(BF16)
        cnt_in_blk = dot(within_t, hb)
        sel_in_blk = dot(sel_t, hb)
        hit = one_hot((sel_in_blk > 0.5) & (cnt_in_blk == local))
        lane_of = jnp.sum(hit * lane_id, axis=0, keepdims=True)
        idx_ref[0, c:c + 1, :] = (blk * LANE + lane_of).astype(jnp.int32)
        aff_in_blk = dot(x_parts[0], hb) + dot(x_parts[1], hb) + dot(x_parts[2], hb)
        gate_ref[0, c:c + 1, :] = jnp.sum(hit * aff_in_blk, axis=0, keepdims=True)


def _route(aff3, cap):
    e, nb, _ = aff3.shape
    blk = lambda rows: pl.BlockSpec((1, rows, LANE), lambda i: (i, 0, 0))
    return pl.pallas_call(
        functools.partial(_route_kernel, cap=cap),
        grid=(e,),
        in_specs=[blk(nb)],
        out_specs=[blk(nb), blk(cap // LANE), pl.BlockSpec((1, SUBLANE, nb), lambda i: (i, 0, 0)),
                   blk(cap // LANE)],
        out_shape=[jax.ShapeDtypeStruct((e, nb, LANE), F32),
                   jax.ShapeDtypeStruct((e, cap // LANE, LANE), jnp.int32),
                   jax.ShapeDtypeStruct((e, SUBLANE, nb), jnp.int32),
                   jax.ShapeDtypeStruct((e, cap // LANE, LANE), F32)],
        compiler_params=_params("parallel"),
        name="route",
    )(aff3)


def _gather_rows(idx, pieces, n):
    return (jnp.arange(pieces, dtype=idx.dtype)[:, None] * n + idx[None, :]).reshape(1, -1)


def _sc_gather(x, rows):
    pieces, n, d = x.shape
    x = x.reshape(pieces * n, d)
    m = rows.shape[1]
    mesh = plsc.VectorSubcoreMesh(core_axis_name="c", subcore_axis_name="s")

    @pl.kernel(out_type=jax.ShapeDtypeStruct((m, d), x.dtype), mesh=mesh, scratch_types=[])
    def gather_kernel(x_hbm, i_hbm, o_hbm):
        def body(i_vmem, o_vmem):
            pltpu.sync_copy(x_hbm.at[i_vmem.at[0]], o_vmem)

        pltpu.emit_pipeline(
            body,
            grid=(m // GATHER_WINDOW,),
            in_specs=[pl.BlockSpec((1, GATHER_WINDOW), lambda i: (0, i))],
            out_specs=[pl.BlockSpec((GATHER_WINDOW, d), lambda i: (i, 0))],
            core_axis_name=("c", "s"),
            dimension_semantics=(pltpu.PARALLEL,),
        )(i_hbm, o_hbm)

    return gather_kernel(x, rows).reshape(pieces, m // pieces, d)


def _ffn_kernel(x_ref, gate_ref, wg_ref, wu_ref, wd_ref, ye_ref, xe_ref, acc_ref):
    f = pl.program_id(2)
    pieces, _, piece = x_ref.shape

    @pl.when(f == 0)
    def _():
        acc_ref[...] = jnp.zeros_like(acc_ref)
        for q in range(pieces):
            xe_ref[:, q * piece:(q + 1) * piece] = x_ref[q].astype(BF16)

    xe = xe_ref[...]
    a = jnp.dot(xe, wg_ref[0].astype(BF16), preferred_element_type=F32)
    b = jnp.dot(xe, wu_ref[0].astype(BF16), preferred_element_type=F32)
    h = (a * jax.nn.sigmoid(a) * b).astype(BF16)
    acc_ref[...] += jnp.dot(h, wd_ref[0].astype(BF16), preferred_element_type=F32)

    @pl.when(f == pl.num_programs(2) - 1)
    def _():
        for c in range(gate_ref.shape[1]):
            rows = slice(c * LANE, (c + 1) * LANE)
            gate = jnp.broadcast_to(gate_ref[0, c:c + 1, :], (LANE, LANE)).T
            ye_ref[0, rows, :] = (acc_ref[rows, :] * jnp.concatenate([gate] * (acc_ref.shape[1] // LANE), axis=1)
                                  ).astype(ye_ref.dtype)


def _expert_ffn(xe, gate, w_gate, w_up, w_down, tm=2048, tf=256):
    e = w_gate.shape[0]
    pieces, rows, piece = xe.shape
    cap, d = rows // e, piece * pieces
    dff = w_gate.shape[-1]
    tiles = cap // tm
    return pl.pallas_call(
        _ffn_kernel,
        grid=(e, tiles, dff // tf),
        in_specs=[pl.BlockSpec((pieces, tm, piece), lambda i, r, f: (0, i * tiles + r, 0)),
                  pl.BlockSpec((1, tm // LANE, LANE), lambda i, r, f: (i, r, 0)),
                  pl.BlockSpec((1, d, tf), lambda i, r, f: (i, 0, f)),
                  pl.BlockSpec((1, d, tf), lambda i, r, f: (i, 0, f)),
                  pl.BlockSpec((1, tf, d), lambda i, r, f: (i, f, 0))],
        out_specs=pl.BlockSpec((1, tm, d), lambda i, r, f: (i, r, 0)),
        out_shape=jax.ShapeDtypeStruct((e, cap, d), BF16),
        scratch_shapes=[pltpu.VMEM((tm, d), BF16), pltpu.VMEM((tm, d), F32)],
        compiler_params=_params("parallel", "parallel", "arbitrary"),
        name="expert_ffn",
    )(xe, gate, w_gate, w_up, w_down)


COMBINE_TAIL = 16
COMBINE_ROWS = 2 * LANE


def _combine_kernel(c0_ref, r0_ref, spill_ref, x1_ref, pos_ref, g_ref, *rest):
    ye_refs, y_ref = rest[:N_EXPERTS], rest[N_EXPERTS]
    i = pl.program_id(0)
    nb = pl.num_programs(0)
    t, d = x1_ref.shape
    pos_t = jnp.concatenate([pos_ref[:, 0, :], jnp.zeros((LANE - N_EXPERTS, LANE), F32)], axis=0).T
    col = lax.broadcasted_iota(jnp.int32, (t, LANE), 1)

    def gathered(n_rows, shift):
        out = jnp.zeros((t, d), F32)
        for e0 in range(0, N_EXPERTS, 2):
            picks, rows = [], []
            for e in (e0, e0 + 1):
                first = r0_ref[e * nb + i] + shift
                slot = jnp.broadcast_to(pos_t[:, e:e + 1], (t, LANE))
                picks.append(jnp.where(slot == (col + first).astype(F32), 1.0, 0.0).astype(BF16))
                start = pl.multiple_of(first - c0_ref[e * nb + i], 16)
                r = ye_refs[e][pl.ds(start, n_rows), :]
                if n_rows < LANE:
                    r = jnp.concatenate([r, jnp.zeros((LANE - n_rows, d), BF16)], axis=0)
                rows.append(r)
            out = out + jnp.dot(jnp.concatenate(picks, axis=1), jnp.concatenate(rows, axis=0),
                                preferred_element_type=F32)
        return out

    def finish(x):
        y_ref[...] = x * lax.rsqrt(jnp.mean(x * x, axis=-1, keepdims=True) + EPS) * g_ref[...]

    acc = x1_ref[...] + gathered(LANE, 0)
    finish(acc)

    @pl.when(spill_ref[i] != 0)
    def _():
        finish(acc + gathered(COMBINE_TAIL, LANE))


def _combine(x1, pos3, off, ye, final_g):
    n, d = x1.shape
    nb = n // LANE
    cap = ye.shape[1]
    first = off[:, 0, :]
    end = jnp.concatenate([first[:, 1:], jnp.full((N_EXPERTS, 1), cap, first.dtype)], axis=1)
    c0 = jnp.minimum(first // LANE * LANE, cap - COMBINE_ROWS)
    r0 = jnp.minimum(first // 16 * 16, cap - LANE - COMBINE_TAIL)
    spill = jnp.any(end > r0 + LANE, axis=0).astype(jnp.int32)
    pos4 = pos3.reshape(N_EXPERTS, nb, 1, LANE)
    window = lambda e: pl.BlockSpec(
        (None, pl.Element(COMBINE_ROWS), pl.Element(d)),
        lambda i, c0_ref, r0_ref, spill_ref: (e, pl.multiple_of(c0_ref[e * nb + i], LANE), 0))
    grid_spec = pltpu.PrefetchScalarGridSpec(
        num_scalar_prefetch=3,
        grid=(nb,),
        in_specs=[pl.BlockSpec((LANE, d), lambda i, c, r, s: (i, 0)),
                  pl.BlockSpec((N_EXPERTS, None, 1, LANE), lambda i, c, r, s: (0, i, 0, 0)),
                  pl.BlockSpec((1, d), lambda i, c, r, s: (0, 0))]
                 + [window(e) for e in range(N_EXPERTS)],
        out_specs=pl.BlockSpec((LANE, d), lambda i, c, r, s: (i, 0)),
    )
    flat = lambda a: a.reshape(-1).astype(jnp.int32)
    return pl.pallas_call(
        _combine_kernel,
        grid_spec=grid_spec,
        out_shape=jax.ShapeDtypeStruct((n, d), F32),
        compiler_params=_params("arbitrary"),
        name="moe_combine",
    )(flat(c0), flat(r0), spill, x1, pos4, final_g, *([ye] * N_EXPERTS))


def _lane_bcast(a, col, width=LANE):
    return jnp.broadcast_to(a[:, col:col + 1], (a.shape[0], width))


DN_TILE = 512
DN_ROWS = 2
INV_STEPS = 5
assert 2 ** (INV_STEPS + 1) == CHUNK
DN_UNROLL = 64
DN_BUILD_UNROLL = 4


def _dn_scan_kernel(q_ref, k_ref, v_ref, kt_ref, ba_ref, alog_ref, dtb_ref, o_ref,
                    s_ref, xbuf, tbuf, abuf, rbuf, uwbuf, qgbuf, kdbuf, lbuf, nbuf, obuf, ecbuf, *, reverse):
    @pl.when(pl.program_id(1) == 0)
    def _():
        s_ref[...] = jnp.zeros_like(s_ref)

    nrow, tile = q_ref.shape[0], q_ref.shape[1]
    nc = tile // CHUNK
    n_prob = nrow * nc * DN_HEADS
    unroll = min(DN_UNROLL, n_prob)
    beta_col = DN_HEADS if reverse else 0
    a_col = 2 * DN_HEADS + (DN_HEADS if reverse else 0)
    ri = lax.broadcasted_iota(jnp.int32, (CHUNK, CHUNK), 0)
    ci = lax.broadcasted_iota(jnp.int32, (CHUNK, CHUNK), 1)
    tri = (ri <= ci) if reverse else (ri >= ci)
    strict = (ri < ci) if reverse else (ri > ci)
    eye = jnp.where(ri == ci, 1.0, 0.0)
    row = lax.broadcasted_iota(jnp.int32, (CHUNK, LANE), 0)
    a_scale = jnp.exp(alog_ref[...])
    dtb = dtb_ref[...]
    dot = lambda a, b: jnp.dot(a, b, preferred_element_type=F32)

    def build(bc, carry):
        b, c = bc // nc, bc % nc
        rows = pl.ds(pl.multiple_of(c * CHUNK, CHUNK), CHUNK)
        raw = ba_ref[b, rows, :]
        beta_all = jax.nn.sigmoid(raw)
        xs = raw + dtb
        g = -a_scale * (jnp.maximum(xs, 0.0) + jnp.log(1.0 + jnp.exp(-jnp.abs(xs))))
        gs = g
        sh = 1
        while sh < CHUNK:
            if reverse:
                gs = gs + jnp.where(row < CHUNK - sh, pltpu.roll(gs, CHUNK - sh, axis=0), 0.0)
            else:
                gs = gs + jnp.where(row >= sh, pltpu.roll(gs, sh, axis=0), 0.0)
            sh *= 2
        gs_t = gs.T
        g_last = gs[0:1, :] if reverse else gs[CHUNK - 1:CHUNK, :]
        eg_all = jnp.exp(gs)
        ek_t = jnp.exp(g_last.T - gs_t)
        ecbuf[bc] = jnp.broadcast_to(jnp.exp(g_last), (SUBLANE, LANE))
        for h in range(DN_HEADS):
            p = bc * DN_HEADS + h
            hs = slice(h * DN_HEAD_DIM, (h + 1) * DN_HEAD_DIM)
            q = q_ref[b, rows, hs].astype(F32)
            k = k_ref[b, rows, hs]
            v = v_ref[b, rows, hs].astype(F32)
            beta = _lane_bcast(beta_all, beta_col + h)
            eg = _lane_bcast(eg_all, a_col + h)
            diff = _lane_bcast(gs, a_col + h, CHUNK) - gs_t[a_col + h:a_col + h + 1, :]
            decay = jnp.where(tri, jnp.exp(diff), 0.0)
            kb = k.astype(F32) * beta
            kq = lax.dot_general(jnp.concatenate([kb, q], axis=0).astype(BF16), k,
                                 (((1,), (1,)), ((), ())), preferred_element_type=F32)
            lower = jnp.where(strict, kq[:CHUNK] * decay, 0.0)
            xbuf[p] = (-lower).astype(BF16)
            tbuf[p] = eye - lower
            abuf[p] = (kq[CHUNK:] * decay).astype(BF16)
            rbuf[p] = jnp.concatenate([v * beta, kb * eg], axis=1).astype(BF16)
            qgbuf[p] = q * eg
            kdbuf[p] = (kt_ref[b, c, hs, :].astype(F32)
                        * ek_t[a_col + h:a_col + h + 1, :]).astype(BF16)
        return carry

    lax.fori_loop(0, nrow * nc, build, 0, unroll=DN_BUILD_UNROLL)

    for step in range(INV_STEPS + 1):
        def double(p, carry, first=(step == 0), last=(step == INV_STEPS)):
            x = xbuf[p]
            if not first:
                t = tbuf[p]
                tbuf[p] = t + dot(t.astype(BF16), x)
            if not last:
                xbuf[p] = dot(x, x).astype(BF16)
            return carry
        lax.fori_loop(0, n_prob, double, 0, unroll=unroll)

    def solve(p, carry):
        uwbuf[p] = dot(tbuf[p].astype(BF16), rbuf[p]).astype(BF16)
        return carry

    lax.fori_loop(0, n_prob, solve, 0, unroll=unroll)

    def fold(p, carry):
        uw = uwbuf[p]
        a_uw = dot(abuf[p], uw)
        k_uw = dot(kdbuf[p], uw)
        obuf[p] = a_uw[:, :DN_HEAD_DIM]
        nbuf[p] = k_uw[:, :DN_HEAD_DIM]
        lbuf[p, :DN_HEAD_DIM, :] = k_uw[:, DN_HEAD_DIM:].astype(BF16)
        lbuf[p, DN_HEAD_DIM:, :] = (qgbuf[p] - a_uw[:, DN_HEAD_DIM:]).astype(BF16)
        return carry

    lax.fori_loop(0, n_prob, fold, 0, unroll=unroll)

    def scan(jc, carry):
        c = (nc - 1 - jc) if reverse else jc
        rows = pl.ds(pl.multiple_of(c * CHUNK, CHUNK), CHUNK)
        for b in range(nrow):
            ec_all = ecbuf[b * nc + c]
            for h in range(DN_HEADS):
                p = (b * nc + c) * DN_HEADS + h
                state = s_ref[b * DN_HEADS + h]
                r = dot(lbuf[p], state.astype(BF16))
                s_ref[b * DN_HEADS + h] = (state * _lane_bcast(ec_all[0:1], a_col + h)
                                           - r[:DN_HEAD_DIM] + nbuf[p])
                o = r[DN_HEAD_DIM:] + obuf[p]
                o_ref[b, rows, h * DN_HEAD_DIM:(h + 1) * DN_HEAD_DIM] = o.astype(o_ref.dtype)
        return carry

    lax.fori_loop(0, nc, scan, 0)


def _dn_scan(q, k, v, kt, ba, a_log_row, dt_bias_row, reverse, tile=DN_TILE, nrow=DN_ROWS):
    b, s, _ = q.shape
    assert b % nrow == 0
    tile = min(tile, s)
    nt = s // tile
    nc = tile // CHUNK
    n_prob = nrow * nc * DN_HEADS
    tmap = (lambda bi, ti: (bi, nt - 1 - ti, 0)) if reverse else (lambda bi, ti: (bi, ti, 0))
    wide = pl.BlockSpec((nrow, tile, DN_WIDTH), tmap)
    par = pl.BlockSpec((1, LANE), lambda bi, ti: (0, 0))
    return pl.pallas_call(
        functools.partial(_dn_scan_kernel, reverse=reverse),
        grid=(b // nrow, nt),
        in_specs=[wide, wide, wide,
                  pl.BlockSpec((nrow, nc, DN_WIDTH, CHUNK), lambda bi, ti: tmap(bi, ti) + (0,)),
                  pl.BlockSpec((nrow, tile, LANE), tmap), par, par],
        out_specs=wide,
        out_shape=jax.ShapeDtypeStruct((b, s, DN_WIDTH), BF16),
        scratch_shapes=[pltpu.VMEM((nrow * DN_HEADS, DN_HEAD_DIM, DN_HEAD_DIM), F32),
                        pltpu.VMEM((n_prob, CHUNK, CHUNK), BF16),
                        pltpu.VMEM((n_prob, CHUNK, CHUNK), F32),
                        pltpu.VMEM((n_prob, CHUNK, CHUNK), BF16),
                        pltpu.VMEM((n_prob, CHUNK, 2 * DN_HEAD_DIM), BF16),
                        pltpu.VMEM((n_prob, CHUNK, 2 * DN_HEAD_DIM), BF16),
                        pltpu.VMEM((n_prob, CHUNK, DN_HEAD_DIM), F32),
                        pltpu.VMEM((n_prob, DN_HEAD_DIM, CHUNK), BF16),
                        pltpu.VMEM((n_prob, DN_HEAD_DIM + CHUNK, DN_HEAD_DIM), BF16),
                        pltpu.VMEM((n_prob, DN_HEAD_DIM, DN_HEAD_DIM), F32),
                        pltpu.VMEM((n_prob, CHUNK, DN_HEAD_DIM), F32),
                        pltpu.VMEM((nrow * nc, SUBLANE, LANE), F32)],
        compiler_params=_params("parallel", "arbitrary"),
        name="dn_scan_bwd" if reverse else "dn_scan_fwd",
    )(q, k, v, kt, ba, a_log_row, dt_bias_row)


def _gate_rows(a_log_fwd, a_log_bwd, dt_bias_fwd, dt_bias_bwd):
    pad = lambda f, b: jnp.pad(jnp.concatenate([f, b]).astype(F32), (2 * DN_HEADS, LANE - 4 * DN_HEADS))[None]
    return pad(a_log_fwd, a_log_bwd), pad(dt_bias_fwd, dt_bias_bwd)


def _deltanet(q, k, v, kt, ba, a_log_fwd, dt_bias_fwd, a_log_bwd, dt_bias_bwd):
    alog, dtb = _gate_rows(a_log_fwd, a_log_bwd, dt_bias_fwd, dt_bias_bwd)
    o_f = _dn_scan(q, k, v, kt, ba, alog, dtb, reverse=False)
    o_b = _dn_scan(q, k, v, kt, ba, alog, dtb, reverse=True)
    return o_f, o_b


def _t5_bucket(rel):
    nb = NUM_BUCKETS // 2
    ret = jnp.where(rel > 0, nb, 0)
    n = jnp.abs(rel)
    max_exact = nb // 2
    nf = jnp.maximum(n, 1).astype(F32)
    large = max_exact + (jnp.log(nf / max_exact) / math.log(MAX_DISTANCE / max_exact)
                         * (nb - max_exact)).astype(jnp.int32)
    large = jnp.minimum(large, nb - 1)
    return ret + jnp.where(n < max_exact, n, large)


ATT_TILE = 2048
ATT_HALO = BAND * max(d for _, d in DILATED_PATTERNS)
QBLK = 2 * BAND
KBLK = QBLK + 2 * BAND


def _attn_bias_table(rel_bias):
    delta = (jnp.arange(KBLK) - BAND)[None, :] - jnp.arange(QBLK)[:, None]
    tabs = []
    for window, dil in DILATED_PATTERNS:
        half = window // (2 * dil)
        onehot = (_t5_bucket(delta * dil)[..., None] == jnp.arange(NUM_BUCKETS)).astype(F32)
        bias = jnp.einsum('qkb,bh->hqk', onehot, rel_bias.astype(F32),
                          precision=lax.Precision.HIGHEST)
        tabs.append(jnp.where((jnp.abs(delta) <= half)[None], bias, NEG))
    return jnp.stack(tabs)


def _attn_kernel(q_ref, kp_ref, kc_ref, kn_ref, vp_ref, vc_ref, vn_ref, bias_ref, o_ref,
                 qd_a, qd_b, kf, vf, kd, vd, od, md, ld, o_sc, m_sc, l_sc, *, seq_len):
    t = pl.program_id(1)
    lane = lax.broadcasted_iota(jnp.int32, (QBLK, LANE), 1)
    low = lane < ATT_HEAD_DIM
    scale = ATT_HEAD_DIM ** -0.5
    kcol = lax.broadcasted_iota(jnp.int32, (1, KBLK), 1)
    ones = jnp.ones((KBLK, LANE), BF16)

    for p, (window, dil) in enumerate(DILATED_PATTERNS):
        lt = ATT_TILE // dil
        lh = ATT_HALO // dil
        span = lt + 2 * lh
        nblk = lt // QBLK
        n_pos = seq_len // dil
        qd, q_prev = (qd_a, qd_b) if p % 2 == 0 else (qd_b, qd_a)
        prev_dil = DILATED_PATTERNS[p - 1][1] if p > 0 else 1
        nested = p > 0 and prev_dil > 1 and dil % prev_dil == 0
        step = dil // prev_dil
        keep_f32 = p + 1 < len(DILATED_PATTERNS) and dil > 1 and DILATED_PATTERNS[p + 1][1] % dil == 0
        for r in range(dil):
            if nested:
                rp, a = r % prev_dil, r // prev_dil
                lt_p, span_p = ATT_TILE // prev_dil, (ATT_TILE + 2 * ATT_HALO) // prev_dil
                qd[pl.ds(r * lt, lt), :] = q_prev[pl.ds(rp * lt_p + a, lt, stride=step), :]
                src = pl.ds(rp * span_p + a, span, stride=step)
                kd[pl.ds(r * span, span), :] = kf[src, :].astype(BF16)
                vd[pl.ds(r * span, span), :] = vf[src, :].astype(BF16)
                continue
            sl = lambda n: pl.ds(r, n, stride=dil) if dil > 1 else pl.ds(0, n)
            qd[pl.ds(r * lt, lt), :] = q_ref[0, sl(lt), :] * scale
            off = r * span
            for (kr, vr, n) in ((kp_ref, vp_ref, lh), (kc_ref, vc_ref, lt), (kn_ref, vn_ref, lh)):
                kx, vx = kr[0, sl(n), :], vr[0, sl(n), :]
                if keep_f32:
                    kf[pl.ds(off, n), :] = kx
                    vf[pl.ds(off, n), :] = vx
                kd[pl.ds(off, n), :] = kx.astype(BF16)
                vd[pl.ds(off, n), :] = vx.astype(BF16)
                off += n


        for i in range(dil * nblk):
            r, j = divmod(i, nblk)
            q0 = r * lt + j * QBLK
            k0 = r * span + lh + j * QBLK - BAND
            q = qd[pl.ds(q0, QBLK), :]
            k = kd[pl.ds(k0, KBLK), :]
            v1 = jnp.concatenate([vd[pl.ds(k0, KBLK), :], ones], axis=1)
            edge = j == 0 or j == nblk - 1
            if edge:
                pos = t * lt + j * QBLK - BAND + kcol
                valid = (pos >= 0) & (pos < n_pos)
            res = []
            for h in range(2):
                qh = jnp.where(low, q, 0.0) if h == 0 else jnp.where(low, 0.0, q)
                s = lax.dot_general(qh.astype(BF16), k, (((1,), (1,)), ((), ())),
                                    preferred_element_type=F32) + bias_ref[p, h]
                if edge:
                    s = jnp.where(valid, s, NEG)
                m = jnp.max(s, axis=-1, keepdims=True)
                e = jnp.exp(s - m).astype(BF16)
                res.append((jnp.dot(e, v1, preferred_element_type=F32), m))
            (pv0, m0), (pv1, m1) = res
            o_blk = jnp.where(low, pv0[:, :LANE], pv1[:, :LANE])
            l_blk = jnp.where(low, pv0[:, LANE:], pv1[:, LANE:])
            m_blk = jnp.where(low, m0, m1)
            if dil == 1:
                o_sc[p, pl.ds(q0, QBLK), :] = o_blk
                l_sc[p, pl.ds(q0, QBLK), :] = l_blk
                m_sc[p, pl.ds(q0, QBLK), :] = m_blk
            else:
                od[pl.ds(q0, QBLK), :] = o_blk
                ld[pl.ds(q0, QBLK), :] = l_blk
                md[pl.ds(q0, QBLK), :] = m_blk

        for r in range(dil if dil > 1 else 0):
            sl = pl.ds(r, lt, stride=dil)
            src = pl.ds(r * lt, lt)
            o_sc[p, sl, :] = od[src, :]
            m_sc[p, sl, :] = md[src, :]
            l_sc[p, sl, :] = ld[src, :]

    n_p = len(DILATED_PATTERNS)
    mx = m_sc[0]
    for p in range(1, n_p):
        mx = jnp.maximum(mx, m_sc[p])
    num = jnp.zeros((ATT_TILE, LANE), F32)
    den = jnp.zeros((ATT_TILE, LANE), F32)
    for p in range(n_p):
        w = jnp.exp(m_sc[p] - mx)
        num += o_sc[p] * w
        den += l_sc[p] * w
    o_ref[0] = (num / den).astype(o_ref.dtype)


def _dilated_attention(att, bias):
    b, s, _ = att.shape
    assert s % ATT_TILE == 0
    nt = s // ATT_TILE
    nh = s // ATT_HALO
    per = ATT_TILE // ATT_HALO
    hp = ATT_WIDTH // LANE
    cur = lambda col0: pl.BlockSpec((1, ATT_TILE, LANE), lambda bi, ti, hi: (bi, ti, col0 + hi))
    prev = lambda col0: pl.BlockSpec(
        (1, ATT_HALO, LANE), lambda bi, ti, hi: (bi, jnp.maximum(ti * per - 1, 0), col0 + hi))
    nxt = lambda col0: pl.BlockSpec(
        (1, ATT_HALO, LANE), lambda bi, ti, hi: (bi, jnp.minimum((ti + 1) * per, nh - 1), col0 + hi))
    n_p = len(DILATED_PATTERNS)
    tile_f32 = pltpu.VMEM((ATT_TILE, LANE), F32)
    span_bf16 = pltpu.VMEM((ATT_TILE + 2 * ATT_HALO, LANE), BF16)
    span_f32 = pltpu.VMEM((ATT_TILE + 2 * ATT_HALO, LANE), F32)
    per_pattern = pltpu.VMEM((n_p, ATT_TILE, LANE), F32)
    return pl.pallas_call(
        functools.partial(_attn_kernel, seq_len=s),
        grid=(b, nt, hp),
        in_specs=[cur(0), prev(hp), cur(hp), nxt(hp), prev(2 * hp), cur(2 * hp), nxt(2 * hp),
                  pl.BlockSpec((n_p, 2, QBLK, KBLK), lambda bi, ti, hi: (0, hi, 0, 0))],
        out_specs=pl.BlockSpec((1, ATT_TILE, LANE), lambda bi, ti, hi: (bi, ti, hi)),
        out_shape=jax.ShapeDtypeStruct((b, s, ATT_WIDTH), BF16),
        scratch_shapes=[tile_f32, tile_f32, span_f32, span_f32, span_bf16, span_bf16, tile_f32, tile_f32, tile_f32,
                        per_pattern, per_pattern, per_pattern],
        compiler_params=_params("parallel", "parallel", "parallel"),
        name="dilated_attention",
    )(att, att, att, att, att, att, att, bias)


def _mix(x, att_bias, norm1_g, w_in, conv_w, a_log_fwd, dt_bias_fwd, a_log_bwd, dt_bias_bwd,
         dn_norm_g, w_out, norm2_g, w_router, pending=None):
    bsz, s, d = x.shape
    xf = x.reshape(bsz * s, d)
    q, k, v, kt, z, ba, att = _in_proj(xf, norm1_g[None], w_in, conv_w, s)
    if pending is not None:
        pending, q = lax.optimization_barrier((pending, q))
    shp = lambda a: a.reshape(bsz, s, a.shape[-1])
    o_f, o_b = _deltanet(shp(q), shp(k), shp(v), kt.reshape(bsz, s // CHUNK, DN_WIDTH, CHUNK), shp(ba),
                         a_log_fwd, dt_bias_fwd, a_log_bwd, dt_bias_bwd)
    att_out = _dilated_attention(shp(att), att_bias)
    flat = lambda a: a.reshape(bsz * s, a.shape[-1])
    x1, h2, afft = _out_proj(flat(o_f), flat(o_b), z, dn_norm_g[None], flat(att_out), xf, w_out,
                             norm2_g[None], w_router)
    return x1, h2, afft, pending


def kernel(x_prompt, x_sample, rel_bias, norm1_g, w_in, conv_w, a_log_fwd, dt_bias_fwd, a_log_bwd,
           dt_bias_bwd, dn_norm_g, w_out, norm2_g, w_router, w_gate, w_up, w_down, final_norm_g):
    groups = [x_prompt, x_sample]
    att_bias = _attn_bias_table(rel_bias)
    parts, xes = [], []
    rows, h2_prev = None, None
    for k in range(len(groups)):
        if rows is not None:
            rows, groups[k] = lax.optimization_barrier((rows, groups[k]))
        x = groups[k]
        x1, h2, afft, rows = _mix(x, att_bias, norm1_g[0], w_in[0], conv_w[0], a_log_fwd[0], dt_bias_fwd[0],
                                  a_log_bwd[0], dt_bias_bwd[0], dn_norm_g[0], w_out[0], norm2_g[0], w_router[0],
                                  pending=rows)
        if rows is not None:
            xes.append(_sc_gather(h2_prev, rows))
        n_tok = x1.shape[0]
        cap = CAPACITY_FACTOR * n_tok // N_EXPERTS
        pos, idx, off, gate = _route(afft, cap)
        rows, h2_prev = _gather_rows(idx.reshape(-1), GATHER_SPLIT, n_tok), h2
        parts.append((x, x1, pos, off, gate))
    rows, xes[0] = lax.optimization_barrier((rows, xes[0]))
    xes.append(_sc_gather(h2_prev, rows))
    outs = []
    for k, (x, x1, pos, off, gate) in enumerate(parts):
        ye = _expert_ffn(xes[k], gate, w_gate[0], w_up[0], w_down[0])
        if k + 1 < len(parts):
            xes[k + 1], ye = lax.optimization_barrier((xes[k + 1], ye))
        outs.append(_combine(x1, pos, off, ye, final_norm_g[None]).reshape(groups[k].shape))
    return tuple(outs)
```
